```python
import math
import jax
import jax.numpy as jnp
from jax import lax
import numpy as np

D_MODEL = 2048
BATCH = 2
SEQ = 4096
DEPTH = 4

MEM_LEN = 256
Q_BLOCK = 128
D_FF = 5632
NORM_EPS = 1e-6
NEG_INF = -1e30
SEL_BIG = 1e9

DIFF_HEADS = 8
DIFF_QK_DIM = 64
DIFF_V_DIM = 2 * DIFF_QK_DIM

NSA_HEADS = 8
NSA_GROUPS = 2
NSA_HPG = NSA_HEADS // NSA_GROUPS
NSA_DIM = 128
CMP_LEN = 32
CMP_STRIDE = 16
SLC_LEN = 64
SLC_TOPK = 8
WIN_LEN = 512

MEM_HEADS = 4
MEM_DIM = 128

N_BRANCH = 3

DIFF_QK_COLS = DIFF_HEADS * 2 * DIFF_QK_DIM
DIFF_V_COLS = DIFF_HEADS * DIFF_V_DIM
NSA_Q_COLS = NSA_HEADS * NSA_DIM
NSA_KV_COLS = 3 * 2 * NSA_GROUPS * NSA_DIM
NSA_GATE_COLS = NSA_HEADS * 3
MEM_Q_COLS = MEM_HEADS * MEM_DIM
MERGE_GATE_COLS = N_BRANCH * D_MODEL
IN_SPLITS = (DIFF_QK_COLS, DIFF_QK_COLS, DIFF_V_COLS, NSA_Q_COLS, NSA_KV_COLS, NSA_GATE_COLS, MEM_Q_COLS, MERGE_GATE_COLS)
IN_COLS = sum(IN_SPLITS)

kernel_name = "hybrid_diff_nsa_macaron_trunk"


def rmsnorm(x, g):
    xf = x.astype(jnp.float32)
    y = xf * lax.rsqrt(jnp.mean(xf * xf, axis=-1, keepdims=True) + NORM_EPS)
    return (y * g.astype(jnp.float32)).astype(x.dtype)


def swiglu(x, w_gate, w_up, w_down):
    return (jax.nn.silu(x @ w_gate) * (x @ w_up)) @ w_down


def alibi_slopes(n_heads):
    return jnp.asarray(np.array([2.0 ** (-8.0 * (h + 1) / n_heads) for h in range(n_heads)], dtype=np.float32))


def masked_softmax(s, mask):
    s = jnp.where(mask, s, NEG_INF)
    m = jnp.max(s, axis=-1, keepdims=True)
    e = jnp.where(mask, jnp.exp(s - m), 0.0)
    return e / jnp.maximum(jnp.sum(e, axis=-1, keepdims=True), 1e-30)


def split_columns(proj):
    outs, start = [], 0
    for width in IN_SPLITS:
        outs.append(proj[..., start:start + width])
        start += width
    return outs


def diff_attention(q, k, v, lam, lam_init, subln_g):
    B, S = q.shape[:2]
    nb = S // Q_BLOCK
    scale = DIFF_QK_DIM ** -0.5
    slopes = alibi_slopes(DIFF_HEADS)[None, :, None, None, None]
    kpos = jnp.arange(S)
    qb = q.reshape(B, nb, Q_BLOCK, DIFF_HEADS, 2, DIFF_QK_DIM).transpose(1, 0, 2, 3, 4, 5)

    def block(args):
        q_blk, i = args
        qpos = i * Q_BLOCK + jnp.arange(Q_BLOCK)
        dist = (qpos[:, None] - kpos[None, :]).astype(jnp.float32)
        s = jnp.einsum('bqhmd,bkhmd->bhmqk', q_blk, k).astype(jnp.float32) * scale - slopes * dist
        p = masked_softmax(s, dist >= 0)
        w = p[:, :, 0] - lam * p[:, :, 1]
        return jnp.einsum('bhqk,bkhd->bqhd', w.astype(v.dtype), v)

    o = lax.map(block, (qb, jnp.arange(nb)))
    o = o.transpose(1, 0, 2, 3, 4).reshape(B, S, DIFF_HEADS, DIFF_V_DIM)
    o = rmsnorm(o, subln_g) * (1.0 - lam_init)
    return o.reshape(B, S, DIFF_HEADS * DIFF_V_DIM)


def compress(kv, pos, w1, w2):
    B, S, G, d = kv.shape
    nc = (S - CMP_LEN) // CMP_STRIDE + 1
    idx = jnp.arange(nc)[:, None] * CMP_STRIDE + jnp.arange(CMP_LEN)[None, :]
    blocks = kv[:, idx] + pos[None, None, :, None, :]
    blocks = blocks.transpose(0, 1, 3, 2, 4).reshape(B, nc, G, CMP_LEN * d)
    return jax.nn.gelu(blocks @ w1) @ w2


def nsa_attention(q, k_cmp, v_cmp, k_slc, v_slc, k_win, v_win, gates):
    B, S = q.shape[:2]
    nb = S // Q_BLOCK
    nc = k_cmp.shape[1]
    ns = S // SLC_LEN
    n_sel = min(SLC_TOPK, ns)
    scale = NSA_DIM ** -0.5
    slopes = alibi_slopes(NSA_HEADS).reshape(NSA_GROUPS, NSA_HPG)[None, :, :, None, None]
    cmp_start = jnp.arange(nc) * CMP_STRIDE
    cmp_end = cmp_start + CMP_LEN - 1
    slc_start = jnp.arange(ns) * SLC_LEN
    overlap = ((cmp_start[:, None] < slc_start[None, :] + SLC_LEN) & (cmp_start[:, None] + CMP_LEN > slc_start[None, :])).astype(jnp.float32)
    k_slc_t = k_slc.transpose(0, 2, 1, 3)
    v_slc_t = v_slc.transpose(0, 2, 1, 3)
    kw_pad = jnp.pad(k_win, ((0, 0), (WIN_LEN, 0), (0, 0), (0, 0)))
    vw_pad = jnp.pad(v_win, ((0, 0), (WIN_LEN, 0), (0, 0), (0, 0)))
    bi = jnp.arange(B)[:, None, None, None]
    gi = jnp.arange(NSA_GROUPS)[None, :, None, None]
    jblk = jnp.arange(ns)
    qb = q.reshape(B, nb, Q_BLOCK, NSA_GROUPS, NSA_HPG, NSA_DIM).transpose(1, 0, 2, 3, 4, 5)
    gb = gates.reshape(B, nb, Q_BLOCK, NSA_GROUPS, NSA_HPG, 3).transpose(1, 0, 2, 3, 4, 5)

    def block(args):
        q_blk, g_blk, i = args
        q0 = i * Q_BLOCK
        qpos = q0 + jnp.arange(Q_BLOCK)
        s_c = jnp.einsum('bqghd,bcgd->bghqc', q_blk, k_cmp).astype(jnp.float32) * scale
        p_c = masked_softmax(s_c, cmp_end[None, :] <= qpos[:, None])
        o_c = jnp.einsum('bghqc,bcgd->bqghd', p_c.astype(v_cmp.dtype), v_cmp)
        imp = jnp.einsum('bghqc,cj->bgqj', p_c, overlap)
        cur = qpos // SLC_LEN
        forced = (jblk[None, :] == 0) | (jblk[None, :] == cur[:, None]) | (jblk[None, :] == cur[:, None] - 1)
        future = jblk[None, :] > cur[:, None]
        imp = jnp.where(forced, SEL_BIG, jnp.where(future, -SEL_BIG, imp))
        _, sel = lax.top_k(imp, n_sel)
        tok = (sel[..., None] * SLC_LEN + jnp.arange(SLC_LEN)).reshape(B, NSA_GROUPS, Q_BLOCK, n_sel * SLC_LEN)
        kg = k_slc_t[bi, gi, tok]
        vg = v_slc_t[bi, gi, tok]
        dist_s = (qpos[None, None, :, None] - tok).astype(jnp.float32)[:, :, None]
        s_s = jnp.einsum('bqghd,bgqtd->bghqt', q_blk, kg).astype(jnp.float32) * scale - slopes * dist_s
        p_s = masked_softmax(s_s, dist_s >= 0)
        o_s = jnp.einsum('bghqt,bgqtd->bqghd', p_s.astype(vg.dtype), vg)
        kw = lax.dynamic_slice_in_dim(kw_pad, q0, Q_BLOCK + WIN_LEN, axis=1)
        vw = lax.dynamic_slice_in_dim(vw_pad, q0, Q_BLOCK + WIN_LEN, axis=1)
        kpos = q0 - WIN_LEN + jnp.arange(Q_BLOCK + WIN_LEN)
        dist_w = (qpos[:, None] - kpos[None, :]).astype(jnp.float32)
        mask_w = (dist_w >= 0) & (dist_w < WIN_LEN) & (kpos[None, :] >= 0)
        s_w = jnp.einsum('bqghd,bkgd->bghqk', q_blk, kw).astype(jnp.float32) * scale - slopes * dist_w
        p_w = masked_softmax(s_w, mask_w)
        o_w = jnp.einsum('bghqk,bkgd->bqghd', p_w.astype(vw.dtype), vw)
        return g_blk[..., 0:1] * o_c + g_blk[..., 1:2] * o_s + g_blk[..., 2:3] * o_w

    o = lax.map(block, (qb, gb, jnp.arange(nb)))
    return o.transpose(1, 0, 2, 3, 4, 5).reshape(B, S, NSA_HEADS * NSA_DIM)


def memory_attention(q, k, v):
    B, S = q.shape[:2]
    s = jnp.einsum('bshd,bmhd->bhsm', q, k).astype(jnp.float32) * (MEM_DIM ** -0.5)
    p = jax.nn.softmax(s, axis=-1)
    return jnp.einsum('bhsm,bmhd->bshd', p.astype(v.dtype), v).reshape(B, S, MEM_HEADS * MEM_DIM)


def setup_inputs(seed: int = 0) -> dict:
    key = jax.random.key(seed)
    keys = iter(jax.random.split(key, 32))
    L, D, F = DEPTH, D_MODEL, D_FF

    def dense(shape, fan_in):
        return jax.random.normal(next(keys), shape, jnp.float32) * (fan_in ** -0.5)

    def gain(shape):
        return 1.0 + 0.02 * jax.random.normal(next(keys), shape, jnp.float32)

    return {
        "x": jax.random.normal(next(keys), (BATCH, SEQ, D), jnp.float32),
        "mem": jax.random.normal(next(keys), (BATCH, MEM_LEN, D), jnp.float32),
        "ffn1_norm": gain((L, D)),
        "ffn1_w_gate": dense((L, D, F), D),
        "ffn1_w_up": dense((L, D, F), D),
        "ffn1_w_down": dense((L, F, D), F),
        "mix_norm": gain((L, D)),
        "w_in": dense((L, D, IN_COLS), D),
        "diff_lambda": 0.1 * jax.random.normal(next(keys), (L, 4, DIFF_QK_DIM), jnp.float32),
        "diff_subln": gain((L, DIFF_V_DIM)),
        "nsa_cmp_pos": 0.1 * jax.random.normal(next(keys), (L, 2, CMP_LEN, NSA_DIM), jnp.float32),
        "nsa_cmp_w1": dense((L, 2, CMP_LEN * NSA_DIM, NSA_DIM), CMP_LEN * NSA_DIM),
        "nsa_cmp_w2": dense((L, 2, NSA_DIM, NSA_DIM), NSA_DIM),
        "mem_norm": gain((L, D)),
        "w_mem_kv": dense((L, D, 2 * MEM_HEADS * MEM_DIM), D),
        "w_up_diff": dense((L, DIFF_HEADS * DIFF_V_DIM, D), DIFF_HEADS * DIFF_V_DIM),
        "w_up_nsa": dense((L, NSA_HEADS * NSA_DIM, D), NSA_HEADS * NSA_DIM),
        "w_up_mem": dense((L, MEM_HEADS * MEM_DIM, D), MEM_HEADS * MEM_DIM),
        "w_out": dense((L, D, D), D),
        "ffn2_norm": gain((L, D)),
        "ffn2_w_gate": dense((L, D, F), D),
        "ffn2_w_up": dense((L, D, F), D),
        "ffn2_w_down": dense((L, F, D), F),
        "final_norm": gain((D,)),
    }


def reference(x, mem, ffn1_norm, ffn1_w_gate, ffn1_w_up, ffn1_w_down, mix_norm, w_in,
              diff_lambda, diff_subln, nsa_cmp_pos, nsa_cmp_w1, nsa_cmp_w2, mem_norm, w_mem_kv,
              w_up_diff, w_up_nsa, w_up_mem, w_out, ffn2_norm, ffn2_w_gate, ffn2_w_up,
              ffn2_w_down, final_norm):
    B, S, _ = x.shape
    M = mem.shape[1]
    for l in range(DEPTH):
        x = x + 0.5 * swiglu(rmsnorm(x, ffn1_norm[l]), ffn1_w_gate[l], ffn1_w_up[l], ffn1_w_down[l])
        h = rmsnorm(x, mix_norm[l])
        dq, dk, dv, nq, nkv, ng, mq, mg = split_columns(h @ w_in[l])
        lam_init = 0.8 - 0.6 * math.exp(-0.3 * l)
        lp = diff_lambda[l].astype(jnp.float32)
        lam = jnp.exp(jnp.sum(lp[0] * lp[1])) - jnp.exp(jnp.sum(lp[2] * lp[3])) + lam_init
        o_diff = diff_attention(dq.reshape(B, S, DIFF_HEADS, 2, DIFF_QK_DIM),
                                dk.reshape(B, S, DIFF_HEADS, 2, DIFF_QK_DIM),
                                dv.reshape(B, S, DIFF_HEADS, DIFF_V_DIM),
                                lam, lam_init, diff_subln[l])
        nkv = nkv.reshape(B, S, 3, 2, NSA_GROUPS, NSA_DIM)
        k_cmp = compress(nkv[:, :, 0, 0], nsa_cmp_pos[l, 0], nsa_cmp_w1[l, 0], nsa_cmp_w2[l, 0])
        v_cmp = compress(nkv[:, :, 0, 1], nsa_cmp_pos[l, 1], nsa_cmp_w1[l, 1], nsa_cmp_w2[l, 1])
        o_nsa = nsa_attention(nq.reshape(B, S, NSA_GROUPS, NSA_HPG, NSA_DIM), k_cmp, v_cmp,
                              nkv[:, :, 1, 0], nkv[:, :, 1, 1], nkv[:, :, 2, 0], nkv[:, :, 2, 1],
                              jax.nn.sigmoid(ng).reshape(B, S, NSA_GROUPS, NSA_HPG, 3))
        mem_kv = (rmsnorm(mem, mem_norm[l]) @ w_mem_kv[l]).reshape(B, M, 2, MEM_HEADS, MEM_DIM)
        o_mem = memory_attention(mq.reshape(B, S, MEM_HEADS, MEM_DIM), mem_kv[:, :, 0], mem_kv[:, :, 1])
        g = jax.nn.sigmoid(mg).reshape(B, S, N_BRANCH, D_MODEL)
        merged = (g[:, :, 0] * (o_diff @ w_up_diff[l]) + g[:, :, 1] * (o_nsa @ w_up_nsa[l])
                  + g[:, :, 2] * (o_mem @ w_up_mem[l]))
        x = x + merged @ w_out[l]
        x = x + 0.5 * swiglu(rmsnorm(x, ffn2_norm[l]), ffn2_w_gate[l], ffn2_w_up[l], ffn2_w_down[l])
    return rmsnorm(x, final_norm)
```

```python
import functools
import math

import jax
import jax.numpy as jnp
import numpy as np
from jax import lax
from jax.experimental import pallas as pl
from jax.experimental.pallas import tpu as pltpu

F32 = jnp.float32
BF16 = jnp.bfloat16

NORM_EPS = 1e-6
NEG_INF = -1e30
SEL_BIG = 1e9

DIFF_HEADS = 8
DIFF_QK_DIM = 64
DIFF_V_DIM = 2 * DIFF_QK_DIM

NSA_HEADS = 8
NSA_GROUPS = 2
NSA_HPG = NSA_HEADS // NSA_GROUPS
NSA_DIM = 128
CMP_LEN = 32
CMP_STRIDE = 16
SLC_LEN = 64
SLC_TOPK = 8
WIN_LEN = 512

MEM_HEADS = 4
MEM_DIM = 128
N_BRANCH = 3

LANES = 128
VMEM_LIMIT = 56 * 1024 * 1024

DIFF_QK_COLS = DIFF_HEADS * 2 * DIFF_QK_DIM
DIFF_V_COLS = DIFF_HEADS * DIFF_V_DIM
NSA_Q_COLS = NSA_HEADS * NSA_DIM
NSA_KV_COLS = 3 * 2 * NSA_GROUPS * NSA_DIM
NSA_GATE_COLS = NSA_HEADS * 3
MEM_Q_COLS = MEM_HEADS * MEM_DIM
C_DQ = 0
C_DK = C_DQ + DIFF_QK_COLS
C_DV = C_DK + DIFF_QK_COLS
C_NQ = C_DV + DIFF_V_COLS
C_NKV = C_NQ + NSA_Q_COLS
C_NG = C_NKV + NSA_KV_COLS
C_MQ = C_NG + NSA_GATE_COLS
C_MG = C_MQ + MEM_Q_COLS
P_MQ = C_NG
P_COLS = P_MQ + MEM_Q_COLS


def _alibi_slopes(n_heads):
    return np.array([2.0 ** (-8.0 * (h + 1) / n_heads) for h in range(n_heads)], dtype=np.float32)


def _rmsnorm(x, g):
    y = x * lax.rsqrt(jnp.mean(x * x, axis=-1, keepdims=True) + NORM_EPS)
    return y * g


def _dot(a, b):
    return jnp.dot(a, b, preferred_element_type=F32)


def _dot_nt(a, b):
    return lax.dot_general(a, b, (((1,), (1,)), ((), ())), preferred_element_type=F32)


def _params(*sem):
    return pltpu.CompilerParams(dimension_semantics=sem, vmem_limit_bytes=VMEM_LIMIT)


def _pick(n, pref):
    t = min(n, pref)
    assert n % t == 0, (n, t)
    return t


def _ffn_kernel(x_ref, g_ref, wg_ref, wu_ref, wd_ref, fg_ref, o_ref, xn_ref, acc_ref, *, final_norm):
    j = pl.program_id(1)

    @pl.when(j == 0)
    def _():
        xn_ref[...] = _rmsnorm(x_ref[...], g_ref[...]).astype(BF16)
        acc_ref[...] = jnp.zeros_like(acc_ref)

    xn = xn_ref[...]
    a = _dot(xn, wg_ref[...])
    u = _dot(xn, wu_ref[...])
    h = (a * jax.nn.sigmoid(a)) * u
    acc_ref[...] += _dot(h.astype(BF16), wd_ref[...])

    @pl.when(j == pl.num_programs(1) - 1)
    def _():
        y = x_ref[...] + 0.5 * acc_ref[...]
        if final_norm:
            y = _rmsnorm(y, fg_ref[...])
        o_ref[...] = y


def _ffn(x, g, wg, wu, wd, fg, final_norm):
    t, d = x.shape
    f = wg.shape[1]
    tm = _pick(t, 512)
    tf = _pick(f, 512)
    return pl.pallas_call(
        functools.partial(_ffn_kernel, final_norm=final_norm),
        grid=(t // tm, f // tf),
        in_specs=[
            pl.BlockSpec((tm, d), lambda i, j: (i, 0)),
            pl.BlockSpec((1, d), lambda i, j: (0, 0)),
            pl.BlockSpec((d, tf), lambda i, j: (0, j)),
            pl.BlockSpec((d, tf), lambda i, j: (0, j)),
            pl.BlockSpec((tf, d), lambda i, j: (j, 0)),
            pl.BlockSpec((1, d), lambda i, j: (0, 0)),
        ],
        out_specs=pl.BlockSpec((tm, d), lambda i, j: (i, 0)),
        out_shape=jax.ShapeDtypeStruct((t, d), F32),
        scratch_shapes=[pltpu.VMEM((tm, d), BF16), pltpu.VMEM((tm, d), F32)],
        compiler_params=_params("parallel", "arbitrary"),
        name="ffn",
    )(x, g, wg, wu, wd, fg)


def _inproj_kernel(x_ref, g_ref, w_ref, wng_ref, proj_ref, hn_ref, ng_ref, xn_ref):
    j = pl.program_id(1)

    @pl.when(j == 0)
    def _():
        xn = _rmsnorm(x_ref[...], g_ref[...]).astype(BF16)
        xn_ref[...] = xn
        hn_ref[...] = xn
        ng_ref[...] = jax.nn.sigmoid(_dot(xn, wng_ref[...]))

    proj_ref[...] = _dot(xn_ref[...], w_ref[...]).astype(BF16)


def _inproj(x, g, w, wng):
    t, d = x.shape
    n = w.shape[1]
    tm = _pick(t, 512)
    tn = _pick(n, 512)
    return pl.pallas_call(
        _inproj_kernel,
        grid=(t // tm, n // tn),
        in_specs=[
            pl.BlockSpec((tm, d), lambda i, j: (i, 0)),
            pl.BlockSpec((1, d), lambda i, j: (0, 0)),
            pl.BlockSpec((d, tn), lambda i, j: (0, j)),
            pl.BlockSpec((d, LANES), lambda i, j: (0, 0)),
        ],
        out_specs=[
            pl.BlockSpec((tm, tn), lambda i, j: (i, j)),
            pl.BlockSpec((tm, d), lambda i, j: (i, 0)),
            pl.BlockSpec((tm, LANES), lambda i, j: (i, 0)),
        ],
        out_shape=[
            jax.ShapeDtypeStruct((t, n), BF16),
            jax.ShapeDtypeStruct((t, d), BF16),
            jax.ShapeDtypeStruct((t, LANES), F32),
        ],
        scratch_shapes=[pltpu.VMEM((tm, d), BF16)],
        compiler_params=_params("parallel", "arbitrary"),
        name="inproj",
    )(x, g, w, wng)


def _softmax_update(s, mask, v, m_ref, l_ref, acc_ref):
    if mask is not None:
        s = jnp.where(mask, s, NEG_INF)
    m_old = m_ref[...]
    m_new = jnp.maximum(m_old, jnp.max(s, axis=-1, keepdims=True))
    e = jnp.exp(s - m_new)
    if mask is not None:
        e = jnp.where(mask, e, 0.0)
    alpha = jnp.exp(m_old - m_new)
    l_ref[...] = alpha * l_ref[...] + jnp.sum(e, axis=-1, keepdims=True)
    acc_ref[...] = alpha * acc_ref[...] + _dot(e.astype(BF16), v)
    m_ref[...] = m_new


def _softmax_finish(l_ref, acc_ref):
    return acc_ref[...] / jnp.maximum(l_ref[...], 1e-30)


def _masked_softmax(s, mask):
    s = jnp.where(mask, s, NEG_INF)
    m = jnp.max(s, axis=-1, keepdims=True)
    e = jnp.where(mask, jnp.exp(s - m), 0.0)
    return e / jnp.maximum(jnp.sum(e, axis=-1, keepdims=True), 1e-30)


def _diff_kernel(slopes_ref, q_ref, k_ref, v_ref, lp_ref, sg_ref, o_ref,
                 m_ref, l_ref, acc_ref, *, tq, tk, lam_init):
    h = pl.program_id(1)
    qi = pl.program_id(2)
    q0 = qi * tq
    slope = slopes_ref[h]
    scale = DIFF_QK_DIM ** -0.5
    q = q_ref[...]
    qs = [q[:, m * DIFF_QK_DIM:(m + 1) * DIFF_QK_DIM] for m in range(2)]

    m_ref[...] = jnp.full_like(m_ref, NEG_INF)
    l_ref[...] = jnp.zeros_like(l_ref)
    acc_ref[...] = jnp.zeros_like(acc_ref)

    row = lax.broadcasted_iota(jnp.int32, (tq, tk), 0)
    col = lax.broadcasted_iota(jnp.int32, (tq, tk), 1)
    col_row = lax.broadcasted_iota(jnp.int32, (1, tk), 1)

    def step(t, carry):
        ks = pl.multiple_of(t * tk, tk)
        k = k_ref[pl.ds(ks, tk), :]
        v = v_ref[pl.ds(ks, tk), :]
        bias = slope * (col_row + (ks - q0)).astype(F32)
        mask = (col + ks) <= (row + q0)
        for m in range(2):
            km = k[:, m * DIFF_QK_DIM:(m + 1) * DIFF_QK_DIM]
            s = _dot_nt(qs[m], km) * scale + bias
            _softmax_update(s, mask, v, m_ref.at[m], l_ref.at[m], acc_ref.at[m])
        return carry

    n_tiles = (q0 + tq + tk - 1) // tk
    lax.fori_loop(0, n_tiles, step, 0)

    o1 = _softmax_finish(l_ref.at[0], acc_ref.at[0])
    o2 = _softmax_finish(l_ref.at[1], acc_ref.at[1])
    lp = lp_ref[...]
    lam = (jnp.exp(jnp.sum(lp[0:1] * lp[1:2], axis=-1, keepdims=True))
           - jnp.exp(jnp.sum(lp[2:3] * lp[3:4], axis=-1, keepdims=True)) + lam_init)
    o = o1 - lam * o2
    o = _rmsnorm(o, sg_ref[...]) * (1.0 - lam_init)
    o_ref[...] = o.astype(BF16)


def _diff_attention(proj, lp, sg, batch, seq, lam_init):
    t = proj.shape[0]
    tq = _pick(seq, 512)
    tk = _pick(seq, 512)
    nq = seq // tq
    kb, vb = C_DK // LANES, C_DV // LANES
    slopes = jnp.asarray(_alibi_slopes(DIFF_HEADS))
    return pl.pallas_call(
        functools.partial(_diff_kernel, tq=tq, tk=tk, lam_init=lam_init),
        grid=(batch, DIFF_HEADS, nq),
        in_specs=[
            pl.BlockSpec(memory_space=pltpu.SMEM),
            pl.BlockSpec((tq, LANES), lambda b, h, i: (b * nq + i, h)),
            pl.BlockSpec((seq, LANES), lambda b, h, i: (b, kb + h)),
            pl.BlockSpec((seq, LANES), lambda b, h, i: (b, vb + h)),
            pl.BlockSpec((4, DIFF_QK_DIM), lambda b, h, i: (0, 0)),
            pl.BlockSpec((1, DIFF_V_DIM), lambda b, h, i: (0, 0)),
        ],
        out_specs=pl.BlockSpec((tq, LANES), lambda b, h, i: (b * nq + i, h)),
        out_shape=jax.ShapeDtypeStruct((t, DIFF_HEADS * DIFF_V_DIM), BF16),
        scratch_shapes=[
            pltpu.VMEM((2, tq, 1), F32),
            pltpu.VMEM((2, tq, 1), F32),
            pltpu.VMEM((2, tq, DIFF_V_DIM), F32),
        ],
        compiler_params=_params("parallel", "parallel", "arbitrary"),
        name="diff_attn",
    )(slopes, proj, proj, proj, lp, sg)


def _compress_kernel(kv_ref, pos_ref, w1_ref, w2_ref, o_ref):
    half = w1_ref.shape[0] // 2
    kv = kv_ref[...]
    top = _dot(kv, w1_ref[:half, :])
    bot = _dot(kv, w1_ref[half:, :])
    rows = kv.shape[0]
    bot_next = pltpu.roll(bot, rows - 1, 0)
    pos_term = _dot(pos_ref[...], w1_ref[...])[0:1]
    hid = jax.nn.gelu(top + bot_next + pos_term)
    o_ref[...] = _dot(hid.astype(BF16), w2_ref[...]).astype(BF16)


def _compress(kvr, pos, w1, w2):
    _, groups, batch, rows, width = kvr.shape
    d = w2.shape[-1]
    return pl.pallas_call(
        _compress_kernel,
        grid=(2, groups, batch),
        in_specs=[
            pl.BlockSpec((None, None, None, rows, width), lambda a, g, b: (a, g, b, 0, 0)),
            pl.BlockSpec((None, 8, 2 * width), lambda a, g, b: (a, 0, 0)),
            pl.BlockSpec((None, 2 * width, d), lambda a, g, b: (a, 0, 0)),
            pl.BlockSpec((None, d, d), lambda a, g, b: (a, 0, 0)),
        ],
        out_specs=pl.BlockSpec((None, None, None, rows, d), lambda a, g, b: (a, b, g, 0, 0)),
        out_shape=jax.ShapeDtypeStruct((2, batch, groups, rows, d), BF16),
        compiler_params=_params("parallel", "parallel", "parallel"),
        name="compress",
    )(kvr, pos, w1, w2)


def _nsa_kernel(slopes_ref, q_ref, gate_ref, kc_ref, vc_ref, ks_ref, vs_ref, kw_ref, vw_ref,
                ov_ref, o_ref, m_ref, l_ref, acc_ref, *, tq, tk, seq):
    g = pl.program_id(1)
    qi = pl.program_id(2)
    q0 = qi * tq
    scale = NSA_DIM ** -0.5
    hp = NSA_HPG
    rows = hp * tq
    n_slc = seq // SLC_LEN

    q = q_ref[...]
    q4 = jnp.concatenate([q[:, h * NSA_DIM:(h + 1) * NSA_DIM] for h in range(hp)], axis=0)
    slope_col = jnp.concatenate(
        [jnp.full((tq, 1), slopes_ref[g * hp + h], F32) for h in range(hp)], axis=0)
    qpos_col = q0 + (lax.broadcasted_iota(jnp.int32, (rows, 1), 0) & (tq - 1))

    n_cmp = kc_ref.shape[0]
    cmp_end = lax.broadcasted_iota(jnp.int32, (1, n_cmp), 1) * CMP_STRIDE + (CMP_LEN - 1)
    s_c = _dot_nt(q4, kc_ref[...]) * scale
    p_c = _masked_softmax(s_c, cmp_end <= qpos_col)
    o_c = _dot(p_c.astype(BF16), vc_ref[...])

    p_sum = p_c[0:tq]
    for h in range(1, hp):
        p_sum = p_sum + p_c[h * tq:(h + 1) * tq]
    ov = ov_ref[...]
    p_hi = p_sum.astype(BF16)
    r1 = p_sum - p_hi.astype(F32)
    p_mid = r1.astype(BF16)
    p_lo = (r1 - p_mid.astype(F32)).astype(BF16)
    imp = _dot(p_hi, ov) + _dot(p_mid, ov) + _dot(p_lo, ov)
    jblk = lax.broadcasted_iota(jnp.int32, (tq, n_slc), 1)
    cur = (q0 + lax.broadcasted_iota(jnp.int32, (tq, 1), 0)) // SLC_LEN
    forced = (jblk == 0) | (jblk == cur) | (jblk == cur - 1)
    imp = jnp.where(forced, SEL_BIG, jnp.where(jblk > cur, -SEL_BIG, imp))
    jblk_f = jblk.astype(F32)
    sel = jnp.zeros((tq, n_slc), F32)
    for _ in range(min(SLC_TOPK, n_slc)):
        mx = jnp.max(imp, axis=-1, keepdims=True)
        first = jnp.min(jnp.where(imp == mx, jblk_f, float(n_slc)), axis=-1, keepdims=True)
        hit = jblk_f == first
        sel = jnp.where(hit, 1.0, sel)
        imp = jnp.where(hit, -jnp.inf, imp)
    sel_bf = sel.astype(BF16)

    m_ref[...] = jnp.full_like(m_ref, NEG_INF)
    l_ref[...] = jnp.zeros_like(l_ref)
    acc_ref[...] = jnp.zeros_like(acc_ref)
    blk_row = lax.broadcasted_iota(jnp.int32, (n_slc, tk), 0)
    blk_col = lax.broadcasted_iota(jnp.int32, (n_slc, tk), 1) // SLC_LEN
    kcol = lax.broadcasted_iota(jnp.int32, (1, tk), 1)
    qpos_tile = q0 + lax.broadcasted_iota(jnp.int32, (tq, 1), 0)

    def step(t, carry):
        ks = pl.multiple_of(t * tk, tk)
        k = ks_ref[pl.ds(ks, tk), :]
        v = vs_ref[pl.ds(ks, tk), :]
        expand = jnp.where(blk_row == blk_col + t * (tk // SLC_LEN), 1.0, 0.0).astype(BF16)
        chosen = _dot(sel_bf, expand) > 0.5
        mask = chosen & ((kcol + ks) <= qpos_tile)
        mask = jnp.concatenate([mask] * hp, axis=0)
        bias = slope_col * (kcol + (ks - q0)).astype(F32)
        s = _dot_nt(q4, k) * scale + bias
        _softmax_update(s, mask, v, m_ref, l_ref, acc_ref)
        return carry

    n_tiles = (q0 + tq + tk - 1) // tk
    lax.fori_loop(0, n_tiles, step, 0)
    o_s = _softmax_finish(l_ref, acc_ref)

    span = WIN_LEN + tq
    w0 = pl.multiple_of(jnp.maximum(q0 - WIN_LEN, 0), tq)
    kw = kw_ref[pl.ds(w0, span), :]
    vw = vw_ref[pl.ds(w0, span), :]
    wcol = lax.broadcasted_iota(jnp.int32, (1, span), 1)
    dist = qpos_col - (wcol + w0)
    mask_w = (dist >= 0) & (dist < WIN_LEN)
    s_w = _dot_nt(q4, kw) * scale - slope_col * dist.astype(F32)
    p_w = _masked_softmax(s_w, mask_w)
    o_w = _dot(p_w.astype(BF16), vw)

    gates = gate_ref[...]
    outs = []
    for h in range(hp):
        c = (g * hp + h) * 3
        sl = slice(h * tq, (h + 1) * tq)
        gsel = [jnp.sum(jnp.where(lax.broadcasted_iota(jnp.int32, gates.shape, 1) == c + b, gates, 0.0),
                        axis=-1, keepdims=True) for b in range(3)]
        outs.append(gsel[0] * o_c[sl] + gsel[1] * o_s[sl] + gsel[2] * o_w[sl])
    o_ref[...] = jnp.concatenate(outs, axis=-1).astype(BF16)


def _nsa_attention(proj, gates, kv_cmp, overlap, batch, seq):
    t = proj.shape[0]
    tq = _pick(seq, 128)
    tk = _pick(seq, 512)
    assert seq >= WIN_LEN + tq and WIN_LEN % tq == 0
    nq = seq // tq
    hp = NSA_HPG
    qb = C_NQ // (hp * NSA_DIM)
    kvb = C_NKV // LANES
    n_cmp = kv_cmp.shape[3]
    slopes = jnp.asarray(_alibi_slopes(NSA_HEADS))

    def kv_spec(branch, which):
        base = kvb + (branch * 2 + which) * NSA_GROUPS
        return pl.BlockSpec((seq, LANES), lambda b, g, i: (b, base + g))

    def cmp_spec(which):
        return pl.BlockSpec((None, None, None, n_cmp, NSA_DIM), lambda b, g, i: (which, b, g, 0, 0))

    return pl.pallas_call(
        functools.partial(_nsa_kernel, tq=tq, tk=tk, seq=seq),
        grid=(batch, NSA_GROUPS, nq),
        in_specs=[
            pl.BlockSpec(memory_space=pltpu.SMEM),
            pl.BlockSpec((tq, hp * NSA_DIM), lambda b, g, i: (b * nq + i, qb + g)),
            pl.BlockSpec((tq, LANES), lambda b, g, i: (b * nq + i, 0)),
            cmp_spec(0), cmp_spec(1),
            kv_spec(1, 0), kv_spec(1, 1),
            kv_spec(2, 0), kv_spec(2, 1),
            pl.BlockSpec(overlap.shape, lambda b, g, i: (0, 0)),
        ],
        out_specs=pl.BlockSpec((tq, hp * NSA_DIM), lambda b, g, i: (b * nq + i, g)),
        out_shape=jax.ShapeDtypeStruct((t, NSA_HEADS * NSA_DIM), BF16),
        scratch_shapes=[
            pltpu.VMEM((hp * tq, 1), F32),
            pltpu.VMEM((hp * tq, 1), F32),
            pltpu.VMEM((hp * tq, NSA_DIM), F32),
        ],
        compiler_params=_params("parallel", "parallel", "arbitrary"),
        name="nsa_attn",
    )(slopes, proj, gates, kv_cmp, kv_cmp, proj, proj, proj, proj, overlap)


def _memkv_kernel(mem_ref, g_ref, w_ref, o_ref):
    mn = _rmsnorm(mem_ref[...], g_ref[...]).astype(BF16)
    o_ref[...] = _dot(mn, w_ref[...]).astype(BF16)


def _memkv(mem, g, w):
    r, d = mem.shape
    n = w.shape[1]
    tm = _pick(r, 256)
    return pl.pallas_call(
        _memkv_kernel,
        grid=(r // tm,),
        in_specs=[
            pl.BlockSpec((tm, d), lambda i: (i, 0)),
            pl.BlockSpec((1, d), lambda i: (0, 0)),
            pl.BlockSpec((d, n), lambda i: (0, 0)),
        ],
        out_specs=pl.BlockSpec((tm, n), lambda i: (i, 0)),
        out_shape=jax.ShapeDtypeStruct((r, n), BF16),
        compiler_params=_params("parallel"),
        name="mem_kv",
    )(mem, g, w)


def _memattn_kernel(q_ref, k_ref, v_ref, o_ref):
    scale = MEM_DIM ** -0.5
    q = q_ref[...]
    k = k_ref[...]
    v = v_ref[...]
    outs = []
    for h in range(MEM_HEADS):
        sl = slice(h * MEM_DIM, (h + 1) * MEM_DIM)
        s = _dot_nt(q[:, sl], k[:, sl]) * scale
        m = jnp.max(s, axis=-1, keepdims=True)
        e = jnp.exp(s - m)
        p = e / jnp.sum(e, axis=-1, keepdims=True)
        outs.append(_dot(p.astype(BF16), v[:, sl]))
    o_ref[...] = jnp.concatenate(outs, axis=-1).astype(BF16)


def _mem_attention(proj, memkv, batch, seq, mem_len):
    t = proj.shape[0]
    tq = _pick(seq, 512)
    nq = seq // tq
    width = MEM_HEADS * MEM_DIM
    qb = P_MQ // width
    return pl.pallas_call(
        _memattn_kernel,
        grid=(batch, nq),
        in_specs=[
            pl.BlockSpec((tq, width), lambda b, i: (b * nq + i, qb)),
            pl.BlockSpec((mem_len, width), lambda b, i: (b, 0)),
            pl.BlockSpec((mem_len, width), lambda b, i: (b, 1)),
        ],
        out_specs=pl.BlockSpec((tq, width), lambda b, i: (b * nq + i, 0)),
        out_shape=jax.ShapeDtypeStruct((t, width), BF16),
        compiler_params=_params("parallel", "parallel"),
        name="mem_attn",
    )(proj, memkv, memkv)


def _merge_kernel(x_ref, hn_ref, od_ref, on_ref, om_ref, wgd_ref, wgn_ref, wgm_ref,
                  wud_ref, wun_ref, wum_ref, wo_ref, o_ref, acc_ref):
    j = pl.program_id(1)

    @pl.when(j == 0)
    def _():
        acc_ref[...] = jnp.zeros_like(acc_ref)

    hn = hn_ref[...]
    merged = jax.nn.sigmoid(_dot(hn, wgd_ref[...])) * _dot(od_ref[...], wud_ref[...])
    merged += jax.nn.sigmoid(_dot(hn, wgn_ref[...])) * _dot(on_ref[...], wun_ref[...])
    merged += jax.nn.sigmoid(_dot(hn, wgm_ref[...])) * _dot(om_ref[...], wum_ref[...])
    acc_ref[...] += _dot(merged.astype(BF16), wo_ref[...])

    @pl.when(j == pl.num_programs(1) - 1)
    def _():
        o_ref[...] = x_ref[...] + acc_ref[...]


def _merge(x, hn, od, on, om, wg, wud, wun, wum, wo):
    t, d = x.shape
    tm = _pick(t, 512)
    tj = _pick(d, 512)
    nj = d // tj

    def gate_spec(branch):
        return pl.BlockSpec((d, tj), lambda i, j: (0, branch * nj + j))

    return pl.pallas_call(
        _merge_kernel,
        grid=(t // tm, nj),
        in_specs=[
            pl.BlockSpec((tm, d), lambda i, j: (i, 0)),
            pl.BlockSpec((tm, d), lambda i, j: (i, 0)),
            pl.BlockSpec((tm, od.shape[1]), lambda i, j: (i, 0)),
            pl.BlockSpec((tm, on.shape[1]), lambda i, j: (i, 0)),
            pl.BlockSpec((tm, om.shape[1]), lambda i, j: (i, 0)),
            gate_spec(0), gate_spec(1), gate_spec(2),
            pl.BlockSpec((wud.shape[0], tj), lambda i, j: (0, j)),
            pl.BlockSpec((wun.shape[0], tj), lambda i, j: (0, j)),
            pl.BlockSpec((wum.shape[0], tj), lambda i, j: (0, j)),
            pl.BlockSpec((tj, d), lambda i, j: (j, 0)),
        ],
        out_specs=pl.BlockSpec((tm, d), lambda i, j: (i, 0)),
        out_shape=jax.ShapeDtypeStruct((t, d), F32),
        scratch_shapes=[pltpu.VMEM((tm, d), F32)],
        compiler_params=_params("parallel", "arbitrary"),
        name="merge",
    )(x, hn, od, on, om, wg, wg, wg, wud, wun, wum, wo)


def _overlap_matrix(n_cmp, n_slc):
    cmp_start = np.arange(n_cmp)[:, None] * CMP_STRIDE
    slc_start = np.arange(n_slc)[None, :] * SLC_LEN
    return ((cmp_start < slc_start + SLC_LEN) & (cmp_start + CMP_LEN > slc_start)).astype(np.float32)


def kernel(x, mem, ffn1_norm, ffn1_w_gate, ffn1_w_up, ffn1_w_down, mix_norm, w_in, diff_lambda, diff_subln, nsa_cmp_pos, nsa_cmp_w1, nsa_cmp_w2, mem_norm, w_mem_kv, w_up_diff, w_up_nsa, w_up_mem, w_out, ffn2_norm, ffn2_w_gate, ffn2_w_up, ffn2_w_down, final_norm):
    batch, seq, d = x.shape
    mem_len = mem.shape[1]
    depth = w_in.shape[0]
    t = batch * seq
    assert seq % CMP_STRIDE == 0 and CMP_LEN == 2 * CMP_STRIDE
    n_rows = seq // CMP_STRIDE
    n_slc = seq // SLC_LEN
    overlap = jnp.asarray(_overlap_matrix(n_rows, n_slc), BF16)

    xt = x.reshape(t, d)
    memt = mem.reshape(batch * mem_len, d)
    fg = final_norm.reshape(1, d)
    bf = lambda a: a.astype(BF16)

    for l in range(depth):
        lam_init = 0.8 - 0.6 * math.exp(-0.3 * l)
        xt = _ffn(xt, ffn1_norm[l].reshape(1, d), bf(ffn1_w_gate[l]), bf(ffn1_w_up[l]),
                  bf(ffn1_w_down[l]), fg, False)

        w = w_in[l]
        w_main = bf(jnp.concatenate([w[:, :C_NG], w[:, C_MQ:C_MG]], axis=1))
        w_ng = bf(jnp.pad(w[:, C_NG:C_MQ], ((0, 0), (0, LANES - NSA_GATE_COLS))))
        proj, hn, ng = _inproj(xt, mix_norm[l].reshape(1, d), w_main, w_ng)

        o_diff = _diff_attention(proj, diff_lambda[l], diff_subln[l].reshape(1, DIFF_V_DIM),
                                 batch, seq, lam_init)

        ckv = proj[:, C_NKV:C_NKV + 2 * NSA_GROUPS * NSA_DIM]
        ckv = ckv.reshape(batch, n_rows, CMP_STRIDE, 2, NSA_GROUPS, NSA_DIM)
        ckv = ckv.transpose(3, 4, 0, 1, 2, 5).reshape(2, NSA_GROUPS, batch, n_rows, CMP_STRIDE * NSA_DIM)
        pos = bf(jnp.broadcast_to(nsa_cmp_pos[l].reshape(2, 1, CMP_LEN * NSA_DIM), (2, 8, CMP_LEN * NSA_DIM)))
        kv_cmp = _compress(ckv, pos, bf(nsa_cmp_w1[l]), bf(nsa_cmp_w2[l]))
        o_nsa = _nsa_attention(proj, ng, kv_cmp, overlap, batch, seq)

        memkv = _memkv(memt, mem_norm[l].reshape(1, d), bf(w_mem_kv[l]))
        o_mem = _mem_attention(proj, memkv, batch, seq, mem_len)

        w_mg = bf(w[:, C_MG:])
        xt = _merge(xt, hn, o_diff, o_nsa, o_mem, w_mg, bf(w_up_diff[l]), bf(w_up_nsa[l]),
                    bf(w_up_mem[l]), bf(w_out[l]))

        xt = _ffn(xt, ffn2_norm[l].reshape(1, d), bf(ffn2_w_gate[l]), bf(ffn2_w_up[l]),
                  bf(ffn2_w_down[l]), fg, l == depth - 1)
    return xt.reshape(batch, seq, d)
```

```python
import functools
import math

import jax
import jax.numpy as jnp
import numpy as np
from jax import lax
from jax.experimental import pallas as pl
from jax.experimental.pallas import tpu as pltpu

F32 = jnp.float32
BF16 = jnp.bfloat16

NORM_EPS = 1e-6
NEG_INF = -1e30
SEL_BIG = 1e9
LOG2E = math.log2(math.e)

DIFF_HEADS = 8
DIFF_QK_DIM = 64
DIFF_V_DIM = 2 * DIFF_QK_DIM

NSA_HEADS = 8
NSA_GROUPS = 2
NSA_HPG = NSA_HEADS // NSA_GROUPS
NSA_DIM = 128
CMP_LEN = 32
CMP_STRIDE = 16
SLC_LEN = 64
SLC_TOPK = 8
WIN_LEN = 512

MEM_HEADS = 4
MEM_DIM = 128
N_BRANCH = 3

LANES = 128
VMEM_LIMIT = 56 * 1024 * 1024

DIFF_QK_COLS = DIFF_HEADS * 2 * DIFF_QK_DIM
DIFF_V_COLS = DIFF_HEADS * DIFF_V_DIM
NSA_Q_COLS = NSA_HEADS * NSA_DIM
NSA_KV_COLS = 3 * 2 * NSA_GROUPS * NSA_DIM
NSA_GATE_COLS = NSA_HEADS * 3
MEM_Q_COLS = MEM_HEADS * MEM_DIM
C_DQ = 0
C_DK = C_DQ + DIFF_QK_COLS
C_DV = C_DK + DIFF_QK_COLS
C_NQ = C_DV + DIFF_V_COLS
C_NKV = C_NQ + NSA_Q_COLS
C_NG = C_NKV + NSA_KV_COLS
C_MQ = C_NG + NSA_GATE_COLS
C_MG = C_MQ + MEM_Q_COLS
P_MQ = C_NG
P_COLS = P_MQ + MEM_Q_COLS


def _alibi_slopes(n_heads):
    return np.array([2.0 ** (-8.0 * (h + 1) / n_heads) for h in range(n_heads)], dtype=np.float32)


def _rmsnorm(x, g):
    y = x * lax.rsqrt(jnp.mean(x * x, axis=-1, keepdims=True) + NORM_EPS)
    return y * g


def _dot(a, b):
    return jnp.dot(a, b, preferred_element_type=F32)


def _dot_nt(a, b):
    return lax.dot_general(a, b, (((1,), (1,)), ((), ())), preferred_element_type=F32)


def _transpose_bf16(x):
    return x.astype(F32).T.astype(BF16)


def _params(*sem):
    return pltpu.CompilerParams(dimension_semantics=sem, vmem_limit_bytes=VMEM_LIMIT)


def _pick(n, pref):
    t = min(n, pref)
    assert n % t == 0, (n, t)
    return t


def _ffn_kernel(x_ref, g_ref, wg_ref, wu_ref, wd_ref, fg_ref, o_ref, xn_ref, acc_ref, *, final_norm):
    j = pl.program_id(1)

    @pl.when(j == 0)
    def _():
        xn_ref[...] = _rmsnorm(x_ref[...], g_ref[...]).astype(BF16)
        acc_ref[...] = jnp.zeros_like(acc_ref)

    xn = xn_ref[...]
    a = _dot(xn, wg_ref[...])
    u = _dot(xn, wu_ref[...])
    h = (a * jax.nn.sigmoid(a)) * u
    acc_ref[...] += _dot(h.astype(BF16), wd_ref[...])

    @pl.when(j == pl.num_programs(1) - 1)
    def _():
        y = x_ref[...] + 0.5 * acc_ref[...]
        if final_norm:
            y = _rmsnorm(y, fg_ref[...])
        o_ref[...] = y


def _ffn(x, g, wg, wu, wd, fg, final_norm):
    t, d = x.shape
    f = wg.shape[1]
    tm = _pick(t, 512)
    tf = _pick(f, 512)
    return pl.pallas_call(
        functools.partial(_ffn_kernel, final_norm=final_norm),
        grid=(t // tm, f // tf),
        in_specs=[
            pl.BlockSpec((tm, d), lambda i, j: (i, 0)),
            pl.BlockSpec((1, d), lambda i, j: (0, 0)),
            pl.BlockSpec((d, tf), lambda i, j: (0, j)),
            pl.BlockSpec((d, tf), lambda i, j: (0, j)),
            pl.BlockSpec((tf, d), lambda i, j: (j, 0)),
            pl.BlockSpec((1, d), lambda i, j: (0, 0)),
        ],
        out_specs=pl.BlockSpec((tm, d), lambda i, j: (i, 0)),
        out_shape=jax.ShapeDtypeStruct((t, d), F32),
        scratch_shapes=[pltpu.VMEM((tm, d), BF16), pltpu.VMEM((tm, d), F32)],
        compiler_params=_params("parallel", "arbitrary"),
        name="ffn",
    )(x, g, wg, wu, wd, fg)


def _inproj_kernel(x_ref, g_ref, w_ref, wng_ref, proj_ref, hn_ref, ngt_ref, xn_ref):
    j = pl.program_id(1)

    @pl.when(j == 0)
    def _():
        xn = _rmsnorm(x_ref[...], g_ref[...]).astype(BF16)
        xn_ref[...] = xn
        hn_ref[...] = xn
        ngt_ref[...] = jax.nn.sigmoid(_dot(xn, wng_ref[...])).T

    proj_ref[...] = _dot(xn_ref[...], w_ref[...]).astype(BF16)


def _inproj(x, g, w, wng):
    t, d = x.shape
    n = w.shape[1]
    tm = _pick(t, 512)
    tn = _pick(n, 512)
    return pl.pallas_call(
        _inproj_kernel,
        grid=(t // tm, n // tn),
        in_specs=[
            pl.BlockSpec((tm, d), lambda i, j: (i, 0)),
            pl.BlockSpec((1, d), lambda i, j: (0, 0)),
            pl.BlockSpec((d, tn), lambda i, j: (0, j)),
            pl.BlockSpec((d, LANES), lambda i, j: (0, 0)),
        ],
        out_specs=[
            pl.BlockSpec((tm, tn), lambda i, j: (i, j)),
            pl.BlockSpec((tm, d), lambda i, j: (i, 0)),
            pl.BlockSpec((LANES, tm), lambda i, j: (0, i)),
        ],
        out_shape=[
            jax.ShapeDtypeStruct((t, n), BF16),
            jax.ShapeDtypeStruct((t, d), BF16),
            jax.ShapeDtypeStruct((LANES, t), F32),
        ],
        scratch_shapes=[pltpu.VMEM((tm, d), BF16)],
        compiler_params=_params("parallel", "arbitrary"),
        name="inproj",
    )(x, g, w, wng)


def _tile_update(z, c, vt, m_ref, l_ref, acc_ref, rate):
    m_old = m_ref[...]
    m_new = jnp.maximum(m_old, jnp.max(z, axis=0, keepdims=True) + c)
    p = jnp.exp2((z - (m_new - c)) * rate)
    alpha = jnp.exp2((m_old - m_new) * rate)
    l_ref[...] = alpha * l_ref[...] + jnp.sum(p, axis=0, keepdims=True)
    acc_ref[...] = alpha * acc_ref[...] + _dot(vt, p.astype(BF16))
    m_ref[...] = m_new


def _normalize(acc, l):
    return acc * (1.0 / jnp.maximum(l, 1e-30))


def _diff_kernel(slopes_ref, q_ref, k_ref, v_ref, lp_ref, sg_ref, o_ref,
                 vt_ref, base_ref, m_ref, l_ref, acc_ref, *, tile, n_tiles, lam_init):
    h = pl.program_id(1)
    qi = pl.program_id(2)
    rate = (DIFF_QK_DIM ** -0.5) * LOG2E
    slope = slopes_ref[h]

    @pl.when(qi == 0)
    def _():
        for t in range(n_tiles):
            vt_ref[t] = _transpose_bf16(v_ref[t * tile:(t + 1) * tile, :])
        base_ref[...] = slope * lax.broadcasted_iota(jnp.int32, (tile, tile), 0).astype(F32)

    qt = _transpose_bf16(q_ref[...])
    feat = lax.broadcasted_iota(jnp.int32, qt.shape, 0)
    qts = [jnp.where((feat >= m * DIFF_QK_DIM) & (feat < (m + 1) * DIFF_QK_DIM), qt, 0.0).astype(BF16)
           for m in range(2)]

    m_ref[...] = jnp.full_like(m_ref, NEG_INF)
    l_ref[...] = jnp.zeros_like(l_ref)
    acc_ref[...] = jnp.zeros_like(acc_ref)

    def tile_step(t, causal):
        ks = pl.multiple_of(t * tile, tile)
        k = k_ref[pl.ds(ks, tile), :]
        vt = vt_ref[t]
        c = slope * ((t - qi) * tile).astype(F32)
        for m in range(2):
            z = _dot(k, qts[m]) + base_ref[...]
            if causal:
                row = lax.broadcasted_iota(jnp.int32, z.shape, 0)
                col = lax.broadcasted_iota(jnp.int32, z.shape, 1)
                z = jnp.where(row <= col, z, NEG_INF)
            _tile_update(z, c, vt, m_ref.at[m], l_ref.at[m], acc_ref.at[m], rate)

    def body(t, carry):
        tile_step(t, False)
        return carry

    lax.fori_loop(0, qi, body, 0)
    tile_step(qi, True)

    o1 = _normalize(acc_ref[0], l_ref[0])
    o2 = _normalize(acc_ref[1], l_ref[1])
    lp = lp_ref[...]
    lam = (jnp.exp(jnp.sum(lp[0:1] * lp[1:2], axis=-1, keepdims=True))
           - jnp.exp(jnp.sum(lp[2:3] * lp[3:4], axis=-1, keepdims=True)) + lam_init)
    ot = o1 - lam * o2
    ot = ot * lax.rsqrt(jnp.mean(ot * ot, axis=0, keepdims=True) + NORM_EPS)
    o_ref[...] = (ot.T * sg_ref[...] * (1.0 - lam_init)).astype(BF16)


def _diff_attention(proj, lp, sg, batch, seq, lam_init):
    t = proj.shape[0]
    tile = _pick(seq, 512)
    nq = seq // tile
    kb, vb = C_DK // LANES, C_DV // LANES
    slopes = jnp.asarray(_alibi_slopes(DIFF_HEADS) / np.float32(DIFF_QK_DIM ** -0.5))
    return pl.pallas_call(
        functools.partial(_diff_kernel, tile=tile, n_tiles=nq, lam_init=lam_init),
        grid=(batch, DIFF_HEADS, nq),
        in_specs=[
            pl.BlockSpec(memory_space=pltpu.SMEM),
            pl.BlockSpec((tile, LANES), lambda b, h, i: (b * nq + i, h)),
            pl.BlockSpec((seq, LANES), lambda b, h, i: (b, kb + h)),
            pl.BlockSpec((seq, LANES), lambda b, h, i: (b, vb + h)),
            pl.BlockSpec((4, DIFF_QK_DIM), lambda b, h, i: (0, 0)),
            pl.BlockSpec((1, DIFF_V_DIM), lambda b, h, i: (0, 0)),
        ],
        out_specs=pl.BlockSpec((tile, LANES), lambda b, h, i: (b * nq + i, h)),
        out_shape=jax.ShapeDtypeStruct((t, DIFF_HEADS * DIFF_V_DIM), BF16),
        scratch_shapes=[
            pltpu.VMEM((nq, DIFF_V_DIM, tile), BF16),
            pltpu.VMEM((tile, tile), F32),
            pltpu.VMEM((2, 1, tile), F32),
            pltpu.VMEM((2, 1, tile), F32),
            pltpu.VMEM((2, DIFF_V_DIM, tile), F32),
        ],
        compiler_params=_params("parallel", "parallel", "arbitrary"),
        name="diff_attn",
    )(slopes, proj, proj, proj, lp, sg)


def _compress_kernel(kv_ref, pos_ref, w1_ref, w2_ref, o_ref, ot_ref):
    half = w1_ref.shape[0] // 2
    kv = kv_ref[...]
    top = _dot(kv, w1_ref[:half, :])
    bot = _dot(kv, w1_ref[half:, :])
    rows = kv.shape[0]
    bot_next = pltpu.roll(bot, rows - 1, 0)
    pos_term = _dot(pos_ref[...], w1_ref[...])[0:1]
    hid = jax.nn.gelu(top + bot_next + pos_term)
    out = _dot(hid.astype(BF16), w2_ref[...])
    o_ref[...] = out.astype(BF16)
    ot_ref[...] = out.T.astype(BF16)


def _compress(kvr, pos, w1, w2):
    _, groups, batch, rows, width = kvr.shape
    d = w2.shape[-1]
    return pl.pallas_call(
        _compress_kernel,
        grid=(2, groups, batch),
        in_specs=[
            pl.BlockSpec((None, None, None, rows, width), lambda a, g, b: (a, g, b, 0, 0)),
            pl.BlockSpec((None, 8, 2 * width), lambda a, g, b: (a, 0, 0)),
            pl.BlockSpec((None, 2 * width, d), lambda a, g, b: (a, 0, 0)),
            pl.BlockSpec((None, d, d), lambda a, g, b: (a, 0, 0)),
        ],
        out_specs=[
            pl.BlockSpec((None, None, None, rows, d), lambda a, g, b: (a, b, g, 0, 0)),
            pl.BlockSpec((None, None, None, d, rows), lambda a, g, b: (a, b, g, 0, 0)),
        ],
        out_shape=[
            jax.ShapeDtypeStruct((2, batch, groups, rows, d), BF16),
            jax.ShapeDtypeStruct((2, batch, groups, d, rows), BF16),
        ],
        compiler_params=_params("parallel", "parallel", "parallel"),
        name="compress",
    )(kvr, pos, w1, w2)


def _nsa_kernel(slopes_ref, q_ref, gate_ref, kc_ref, vct_ref, ks_ref, vs_ref, kw_ref, vw_ref,
                ovt_ref, o_ref, vst_ref, vwt_ref, base_ref, m_ref, l_ref, acc_ref, *, tq, tk, seq):
    g = pl.program_id(1)
    qi = pl.program_id(2)
    q0 = qi * tq
    hp = NSA_HPG
    n = hp * tq
    rate = (NSA_DIM ** -0.5) * LOG2E
    span = WIN_LEN + tq
    n_slc = seq // SLC_LEN
    slope_row = jnp.concatenate(
        [jnp.full((1, tq), slopes_ref[g * hp + h], F32) for h in range(hp)], axis=1)

    @pl.when(qi == 0)
    def _():
        for t in range(seq // tk):
            vst_ref[t] = _transpose_bf16(vs_ref[t * tk:(t + 1) * tk, :])
        for t in range(seq // tq):
            vwt_ref[t] = _transpose_bf16(vw_ref[t * tq:(t + 1) * tq, :])
        base_ref[...] = slope_row * lax.broadcasted_iota(jnp.int32, (span, n), 0).astype(F32)

    q = q_ref[...]
    q4t = jnp.concatenate(
        [q[:, h * NSA_DIM:(h + 1) * NSA_DIM].astype(F32).T for h in range(hp)], axis=1).astype(BF16)
    qpos_all = q0 + (lax.broadcasted_iota(jnp.int32, (1, n), 1) & (tq - 1))
    qpos = q0 + lax.broadcasted_iota(jnp.int32, (1, tq), 1)

    def tile_heads(a):
        return jnp.concatenate([a] * hp, axis=1)

    n_cmp = kc_ref.shape[0]
    cmp_end = lax.broadcasted_iota(jnp.int32, (n_cmp, 1), 0) * CMP_STRIDE + (CMP_LEN - 1)
    mask_c = cmp_end <= qpos_all
    z_c = jnp.where(mask_c, _dot(kc_ref[...], q4t), NEG_INF)
    m_c = jnp.max(z_c, axis=0, keepdims=True)
    e_c = jnp.where(mask_c, jnp.exp2((z_c - m_c) * rate), 0.0)
    p_c = _normalize(e_c, jnp.sum(e_c, axis=0, keepdims=True))
    o_c = _dot(vct_ref[...], p_c.astype(BF16))

    p_sum = p_c[:, 0:tq]
    for h in range(1, hp):
        p_sum = p_sum + p_c[:, h * tq:(h + 1) * tq]
    ovt = ovt_ref[...]
    p_hi = p_sum.astype(BF16)
    r1 = p_sum - p_hi.astype(F32)
    p_mid = r1.astype(BF16)
    p_lo = (r1 - p_mid.astype(F32)).astype(BF16)
    imp = _dot(ovt, p_hi) + _dot(ovt, p_mid) + _dot(ovt, p_lo)
    jblk = lax.broadcasted_iota(jnp.int32, (n_slc, tq), 0)
    cur = qpos // SLC_LEN
    forced = (jblk == 0) | (jblk == cur) | (jblk == cur - 1)
    imp = jnp.where(forced, SEL_BIG, jnp.where(jblk > cur, -SEL_BIG, imp))
    jblk_f = jblk.astype(F32)
    sel = jnp.zeros((n_slc, tq), F32)
    for _ in range(min(SLC_TOPK, n_slc)):
        mx = jnp.max(imp, axis=0, keepdims=True)
        first = jnp.min(jnp.where(imp == mx, jblk_f, float(n_slc)), axis=0, keepdims=True)
        hit = jblk_f == first
        sel = jnp.where(hit, 1.0, sel)
        imp = jnp.where(hit, -jnp.inf, imp)
    sel_bf = sel.astype(BF16)

    m_ref[...] = jnp.full_like(m_ref, NEG_INF)
    l_ref[...] = jnp.zeros_like(l_ref)
    acc_ref[...] = jnp.zeros_like(acc_ref)
    exp_blk = lax.broadcasted_iota(jnp.int32, (tk, n_slc), 0) // SLC_LEN
    exp_col = lax.broadcasted_iota(jnp.int32, (tk, n_slc), 1)
    krow = lax.broadcasted_iota(jnp.int32, (tk, 1), 0)

    def step(t, carry):
        ks = pl.multiple_of(t * tk, tk)
        expand = jnp.where(exp_col == exp_blk + t * (tk // SLC_LEN), 1.0, 0.0).astype(BF16)
        chosen = _dot(expand, sel_bf)
        pen = jnp.where((chosen > 0.5) & ((krow + ks) <= qpos), 0.0, NEG_INF)
        z = _dot(ks_ref[pl.ds(ks, tk), :], q4t) + base_ref[0:tk, :] + tile_heads(pen)
        c = slope_row * (ks - q0).astype(F32)
        _tile_update(z, c, vst_ref[t], m_ref, l_ref, acc_ref, rate)
        return carry

    lax.fori_loop(0, q0 // tk + 1, step, 0)
    o_s = _normalize(acc_ref[...], l_ref[...])

    wt = jnp.maximum(qi - WIN_LEN // tq, 0)
    w0 = pl.multiple_of(wt * tq, tq)
    dist = qpos - (w0 + lax.broadcasted_iota(jnp.int32, (span, 1), 0))
    pen_w = jnp.where((dist >= 0) & (dist < WIN_LEN), 0.0, NEG_INF)
    z_w = _dot(kw_ref[pl.ds(w0, span), :], q4t) + base_ref[...] + tile_heads(pen_w)
    m_w = jnp.max(z_w, axis=0, keepdims=True)
    e_w = jnp.exp2((z_w - m_w) * rate)
    l_w = jnp.sum(e_w, axis=0, keepdims=True)
    e_bf = e_w.astype(BF16)
    o_w = _dot(vwt_ref[wt], e_bf[0:tq, :])
    for j in range(1, span // tq):
        o_w = o_w + _dot(vwt_ref[wt + j], e_bf[j * tq:(j + 1) * tq, :])
    o_w = _normalize(o_w, l_w)

    def gate_row(branch):
        return jnp.concatenate(
            [gate_ref[pl.ds((g * hp + h) * 3 + branch, 1), :] for h in range(hp)], axis=1)

    ot = gate_row(0) * o_c + gate_row(1) * o_s + gate_row(2) * o_w
    o_ref[...] = jnp.concatenate(
        [ot[:, h * tq:(h + 1) * tq].T for h in range(hp)], axis=1).astype(BF16)


def _nsa_attention(proj, gates_t, k_cmp, vt_cmp, overlap_t, batch, seq):
    t = proj.shape[0]
    tq = _pick(seq, 128)
    tk = _pick(seq, 512)
    assert seq >= WIN_LEN + tq and WIN_LEN % tq == 0 and tk % tq == 0
    nq = seq // tq
    hp = NSA_HPG
    qb = C_NQ // (hp * NSA_DIM)
    kvb = C_NKV // LANES
    n_cmp = k_cmp.shape[3]
    slopes = jnp.asarray(_alibi_slopes(NSA_HEADS) / np.float32(NSA_DIM ** -0.5))

    def kv_spec(branch, which):
        base = kvb + (branch * 2 + which) * NSA_GROUPS
        return pl.BlockSpec((seq, LANES), lambda b, g, i: (b, base + g))

    return pl.pallas_call(
        functools.partial(_nsa_kernel, tq=tq, tk=tk, seq=seq),
        grid=(batch, NSA_GROUPS, nq),
        in_specs=[
            pl.BlockSpec(memory_space=pltpu.SMEM),
            pl.BlockSpec((tq, hp * NSA_DIM), lambda b, g, i: (b * nq + i, qb + g)),
            pl.BlockSpec((LANES, tq), lambda b, g, i: (0, b * nq + i)),
            pl.BlockSpec((None, None, None, n_cmp, NSA_DIM), lambda b, g, i: (0, b, g, 0, 0)),
            pl.BlockSpec((None, None, None, NSA_DIM, n_cmp), lambda b, g, i: (1, b, g, 0, 0)),
            kv_spec(1, 0), kv_spec(1, 1),
            kv_spec(2, 0), kv_spec(2, 1),
            pl.BlockSpec(overlap_t.shape, lambda b, g, i: (0, 0)),
        ],
        out_specs=pl.BlockSpec((tq, hp * NSA_DIM), lambda b, g, i: (b * nq + i, g)),
        out_shape=jax.ShapeDtypeStruct((t, NSA_HEADS * NSA_DIM), BF16),
        scratch_shapes=[
            pltpu.VMEM((seq // tk, NSA_DIM, tk), BF16),
            pltpu.VMEM((seq // tq, NSA_DIM, tq), BF16),
            pltpu.VMEM((WIN_LEN + tq, hp * tq), F32),
            pltpu.VMEM((1, hp * tq), F32),
            pltpu.VMEM((1, hp * tq), F32),
            pltpu.VMEM((NSA_DIM, hp * tq), F32),
        ],
        compiler_params=_params("parallel", "parallel", "arbitrary"),
        name="nsa_attn",
    )(slopes, proj, gates_t, k_cmp, vt_cmp, proj, proj, proj, proj, overlap_t)


def _memkv_kernel(mem_ref, g_ref, w_ref, o_ref):
    mn = _rmsnorm(mem_ref[...], g_ref[...]).astype(BF16)
    o_ref[...] = _dot(mn, w_ref[...]).astype(BF16)


def _memkv(mem, g, w):
    r, d = mem.shape
    n = w.shape[1]
    tm = _pick(r, 256)
    return pl.pallas_call(
        _memkv_kernel,
        grid=(r // tm,),
        in_specs=[
            pl.BlockSpec((tm, d), lambda i: (i, 0)),
            pl.BlockSpec((1, d), lambda i: (0, 0)),
            pl.BlockSpec((d, n), lambda i: (0, 0)),
        ],
        out_specs=pl.BlockSpec((tm, n), lambda i: (i, 0)),
        out_shape=jax.ShapeDtypeStruct((r, n), BF16),
        compiler_params=_params("parallel"),
        name="mem_kv",
    )(mem, g, w)


def _memattn_kernel(q_ref, k_ref, v_ref, o_ref):
    scale = MEM_DIM ** -0.5
    q = q_ref[...]
    k = k_ref[...]
    v = v_ref[...]
    outs = []
    for h in range(MEM_HEADS):
        sl = slice(h * MEM_DIM, (h + 1) * MEM_DIM)
        s = _dot_nt(q[:, sl], k[:, sl]) * scale
        m = jnp.max(s, axis=-1, keepdims=True)
        e = jnp.exp(s - m)
        p = e / jnp.sum(e, axis=-1, keepdims=True)
        outs.append(_dot(p.astype(BF16), v[:, sl]))
    o_ref[...] = jnp.concatenate(outs, axis=-1).astype(BF16)


def _mem_attention(proj, memkv, batch, seq, mem_len):
    t = proj.shape[0]
    tq = _pick(seq, 512)
    nq = seq // tq
    width = MEM_HEADS * MEM_DIM
    qb = P_MQ // width
    return pl.pallas_call(
        _memattn_kernel,
        grid=(batch, nq),
        in_specs=[
            pl.BlockSpec((tq, width), lambda b, i: (b * nq + i, qb)),
            pl.BlockSpec((mem_len, width), lambda b, i: (b, 0)),
            pl.BlockSpec((mem_len, width), lambda b, i: (b, 1)),
        ],
        out_specs=pl.BlockSpec((tq, width), lambda b, i: (b * nq + i, 0)),
        out_shape=jax.ShapeDtypeStruct((t, width), BF16),
        compiler_params=_params("parallel", "parallel"),
        name="mem_attn",
    )(proj, memkv, memkv)


def _merge_kernel(x_ref, hn_ref, od_ref, on_ref, om_ref, wgd_ref, wgn_ref, wgm_ref,
                  wud_ref, wun_ref, wum_ref, wo_ref, o_ref, acc_ref):
    j = pl.program_id(1)

    @pl.when(j == 0)
    def _():
        acc_ref[...] = jnp.zeros_like(acc_ref)

    hn = hn_ref[...]
    merged = jax.nn.sigmoid(_dot(hn, wgd_ref[...])) * _dot(od_ref[...], wud_ref[...])
    merged += jax.nn.sigmoid(_dot(hn, wgn_ref[...])) * _dot(on_ref[...], wun_ref[...])
    merged += jax.nn.sigmoid(_dot(hn, wgm_ref[...])) * _dot(om_ref[...], wum_ref[...])
    acc_ref[...] += _dot(merged.astype(BF16), wo_ref[...])

    @pl.when(j == pl.num_programs(1) - 1)
    def _():
        o_ref[...] = x_ref[...] + acc_ref[...]


def _merge(x, hn, od, on, om, wg, wud, wun, wum, wo):
    t, d = x.shape
    tm = _pick(t, 512)
    tj = _pick(d, 512)
    nj = d // tj

    def gate_spec(branch):
        return pl.BlockSpec((d, tj), lambda i, j: (0, branch * nj + j))

    return pl.pallas_call(
        _merge_kernel,
        grid=(t // tm, nj),
        in_specs=[
            pl.BlockSpec((tm, d), lambda i, j: (i, 0)),
            pl.BlockSpec((tm, d), lambda i, j: (i, 0)),
            pl.BlockSpec((tm, od.shape[1]), lambda i, j: (i, 0)),
            pl.BlockSpec((tm, on.shape[1]), lambda i, j: (i, 0)),
            pl.BlockSpec((tm, om.shape[1]), lambda i, j: (i, 0)),
            gate_spec(0), gate_spec(1), gate_spec(2),
            pl.BlockSpec((wud.shape[0], tj), lambda i, j: (0, j)),
            pl.BlockSpec((wun.shape[0], tj), lambda i, j: (0, j)),
            pl.BlockSpec((wum.shape[0], tj), lambda i, j: (0, j)),
            pl.BlockSpec((tj, d), lambda i, j: (j, 0)),
        ],
        out_specs=pl.BlockSpec((tm, d), lambda i, j: (i, 0)),
        out_shape=jax.ShapeDtypeStruct((t, d), F32),
        scratch_shapes=[pltpu.VMEM((tm, d), F32)],
        compiler_params=_params("parallel", "arbitrary"),
        name="merge",
    )(x, hn, od, on, om, wg, wg, wg, wud, wun, wum, wo)


def _overlap_matrix_t(n_slc, n_cmp):
    cmp_start = np.arange(n_cmp)[None, :] * CMP_STRIDE
    slc_start = np.arange(n_slc)[:, None] * SLC_LEN
    return ((cmp_start < slc_start + SLC_LEN) & (cmp_start + CMP_LEN > slc_start)).astype(np.float32)


def kernel(x, mem, ffn1_norm, ffn1_w_gate, ffn1_w_up, ffn1_w_down, mix_norm, w_in, diff_lambda, diff_subln, nsa_cmp_pos, nsa_cmp_w1, nsa_cmp_w2, mem_norm, w_mem_kv, w_up_diff, w_up_nsa, w_up_mem, w_out, ffn2_norm, ffn2_w_gate, ffn2_w_up, ffn2_w_down, final_norm):
    batch, seq, d = x.shape
    mem_len = mem.shape[1]
    depth = w_in.shape[0]
    t = batch * seq
    assert seq % CMP_STRIDE == 0 and CMP_LEN == 2 * CMP_STRIDE
    n_rows = seq // CMP_STRIDE
    n_slc = seq // SLC_LEN
    overlap_t = jnp.asarray(_overlap_matrix_t(n_slc, n_rows), BF16)

    xt = x.reshape(t, d)
    memt = mem.reshape(batch * mem_len, d)
    fg = final_norm.reshape(1, d)
    bf = lambda a: a.astype(BF16)

    for l in range(depth):
        lam_init = 0.8 - 0.6 * math.exp(-0.3 * l)
        xt = _ffn(xt, ffn1_norm[l].reshape(1, d), bf(ffn1_w_gate[l]), bf(ffn1_w_up[l]),
                  bf(ffn1_w_down[l]), fg, False)

        w = w_in[l]
        w_main = bf(jnp.concatenate([w[:, :C_NG], w[:, C_MQ:C_MG]], axis=1))
        w_ng = bf(jnp.pad(w[:, C_NG:C_MQ], ((0, 0), (0, LANES - NSA_GATE_COLS))))
        proj, hn, ngt = _inproj(xt, mix_norm[l].reshape(1, d), w_main, w_ng)

        o_diff = _diff_attention(proj, diff_lambda[l], diff_subln[l].reshape(1, DIFF_V_DIM),
                                 batch, seq, lam_init)

        ckv = proj[:, C_NKV:C_NKV + 2 * NSA_GROUPS * NSA_DIM]
        ckv = ckv.reshape(batch, n_rows, CMP_STRIDE, 2, NSA_GROUPS, NSA_DIM)
        ckv = ckv.transpose(3, 4, 0, 1, 2, 5).reshape(2, NSA_GROUPS, batch, n_rows, CMP_STRIDE * NSA_DIM)
        pos = bf(jnp.broadcast_to(nsa_cmp_pos[l].reshape(2, 1, CMP_LEN * NSA_DIM), (2, 8, CMP_LEN * NSA_DIM)))
        kv_cmp, kvt_cmp = _compress(ckv, pos, bf(nsa_cmp_w1[l]), bf(nsa_cmp_w2[l]))
        o_nsa = _nsa_attention(proj, ngt, kv_cmp, kvt_cmp, overlap_t, batch, seq)

        memkv = _memkv(memt, mem_norm[l].reshape(1, d), bf(w_mem_kv[l]))
        o_mem = _mem_attention(proj, memkv, batch, seq, mem_len)

        w_mg = bf(w[:, C_MG:])
        xt = _merge(xt, hn, o_diff, o_nsa, o_mem, w_mg, bf(w_up_diff[l]), bf(w_up_nsa[l]),
                    bf(w_up_mem[l]), bf(w_out[l]))

        xt = _ffn(xt, ffn2_norm[l].reshape(1, d), bf(ffn2_w_gate[l]), bf(ffn2_w_up[l]),
                  bf(ffn2_w_down[l]), fg, l == depth - 1)
    return xt.reshape(batch, seq, d)
```

```python
import functools
import math

import jax
import jax.numpy as jnp
import numpy as np
from jax import lax
from jax.experimental import pallas as pl
from jax.experimental.pallas import tpu as pltpu

F32 = jnp.float32
BF16 = jnp.bfloat16

NORM_EPS = 1e-6
NEG_INF = -1e30
SEL_BIG = 1e9
LOG2E = math.log2(math.e)

DIFF_HEADS = 8
DIFF_QK_DIM = 64
DIFF_V_DIM = 2 * DIFF_QK_DIM

NSA_HEADS = 8
NSA_GROUPS = 2
NSA_HPG = NSA_HEADS // NSA_GROUPS
NSA_DIM = 128
CMP_LEN = 32
CMP_STRIDE = 16
SLC_LEN = 64
SLC_TOPK = 8
WIN_LEN = 512

MEM_HEADS = 4
MEM_DIM = 128
N_BRANCH = 3

LANES = 128
VMEM_LIMIT = 56 * 1024 * 1024

DIFF_QK_COLS = DIFF_HEADS * 2 * DIFF_QK_DIM
DIFF_V_COLS = DIFF_HEADS * DIFF_V_DIM
NSA_Q_COLS = NSA_HEADS * NSA_DIM
NSA_KV_COLS = 3 * 2 * NSA_GROUPS * NSA_DIM
NSA_GATE_COLS = NSA_HEADS * 3
MEM_Q_COLS = MEM_HEADS * MEM_DIM
C_DQ = 0
C_DK = C_DQ + DIFF_QK_COLS
C_DV = C_DK + DIFF_QK_COLS
C_NQ = C_DV + DIFF_V_COLS
C_NKV = C_NQ + NSA_Q_COLS
C_NG = C_NKV + NSA_KV_COLS
C_MQ = C_NG + NSA_GATE_COLS
C_MG = C_MQ + MEM_Q_COLS
P_MQ = C_NG
P_COLS = P_MQ + MEM_Q_COLS


def _alibi_slopes(n_heads):
    return np.array([2.0 ** (-8.0 * (h + 1) / n_heads) for h in range(n_heads)], dtype=np.float32)


def _rmsnorm(x, g):
    y = x * lax.rsqrt(jnp.mean(x * x, axis=-1, keepdims=True) + NORM_EPS)
    return y * g


def _dot(a, b):
    return jnp.dot(a, b, preferred_element_type=F32)


def _dot_nt(a, b):
    return lax.dot_general(a, b, (((1,), (1,)), ((), ())), preferred_element_type=F32)


def _transpose_bf16(x):
    return x.astype(F32).T.astype(BF16)


def _params(*sem):
    return pltpu.CompilerParams(dimension_semantics=sem, vmem_limit_bytes=VMEM_LIMIT)


def _pick(n, pref):
    t = min(n, pref)
    assert n % t == 0, (n, t)
    return t


def _ffn_kernel(x_ref, g_ref, wg_ref, wu_ref, wd_ref, fg_ref, o_ref, xn_ref, acc_ref, *, final_norm):
    j = pl.program_id(1)

    @pl.when(j == 0)
    def _():
        xn_ref[...] = _rmsnorm(x_ref[...], g_ref[...]).astype(BF16)
        acc_ref[...] = jnp.zeros_like(acc_ref)

    xn = xn_ref[...]
    a = _dot(xn, wg_ref[...])
    u = _dot(xn, wu_ref[...])
    h = (a * jax.nn.sigmoid(a)) * u
    acc_ref[...] += _dot(h.astype(BF16), wd_ref[...])

    @pl.when(j == pl.num_programs(1) - 1)
    def _():
        y = x_ref[...] + 0.5 * acc_ref[...]
        if final_norm:
            y = _rmsnorm(y, fg_ref[...])
        o_ref[...] = y


def _ffn(x, g, wg, wu, wd, fg, final_norm):
    t, d = x.shape
    f = wg.shape[1]
    tm = _pick(t, 512)
    tf = _pick(f, 512)
    return pl.pallas_call(
        functools.partial(_ffn_kernel, final_norm=final_norm),
        grid=(t // tm, f // tf),
        in_specs=[
            pl.BlockSpec((tm, d), lambda i, j: (i, 0)),
            pl.BlockSpec((1, d), lambda i, j: (0, 0)),
            pl.BlockSpec((d, tf), lambda i, j: (0, j)),
            pl.BlockSpec((d, tf), lambda i, j: (0, j)),
            pl.BlockSpec((tf, d), lambda i, j: (j, 0)),
            pl.BlockSpec((1, d), lambda i, j: (0, 0)),
        ],
        out_specs=pl.BlockSpec((tm, d), lambda i, j: (i, 0)),
        out_shape=jax.ShapeDtypeStruct((t, d), F32),
        scratch_shapes=[pltpu.VMEM((tm, d), BF16), pltpu.VMEM((tm, d), F32)],
        compiler_params=_params("parallel", "arbitrary"),
        name="ffn",
    )(x, g, wg, wu, wd, fg)


def _inproj_kernel(x_ref, g_ref, w_ref, wng_ref, cs_ref, proj_ref, hn_ref, ngt_ref, xn_ref):
    j = pl.program_id(1)

    @pl.when(j == 0)
    def _():
        xn = _rmsnorm(x_ref[...], g_ref[...]).astype(BF16)
        xn_ref[...] = xn
        hn_ref[...] = xn
        ngt_ref[...] = jax.nn.sigmoid(_dot(xn, wng_ref[...])).T

    proj_ref[...] = (_dot(xn_ref[...], w_ref[...]) * cs_ref[...]).astype(BF16)


def _inproj(x, g, w, wng, cs):
    t, d = x.shape
    n = w.shape[1]
    tm = _pick(t, 512)
    tn = _pick(n, 512)
    return pl.pallas_call(
        _inproj_kernel,
        grid=(t // tm, n // tn),
        in_specs=[
            pl.BlockSpec((tm, d), lambda i, j: (i, 0)),
            pl.BlockSpec((1, d), lambda i, j: (0, 0)),
            pl.BlockSpec((d, tn), lambda i, j: (0, j)),
            pl.BlockSpec((d, LANES), lambda i, j: (0, 0)),
            pl.BlockSpec((1, tn), lambda i, j: (0, j)),
        ],
        out_specs=[
            pl.BlockSpec((tm, tn), lambda i, j: (i, j)),
            pl.BlockSpec((tm, d), lambda i, j: (i, 0)),
            pl.BlockSpec((LANES, tm), lambda i, j: (0, i)),
        ],
        out_shape=[
            jax.ShapeDtypeStruct((t, n), BF16),
            jax.ShapeDtypeStruct((t, d), BF16),
            jax.ShapeDtypeStruct((LANES, t), F32),
        ],
        scratch_shapes=[pltpu.VMEM((tm, d), BF16)],
        compiler_params=_params("parallel", "arbitrary"),
        name="inproj",
    )(x, g, w, wng, cs)


SUB = 128
ROWS_L = 16


def _assert_slopes_fit(slopes2):
    assert float(np.max(slopes2)) * SUB < 100.0, "ALiBi factor per bias block leaves the safe f32 range"


def _bias_factor_row(slope2, width):
    r = lax.broadcasted_iota(jnp.int32, (1, width), 1) & (SUB - 1)
    return jnp.exp2(slope2 * r.astype(F32))


def _scaled_vt(vt, w_row):
    ones = jnp.broadcast_to(w_row, (ROWS_L, w_row.shape[1]))
    return jnp.concatenate([vt * w_row, ones], axis=0).astype(BF16)


def _block_weights(z, off0, step, m_prev):
    nb = z.shape[0] // SUB
    blocks = [z[j * SUB:(j + 1) * SUB] for j in range(nb)]
    offs = [off0 + j * step for j in range(nb)]
    m_new = m_prev
    for zb, off in zip(blocks, offs):
        m_new = jnp.maximum(m_new, jnp.max(zb, axis=0, keepdims=True) + off)
    p = [jnp.exp2(zb - (m_new - off)).astype(BF16) for zb, off in zip(blocks, offs)]
    return m_new, jnp.concatenate(p, axis=0)


def _split_acc(acc, dv):
    return acc[0:dv] * (1.0 / jnp.maximum(acc[dv:dv + 1], 1e-30))


def _diff_kernel(slopes_ref, q_ref, k_ref, v_ref, lp_ref, sg_ref, o_ref,
                 vt_ref, m_ref, acc_ref, *, tile, n_tiles, lam_init):
    h = pl.program_id(1)
    qi = pl.program_id(2)
    slope2 = slopes_ref[h]

    @pl.when(qi == 0)
    def _():
        w_row = _bias_factor_row(slope2, tile)
        for t in range(n_tiles):
            vt_ref[t] = _scaled_vt(v_ref[t * tile:(t + 1) * tile, :].astype(F32).T, w_row)

    qt = _transpose_bf16(q_ref[...])
    feat = lax.broadcasted_iota(jnp.int32, qt.shape, 0)
    qts = [jnp.where((feat >= m * DIFF_QK_DIM) & (feat < (m + 1) * DIFF_QK_DIM), qt, 0.0).astype(BF16)
           for m in range(2)]

    m_ref[...] = jnp.full_like(m_ref, NEG_INF)
    acc_ref[...] = jnp.zeros_like(acc_ref)
    step = slope2 * float(SUB)

    def tile_step(t, causal):
        ks = pl.multiple_of(t * tile, tile)
        k = k_ref[pl.ds(ks, tile), :]
        vt = vt_ref[t]
        off0 = slope2 * ((t - qi) * tile).astype(F32)
        zs = [_dot(k, qts[m]) for m in range(2)]
        if causal:
            row = lax.broadcasted_iota(jnp.int32, zs[0].shape, 0)
            col = lax.broadcasted_iota(jnp.int32, zs[0].shape, 1)
            zs = [jnp.where(row <= col, z, NEG_INF) for z in zs]
        m_old = [m_ref[m] for m in range(2)]
        mp = [_block_weights(zs[m], off0, step, m_old[m]) for m in range(2)]
        for m in range(2):
            m_new, p = mp[m]
            acc_ref[m] = jnp.exp2(m_old[m] - m_new) * acc_ref[m] + _dot(vt, p)
            m_ref[m] = m_new

    def body(t, carry):
        tile_step(t, False)
        return carry

    lax.fori_loop(0, qi, body, 0)
    tile_step(qi, True)

    o1 = _split_acc(acc_ref[0], DIFF_V_DIM)
    o2 = _split_acc(acc_ref[1], DIFF_V_DIM)
    lp = lp_ref[...]
    lam = (jnp.exp(jnp.sum(lp[0:1] * lp[1:2], axis=-1, keepdims=True))
           - jnp.exp(jnp.sum(lp[2:3] * lp[3:4], axis=-1, keepdims=True)) + lam_init)
    ot = o1 - lam * o2
    ot = ot * lax.rsqrt(jnp.mean(ot * ot, axis=0, keepdims=True) + NORM_EPS)
    o_ref[...] = (ot.T * sg_ref[...] * (1.0 - lam_init)).astype(BF16)


def _diff_attention(proj, lp, sg, batch, seq, lam_init):
    t = proj.shape[0]
    tile = _pick(seq, 512)
    nq = seq // tile
    kb, vb = C_DK // LANES, C_DV // LANES
    slopes2 = _alibi_slopes(DIFF_HEADS) * np.float32(LOG2E)
    _assert_slopes_fit(slopes2)
    slopes = jnp.asarray(slopes2)
    assert tile % SUB == 0
    return pl.pallas_call(
        functools.partial(_diff_kernel, tile=tile, n_tiles=nq, lam_init=lam_init),
        grid=(batch, DIFF_HEADS, nq),
        in_specs=[
            pl.BlockSpec(memory_space=pltpu.SMEM),
            pl.BlockSpec((tile, LANES), lambda b, h, i: (b * nq + i, h)),
            pl.BlockSpec((seq, LANES), lambda b, h, i: (b, kb + h)),
            pl.BlockSpec((seq, LANES), lambda b, h, i: (b, vb + h)),
            pl.BlockSpec((4, DIFF_QK_DIM), lambda b, h, i: (0, 0)),
            pl.BlockSpec((1, DIFF_V_DIM), lambda b, h, i: (0, 0)),
        ],
        out_specs=pl.BlockSpec((tile, LANES), lambda b, h, i: (b * nq + i, h)),
        out_shape=jax.ShapeDtypeStruct((t, DIFF_HEADS * DIFF_V_DIM), BF16),
        scratch_shapes=[
            pltpu.VMEM((nq, DIFF_V_DIM + ROWS_L, tile), BF16),
            pltpu.VMEM((2, 1, tile), F32),
            pltpu.VMEM((2, DIFF_V_DIM + ROWS_L, tile), F32),
        ],
        compiler_params=_params("parallel", "parallel", "arbitrary"),
        name="diff_attn",
    )(slopes, proj, proj, proj, lp, sg)


def _compress_kernel(kv_ref, pos_ref, w1_ref, w2_ref, o_ref, ot_ref):
    half = w1_ref.shape[0] // 2
    kv = kv_ref[...]
    top = _dot(kv, w1_ref[:half, :])
    bot = _dot(kv, w1_ref[half:, :])
    rows = kv.shape[0]
    bot_next = pltpu.roll(bot, rows - 1, 0)
    pos_term = _dot(pos_ref[...], w1_ref[...])[0:1]
    hid = jax.nn.gelu(top + bot_next + pos_term)
    out = _dot(hid.astype(BF16), w2_ref[...])
    o_ref[...] = out.astype(BF16)
    ot_ref[...] = out.T.astype(BF16)


def _compress(kvr, pos, w1, w2):
    _, groups, batch, rows, width = kvr.shape
    d = w2.shape[-1]
    return pl.pallas_call(
        _compress_kernel,
        grid=(2, groups, batch),
        in_specs=[
            pl.BlockSpec((None, None, None, rows, width), lambda a, g, b: (a, g, b, 0, 0)),
            pl.BlockSpec((None, 8, 2 * width), lambda a, g, b: (a, 0, 0)),
            pl.BlockSpec((None, 2 * width, d), lambda a, g, b: (a, 0, 0)),
            pl.BlockSpec((None, d, d), lambda a, g, b: (a, 0, 0)),
        ],
        out_specs=[
            pl.BlockSpec((None, None, None, rows, d), lambda a, g, b: (a, b, g, 0, 0)),
            pl.BlockSpec((None, None, None, d, rows), lambda a, g, b: (a, b, g, 0, 0)),
        ],
        out_shape=[
            jax.ShapeDtypeStruct((2, batch, groups, rows, d), BF16),
            jax.ShapeDtypeStruct((2, batch, groups, d, rows), BF16),
        ],
        compiler_params=_params("parallel", "parallel", "parallel"),
        name="compress",
    )(kvr, pos, w1, w2)


def _nsa_kernel(slopes_ref, q_ref, gate_ref, kc_ref, vct_ref, ks_ref, vs_ref, kw_ref, vw_ref,
                ovt_ref, o_ref, vst_ref, vwt_ref, m_ref, acc_ref, *, tq, tk, seq):
    g = pl.program_id(1)
    qi = pl.program_id(2)
    q0 = qi * tq
    hp = NSA_HPG
    n = hp * tq
    dv = NSA_DIM
    span = WIN_LEN + tq
    n_slc = seq // SLC_LEN
    slope_row = jnp.concatenate(
        [jnp.full((1, tq), slopes_ref[g * hp + h], F32) for h in range(hp)], axis=1)

    @pl.when(qi == 0)
    def _():
        w_s = [_bias_factor_row(slopes_ref[g * hp + h], tk) for h in range(hp)]
        w_w = [_bias_factor_row(slopes_ref[g * hp + h], tq) for h in range(hp)]
        for t in range(seq // tk):
            vt = vs_ref[t * tk:(t + 1) * tk, :].astype(F32).T
            for h in range(hp):
                vst_ref[h, t] = _scaled_vt(vt, w_s[h])
        for t in range(seq // tq):
            vt = vw_ref[t * tq:(t + 1) * tq, :].astype(F32).T
            for h in range(hp):
                vwt_ref[h, t] = _scaled_vt(vt, w_w[h])

    q = q_ref[...]
    q4t = jnp.concatenate(
        [q[:, h * NSA_DIM:(h + 1) * NSA_DIM].astype(F32).T for h in range(hp)], axis=1).astype(BF16)
    qpos_all = q0 + (lax.broadcasted_iota(jnp.int32, (1, n), 1) & (tq - 1))
    qpos = q0 + lax.broadcasted_iota(jnp.int32, (1, tq), 1)

    def tile_heads(a):
        return jnp.concatenate([a] * hp, axis=1)

    n_cmp = kc_ref.shape[0]
    cmp_end = lax.broadcasted_iota(jnp.int32, (n_cmp, 1), 0) * CMP_STRIDE + (CMP_LEN - 1)
    mask_c = cmp_end <= qpos_all
    z_c = jnp.where(mask_c, _dot(kc_ref[...], q4t), NEG_INF)
    m_c = jnp.max(z_c, axis=0, keepdims=True)
    e_c = jnp.where(mask_c, jnp.exp2(z_c - m_c), 0.0)
    p_c = e_c * (1.0 / jnp.maximum(jnp.sum(e_c, axis=0, keepdims=True), 1e-30))
    o_c = _dot(vct_ref[...], p_c.astype(BF16))

    p_sum = p_c[:, 0:tq]
    for h in range(1, hp):
        p_sum = p_sum + p_c[:, h * tq:(h + 1) * tq]
    ovt = ovt_ref[...]
    p_hi = p_sum.astype(BF16)
    r1 = p_sum - p_hi.astype(F32)
    p_mid = r1.astype(BF16)
    p_lo = (r1 - p_mid.astype(F32)).astype(BF16)
    imp = _dot(ovt, p_hi) + _dot(ovt, p_mid) + _dot(ovt, p_lo)
    jblk = lax.broadcasted_iota(jnp.int32, (n_slc, tq), 0)
    cur = qpos // SLC_LEN
    forced = (jblk == 0) | (jblk == cur) | (jblk == cur - 1)
    imp = jnp.where(forced, SEL_BIG, jnp.where(jblk > cur, -SEL_BIG, imp))
    jblk_f = jblk.astype(F32)
    sel = jnp.zeros((n_slc, tq), F32)
    for _ in range(min(SLC_TOPK, n_slc)):
        mx = jnp.max(imp, axis=0, keepdims=True)
        first = jnp.min(jnp.where(imp == mx, jblk_f, float(n_slc)), axis=0, keepdims=True)
        hit = jblk_f == first
        sel = jnp.where(hit, 1.0, sel)
        imp = jnp.where(hit, -jnp.inf, imp)
    sel_bf = sel.astype(BF16)

    m_ref[...] = jnp.full_like(m_ref, NEG_INF)
    acc_ref[...] = jnp.zeros_like(acc_ref)
    exp_blk = lax.broadcasted_iota(jnp.int32, (tk, n_slc), 0) // SLC_LEN
    exp_col = lax.broadcasted_iota(jnp.int32, (tk, n_slc), 1)
    krow = lax.broadcasted_iota(jnp.int32, (tk, 1), 0)
    step_row = slope_row * float(SUB)
    assert hp == 4
    half = 2 * tq

    def step(t, carry):
        ks = pl.multiple_of(t * tk, tk)
        expand = jnp.where(exp_col == exp_blk + t * (tk // SLC_LEN), 1.0, 0.0).astype(BF16)
        chosen = _dot(expand, sel_bf)
        pen = jnp.where((chosen > 0.5) & ((krow + ks) <= qpos), 0.0, NEG_INF)
        k = ks_ref[pl.ds(ks, tk), :]
        off0 = slope_row * (ks - q0).astype(F32)
        pen2 = jnp.concatenate([pen, pen], axis=1)
        zs = [_dot(k, q4t[:, c:c + half]) + pen2 for c in (0, half)]
        m_old = [m_ref[:, c:c + half] for c in (0, half)]
        mp = [_block_weights(zs[i], off0[:, c:c + half], step_row[:, c:c + half], m_old[i])
              for i, c in enumerate((0, half))]
        for i, c in enumerate((0, half)):
            m_new, p = mp[i]
            pv = jnp.concatenate(
                [_dot(vst_ref[2 * i + j, t], p[:, j * tq:(j + 1) * tq]) for j in range(2)], axis=1)
            acc_ref[:, c:c + half] = jnp.exp2(m_old[i] - m_new) * acc_ref[:, c:c + half] + pv
            m_ref[:, c:c + half] = m_new
        return carry

    lax.fori_loop(0, q0 // tk + 1, step, 0)
    o_s = _split_acc(acc_ref[...], dv)

    wt = jnp.maximum(qi - WIN_LEN // tq, 0)
    w0 = pl.multiple_of(wt * tq, tq)
    dist = qpos - (w0 + lax.broadcasted_iota(jnp.int32, (span, 1), 0))
    pen_w = jnp.where((dist >= 0) & (dist < WIN_LEN), 0.0, NEG_INF)
    z_w = _dot(kw_ref[pl.ds(w0, span), :], q4t) + tile_heads(pen_w)
    _, p_w = _block_weights(z_w, 0.0, step_row, jnp.full((1, n), NEG_INF, F32))
    heads = []
    for h in range(hp):
        acc_w = _dot(vwt_ref[h, wt], p_w[0:tq, h * tq:(h + 1) * tq])
        for j in range(1, span // tq):
            acc_w = acc_w + _dot(vwt_ref[h, wt + j], p_w[j * tq:(j + 1) * tq, h * tq:(h + 1) * tq])
        heads.append(acc_w)
    o_w = _split_acc(jnp.concatenate(heads, axis=1), dv)

    def gate_row(branch):
        return jnp.concatenate(
            [gate_ref[pl.ds((g * hp + h) * 3 + branch, 1), :] for h in range(hp)], axis=1)

    ot = gate_row(0) * o_c + gate_row(1) * o_s + gate_row(2) * o_w
    o_ref[...] = jnp.concatenate(
        [ot[:, h * tq:(h + 1) * tq].T for h in range(hp)], axis=1).astype(BF16)


def _nsa_attention(proj, gates_t, k_cmp, vt_cmp, overlap_t, batch, seq):
    t = proj.shape[0]
    tq = _pick(seq, 128)
    tk = _pick(seq, 512)
    assert seq >= WIN_LEN + tq and WIN_LEN % tq == 0 and tk % tq == 0 and tq == SUB
    nq = seq // tq
    hp = NSA_HPG
    qb = C_NQ // (hp * NSA_DIM)
    kvb = C_NKV // LANES
    n_cmp = k_cmp.shape[3]
    slopes2 = _alibi_slopes(NSA_HEADS) * np.float32(LOG2E)
    _assert_slopes_fit(slopes2)
    slopes = jnp.asarray(slopes2)

    def kv_spec(branch, which):
        base = kvb + (branch * 2 + which) * NSA_GROUPS
        return pl.BlockSpec((seq, LANES), lambda b, g, i: (b, base + g))

    return pl.pallas_call(
        functools.partial(_nsa_kernel, tq=tq, tk=tk, seq=seq),
        grid=(batch, NSA_GROUPS, nq),
        in_specs=[
            pl.BlockSpec(memory_space=pltpu.SMEM),
            pl.BlockSpec((tq, hp * NSA_DIM), lambda b, g, i: (b * nq + i, qb + g)),
            pl.BlockSpec((LANES, tq), lambda b, g, i: (0, b * nq + i)),
            pl.BlockSpec((None, None, None, n_cmp, NSA_DIM), lambda b, g, i: (0, b, g, 0, 0)),
            pl.BlockSpec((None, None, None, NSA_DIM, n_cmp), lambda b, g, i: (1, b, g, 0, 0)),
            kv_spec(1, 0), kv_spec(1, 1),
            kv_spec(2, 0), kv_spec(2, 1),
            pl.BlockSpec(overlap_t.shape, lambda b, g, i: (0, 0)),
        ],
        out_specs=pl.BlockSpec((tq, hp * NSA_DIM), lambda b, g, i: (b * nq + i, g)),
        out_shape=jax.ShapeDtypeStruct((t, NSA_HEADS * NSA_DIM), BF16),
        scratch_shapes=[
            pltpu.VMEM((hp, seq // tk, NSA_DIM + ROWS_L, tk), BF16),
            pltpu.VMEM((hp, seq // tq, NSA_DIM + ROWS_L, tq), BF16),
            pltpu.VMEM((1, hp * tq), F32),
            pltpu.VMEM((NSA_DIM + ROWS_L, hp * tq), F32),
        ],
        compiler_params=_params("parallel", "parallel", "arbitrary"),
        name="nsa_attn",
    )(slopes, proj, gates_t, k_cmp, vt_cmp, proj, proj, proj, proj, overlap_t)


def _memkv_kernel(mem_ref, g_ref, w_ref, o_ref):
    mn = _rmsnorm(mem_ref[...], g_ref[...]).astype(BF16)
    o_ref[...] = _dot(mn, w_ref[...]).astype(BF16)


def _memkv(mem, g, w):
    r, d = mem.shape
    n = w.shape[1]
    tm = _pick(r, 256)
    return pl.pallas_call(
        _memkv_kernel,
        grid=(r // tm,),
        in_specs=[
            pl.BlockSpec((tm, d), lambda i: (i, 0)),
            pl.BlockSpec((1, d), lambda i: (0, 0)),
            pl.BlockSpec((d, n), lambda i: (0, 0)),
        ],
        out_specs=pl.BlockSpec((tm, n), lambda i: (i, 0)),
        out_shape=jax.ShapeDtypeStruct((r, n), BF16),
        compiler_params=_params("parallel"),
        name="mem_kv",
    )(mem, g, w)


def _memattn_kernel(q_ref, k_ref, v_ref, o_ref):
    scale = MEM_DIM ** -0.5
    q = q_ref[...]
    k = k_ref[...]
    v = v_ref[...]
    outs = []
    for h in range(MEM_HEADS):
        sl = slice(h * MEM_DIM, (h + 1) * MEM_DIM)
        s = _dot_nt(q[:, sl], k[:, sl]) * scale
        m = jnp.max(s, axis=-1, keepdims=True)
        e = jnp.exp(s - m)
        p = e / jnp.sum(e, axis=-1, keepdims=True)
        outs.append(_dot(p.astype(BF16), v[:, sl]))
    o_ref[...] = jnp.concatenate(outs, axis=-1).astype(BF16)


def _mem_attention(proj, memkv, batch, seq, mem_len):
    t = proj.shape[0]
    tq = _pick(seq, 512)
    nq = seq // tq
    width = MEM_HEADS * MEM_DIM
    qb = P_MQ // width
    return pl.pallas_call(
        _memattn_kernel,
        grid=(batch, nq),
        in_specs=[
            pl.BlockSpec((tq, width), lambda b, i: (b * nq + i, qb)),
            pl.BlockSpec((mem_len, width), lambda b, i: (b, 0)),
            pl.BlockSpec((mem_len, width), lambda b, i: (b, 1)),
        ],
        out_specs=pl.BlockSpec((tq, width), lambda b, i: (b * nq + i, 0)),
        out_shape=jax.ShapeDtypeStruct((t, width), BF16),
        compiler_params=_params("parallel", "parallel"),
        name="mem_attn",
    )(proj, memkv, memkv)


def _merge_kernel(x_ref, hn_ref, od_ref, on_ref, om_ref, wgd_ref, wgn_ref, wgm_ref,
                  wud_ref, wun_ref, wum_ref, wo_ref, o_ref, acc_ref):
    j = pl.program_id(1)

    @pl.when(j == 0)
    def _():
        acc_ref[...] = jnp.zeros_like(acc_ref)

    hn = hn_ref[...]
    merged = jax.nn.sigmoid(_dot(hn, wgd_ref[...])) * _dot(od_ref[...], wud_ref[...])
    merged += jax.nn.sigmoid(_dot(hn, wgn_ref[...])) * _dot(on_ref[...], wun_ref[...])
    merged += jax.nn.sigmoid(_dot(hn, wgm_ref[...])) * _dot(om_ref[...], wum_ref[...])
    acc_ref[...] += _dot(merged.astype(BF16), wo_ref[...])

    @pl.when(j == pl.num_programs(1) - 1)
    def _():
        o_ref[...] = x_ref[...] + acc_ref[...]


def _merge(x, hn, od, on, om, wg, wud, wun, wum, wo):
    t, d = x.shape
    tm = _pick(t, 512)
    tj = _pick(d, 512)
    nj = d // tj

    def gate_spec(branch):
        return pl.BlockSpec((d, tj), lambda i, j: (0, branch * nj + j))

    return pl.pallas_call(
        _merge_kernel,
        grid=(t // tm, nj),
        in_specs=[
            pl.BlockSpec((tm, d), lambda i, j: (i, 0)),
            pl.BlockSpec((tm, d), lambda i, j: (i, 0)),
            pl.BlockSpec((tm, od.shape[1]), lambda i, j: (i, 0)),
            pl.BlockSpec((tm, on.shape[1]), lambda i, j: (i, 0)),
            pl.BlockSpec((tm, om.shape[1]), lambda i, j: (i, 0)),
            gate_spec(0), gate_spec(1), gate_spec(2),
            pl.BlockSpec((wud.shape[0], tj), lambda i, j: (0, j)),
            pl.BlockSpec((wun.shape[0], tj), lambda i, j: (0, j)),
            pl.BlockSpec((wum.shape[0], tj), lambda i, j: (0, j)),
            pl.BlockSpec((tj, d), lambda i, j: (j, 0)),
        ],
        out_specs=pl.BlockSpec((tm, d), lambda i, j: (i, 0)),
        out_shape=jax.ShapeDtypeStruct((t, d), F32),
        scratch_shapes=[pltpu.VMEM((tm, d), F32)],
        compiler_params=_params("parallel", "arbitrary"),
        name="merge",
    )(x, hn, od, on, om, wg, wg, wg, wud, wun, wum, wo)


def _overlap_matrix_t(n_slc, n_cmp):
    cmp_start = np.arange(n_cmp)[None, :] * CMP_STRIDE
    slc_start = np.arange(n_slc)[:, None] * SLC_LEN
    return ((cmp_start < slc_start + SLC_LEN) & (cmp_start + CMP_LEN > slc_start)).astype(np.float32)


def kernel(x, mem, ffn1_norm, ffn1_w_gate, ffn1_w_up, ffn1_w_down, mix_norm, w_in, diff_lambda, diff_subln, nsa_cmp_pos, nsa_cmp_w1, nsa_cmp_w2, mem_norm, w_mem_kv, w_up_diff, w_up_nsa, w_up_mem, w_out, ffn2_norm, ffn2_w_gate, ffn2_w_up, ffn2_w_down, final_norm):
    batch, seq, d = x.shape
    mem_len = mem.shape[1]
    depth = w_in.shape[0]
    t = batch * seq
    assert seq % CMP_STRIDE == 0 and CMP_LEN == 2 * CMP_STRIDE
    n_rows = seq // CMP_STRIDE
    n_slc = seq // SLC_LEN
    overlap_t = jnp.asarray(_overlap_matrix_t(n_slc, n_rows), BF16)

    col_scale = np.ones((1, P_COLS), np.float32)
    col_scale[0, C_DQ:C_DK] = (DIFF_QK_DIM ** -0.5) * LOG2E
    col_scale[0, C_NQ:C_NKV] = (NSA_DIM ** -0.5) * LOG2E
    col_scale = jnp.asarray(col_scale)

    xt = x.reshape(t, d)
    memt = mem.reshape(batch * mem_len, d)
    fg = final_norm.reshape(1, d)
    bf = lambda a: a.astype(BF16)

    for l in range(depth):
        lam_init = 0.8 - 0.6 * math.exp(-0.3 * l)
        xt = _ffn(xt, ffn1_norm[l].reshape(1, d), bf(ffn1_w_gate[l]), bf(ffn1_w_up[l]),
                  bf(ffn1_w_down[l]), fg, False)

        w = w_in[l]
        w_main = bf(jnp.concatenate([w[:, :C_NG], w[:, C_MQ:C_MG]], axis=1))
        w_ng = bf(jnp.pad(w[:, C_NG:C_MQ], ((0, 0), (0, LANES - NSA_GATE_COLS))))
        proj, hn, ngt = _inproj(xt, mix_norm[l].reshape(1, d), w_main, w_ng, col_scale)

        o_diff = _diff_attention(proj, diff_lambda[l], diff_subln[l].reshape(1, DIFF_V_DIM),
                                 batch, seq, lam_init)

        ckv = proj[:, C_NKV:C_NKV + 2 * NSA_GROUPS * NSA_DIM]
        ckv = ckv.reshape(batch, n_rows, CMP_STRIDE, 2, NSA_GROUPS, NSA_DIM)
        ckv = ckv.transpose(3, 4, 0, 1, 2, 5).reshape(2, NSA_GROUPS, batch, n_rows, CMP_STRIDE * NSA_DIM)
        pos = bf(jnp.broadcast_to(nsa_cmp_pos[l].reshape(2, 1, CMP_LEN * NSA_DIM), (2, 8, CMP_LEN * NSA_DIM)))
        kv_cmp, kvt_cmp = _compress(ckv, pos, bf(nsa_cmp_w1[l]), bf(nsa_cmp_w2[l]))
        o_nsa = _nsa_attention(proj, ngt, kv_cmp, kvt_cmp, overlap_t, batch, seq)

        memkv = _memkv(memt, mem_norm[l].reshape(1, d), bf(w_mem_kv[l]))
        o_mem = _mem_attention(proj, memkv, batch, seq, mem_len)

        w_mg = bf(w[:, C_MG:])
        xt = _merge(xt, hn, o_diff, o_nsa, o_mem, w_mg, bf(w_up_diff[l]), bf(w_up_nsa[l]),
                    bf(w_up_mem[l]), bf(w_out[l]))

        xt = _ffn(xt, ffn2_norm[l].reshape(1, d), bf(ffn2_w_gate[l]), bf(ffn2_w_up[l]),
                  bf(ffn2_w_down[l]), fg, l == depth - 1)
    return xt.reshape(batch, seq, d)
```

```python
import functools
import math

import jax
import jax.numpy as jnp
import numpy as np
from jax import lax
from jax.experimental import pallas as pl
from jax.experimental.pallas import tpu as pltpu

F32 = jnp.float32
BF16 = jnp.bfloat16

NORM_EPS = 1e-6
NEG_INF = -1e30
SEL_BIG = 1e9
LOG2E = math.log2(math.e)

DIFF_HEADS = 8
DIFF_QK_DIM = 64
DIFF_V_DIM = 2 * DIFF_QK_DIM

NSA_HEADS = 8
NSA_GROUPS = 2
NSA_HPG = NSA_HEADS // NSA_GROUPS
NSA_DIM = 128
CMP_LEN = 32
CMP_STRIDE = 16
SLC_LEN = 64
SLC_TOPK = 8
WIN_LEN = 512

MEM_HEADS = 4
MEM_DIM = 128
N_BRANCH = 3

LANES = 128
VMEM_LIMIT = 56 * 1024 * 1024

DIFF_QK_COLS = DIFF_HEADS * 2 * DIFF_QK_DIM
DIFF_V_COLS = DIFF_HEADS * DIFF_V_DIM
NSA_Q_COLS = NSA_HEADS * NSA_DIM
NSA_KV_COLS = 3 * 2 * NSA_GROUPS * NSA_DIM
NSA_GATE_COLS = NSA_HEADS * 3
MEM_Q_COLS = MEM_HEADS * MEM_DIM
C_DQ = 0
C_DK = C_DQ + DIFF_QK_COLS
C_DV = C_DK + DIFF_QK_COLS
C_NQ = C_DV + DIFF_V_COLS
C_NKV = C_NQ + NSA_Q_COLS
C_NG = C_NKV + NSA_KV_COLS
C_MQ = C_NG + NSA_GATE_COLS
C_MG = C_MQ + MEM_Q_COLS
P_MQ = C_NG
P_COLS = P_MQ + MEM_Q_COLS


def _alibi_slopes(n_heads):
    return np.array([2.0 ** (-8.0 * (h + 1) / n_heads) for h in range(n_heads)], dtype=np.float32)


def _rmsnorm(x, g):
    y = x * lax.rsqrt(jnp.mean(x * x, axis=-1, keepdims=True) + NORM_EPS)
    return y * g


def _dot(a, b):
    return jnp.dot(a, b, preferred_element_type=F32)


def _dot_nt(a, b):
    return lax.dot_general(a, b, (((1,), (1,)), ((), ())), preferred_element_type=F32)


def _transpose_bf16(x):
    return x.astype(F32).T.astype(BF16)


def _params(*sem):
    return pltpu.CompilerParams(dimension_semantics=sem, vmem_limit_bytes=VMEM_LIMIT)


def _pick(n, pref):
    t = min(n, pref)
    assert n % t == 0, (n, t)
    return t


def _ffn_kernel(x_ref, g_ref, wg_ref, wu_ref, wd_ref, fg_ref, o_ref, xn_ref, *, final_norm):
    j = pl.program_id(1)

    @pl.when(j == 0)
    def _():
        xn_ref[...] = _rmsnorm(x_ref[...], g_ref[...]).astype(BF16)
        o_ref[...] = jnp.zeros_like(o_ref)

    xn = xn_ref[...]
    a = _dot(xn, wg_ref[...].astype(BF16))
    u = _dot(xn, wu_ref[...].astype(BF16))
    h = (a * jax.nn.sigmoid(a)) * u
    o_ref[...] += _dot(h.astype(BF16), wd_ref[...].astype(BF16))

    @pl.when(j == pl.num_programs(1) - 1)
    def _():
        y = x_ref[...] + 0.5 * o_ref[...]
        if final_norm:
            y = _rmsnorm(y, fg_ref[...])
        o_ref[...] = y


def _ffn(x, g, wg, wu, wd, fg, layer, final_norm):
    t, d = x.shape
    f = wg.shape[2]
    tm = _pick(t, 1024)
    tf = _pick(f, 256)
    return pl.pallas_call(
        functools.partial(_ffn_kernel, final_norm=final_norm),
        grid=(t // tm, f // tf),
        in_specs=[
            pl.BlockSpec((tm, d), lambda i, j: (i, 0), pipeline_mode=pl.Buffered(1)),
            pl.BlockSpec((1, d), lambda i, j: (0, 0)),
            pl.BlockSpec((None, d, tf), lambda i, j: (layer, 0, j)),
            pl.BlockSpec((None, d, tf), lambda i, j: (layer, 0, j)),
            pl.BlockSpec((None, tf, d), lambda i, j: (layer, j, 0)),
            pl.BlockSpec((1, d), lambda i, j: (0, 0)),
        ],
        out_specs=pl.BlockSpec((tm, d), lambda i, j: (i, 0)),
        out_shape=jax.ShapeDtypeStruct((t, d), F32),
        scratch_shapes=[pltpu.VMEM((tm, d), BF16)],
        compiler_params=_params("parallel", "arbitrary"),
        name="ffn",
    )(x, g, wg, wu, wd, fg)


def _inproj_kernel(x_ref, g_ref, w_ref, wmq_ref, wng_ref, cs_ref, proj_ref, hn_ref, ngt_ref, xn_ref,
                   *, n_main):
    j = pl.program_id(1)

    @pl.when(j == 0)
    def _():
        xn = _rmsnorm(x_ref[...], g_ref[...]).astype(BF16)
        xn_ref[...] = xn
        hn_ref[...] = xn
        ngt_ref[...] = jax.nn.sigmoid(_dot(xn, wng_ref[...])).T

    @pl.when(j < n_main)
    def _():
        proj_ref[...] = (_dot(xn_ref[...], w_ref[...].astype(BF16)) * cs_ref[...]).astype(BF16)

    @pl.when(j >= n_main)
    def _():
        proj_ref[...] = (_dot(xn_ref[...], wmq_ref[...]) * cs_ref[...]).astype(BF16)


def _inproj(x, g, w_in, layer, wmq, wng, cs):
    t, d = x.shape
    n = P_COLS
    tm = _pick(t, 512)
    tn = MEM_Q_COLS
    assert P_MQ % tn == 0 and wmq.shape == (d, tn)
    n_main = P_MQ // tn
    return pl.pallas_call(
        functools.partial(_inproj_kernel, n_main=n_main),
        grid=(t // tm, n // tn),
        in_specs=[
            pl.BlockSpec((tm, d), lambda i, j: (i, 0)),
            pl.BlockSpec((1, d), lambda i, j: (0, 0)),
            pl.BlockSpec((None, d, tn), lambda i, j: (layer, 0, jnp.minimum(j, n_main - 1))),
            pl.BlockSpec((d, tn), lambda i, j: (0, 0)),
            pl.BlockSpec((d, LANES), lambda i, j: (0, 0)),
            pl.BlockSpec((1, tn), lambda i, j: (0, j)),
        ],
        out_specs=[
            pl.BlockSpec((tm, tn), lambda i, j: (i, j)),
            pl.BlockSpec((tm, d), lambda i, j: (i, 0)),
            pl.BlockSpec((LANES, tm), lambda i, j: (0, i)),
        ],
        out_shape=[
            jax.ShapeDtypeStruct((t, n), BF16),
            jax.ShapeDtypeStruct((t, d), BF16),
            jax.ShapeDtypeStruct((LANES, t), F32),
        ],
        scratch_shapes=[pltpu.VMEM((tm, d), BF16)],
        compiler_params=_params("parallel", "arbitrary"),
        name="inproj",
    )(x, g, w_in, wmq, wng, cs)


SUB = 128
ROWS_L = 16


def _assert_slopes_fit(slopes2):
    assert float(np.max(slopes2)) * SUB < 100.0, "ALiBi factor per bias block leaves the safe f32 range"


def _bias_factor_row(slope2, width):
    r = lax.broadcasted_iota(jnp.int32, (1, width), 1) & (SUB - 1)
    return jnp.exp2(slope2 * r.astype(F32))


def _scaled_vt(vt, w_row):
    ones = jnp.broadcast_to(w_row, (ROWS_L, w_row.shape[1]))
    return jnp.concatenate([vt * w_row, ones], axis=0).astype(BF16)


def _block_weights(z, off0, step, m_prev):
    nb = z.shape[0] // SUB
    blocks = [z[j * SUB:(j + 1) * SUB] for j in range(nb)]
    offs = [off0 + j * step for j in range(nb)]
    m_new = m_prev
    for zb, off in zip(blocks, offs):
        m_new = jnp.maximum(m_new, jnp.max(zb, axis=0, keepdims=True) + off)
    p = [jnp.exp2(zb - (m_new - off)).astype(BF16) for zb, off in zip(blocks, offs)]
    return m_new, jnp.concatenate(p, axis=0)


def _split_acc(acc, dv):
    return acc[0:dv] * (1.0 / jnp.maximum(acc[dv:dv + 1], 1e-30))


def _diff_kernel(slopes_ref, q_ref, k_ref, v_ref, lp_ref, sg_ref, o_ref,
                 vt_ref, m_ref, acc_ref, *, tile, n_tiles, lam_init):
    h = pl.program_id(1)
    qi = pl.program_id(2)
    slope2 = slopes_ref[h]

    @pl.when(qi == 0)
    def _():
        w_row = _bias_factor_row(slope2, tile)
        for t in range(n_tiles):
            vt_ref[t] = _scaled_vt(v_ref[t * tile:(t + 1) * tile, :].astype(F32).T, w_row)

    qt = _transpose_bf16(q_ref[...])
    feat = lax.broadcasted_iota(jnp.int32, qt.shape, 0)
    qts = [jnp.where((feat >= m * DIFF_QK_DIM) & (feat < (m + 1) * DIFF_QK_DIM), qt, 0.0).astype(BF16)
           for m in range(2)]

    m_ref[...] = jnp.full_like(m_ref, NEG_INF)
    acc_ref[...] = jnp.zeros_like(acc_ref)
    step = slope2 * float(SUB)

    def tile_step(t, causal):
        ks = pl.multiple_of(t * tile, tile)
        k = k_ref[pl.ds(ks, tile), :]
        vt = vt_ref[t]
        off0 = slope2 * ((t - qi) * tile).astype(F32)
        zs = [_dot(k, qts[m]) for m in range(2)]
        if causal:
            row = lax.broadcasted_iota(jnp.int32, zs[0].shape, 0)
            col = lax.broadcasted_iota(jnp.int32, zs[0].shape, 1)
            zs = [jnp.where(row <= col, z, NEG_INF) for z in zs]
        m_old = [m_ref[m] for m in range(2)]
        mp = [_block_weights(zs[m], off0, step, m_old[m]) for m in range(2)]
        for m in range(2):
            m_new, p = mp[m]
            acc_ref[m] = jnp.exp2(m_old[m] - m_new) * acc_ref[m] + _dot(vt, p)
            m_ref[m] = m_new

    def body(t, carry):
        tile_step(t, False)
        return carry

    lax.fori_loop(0, qi, body, 0)
    tile_step(qi, True)

    o1 = _split_acc(acc_ref[0], DIFF_V_DIM)
    o2 = _split_acc(acc_ref[1], DIFF_V_DIM)
    lp = lp_ref[...]
    lam = (jnp.exp(jnp.sum(lp[0:1] * lp[1:2], axis=-1, keepdims=True))
           - jnp.exp(jnp.sum(lp[2:3] * lp[3:4], axis=-1, keepdims=True)) + lam_init)
    ot = o1 - lam * o2
    ot = ot * lax.rsqrt(jnp.mean(ot * ot, axis=0, keepdims=True) + NORM_EPS)
    o_ref[...] = (ot.T * sg_ref[...] * (1.0 - lam_init)).astype(BF16)


def _diff_attention(proj, lp, sg, batch, seq, lam_init):
    t = proj.shape[0]
    tile = _pick(seq, 512)
    nq = seq // tile
    kb, vb = C_DK // LANES, C_DV // LANES
    slopes2 = _alibi_slopes(DIFF_HEADS) * np.float32(LOG2E)
    _assert_slopes_fit(slopes2)
    slopes = jnp.asarray(slopes2)
    assert tile % SUB == 0
    return pl.pallas_call(
        functools.partial(_diff_kernel, tile=tile, n_tiles=nq, lam_init=lam_init),
        grid=(batch, DIFF_HEADS, nq),
        in_specs=[
            pl.BlockSpec(memory_space=pltpu.SMEM),
            pl.BlockSpec((tile, LANES), lambda b, h, i: (b * nq + i, h)),
            pl.BlockSpec((seq, LANES), lambda b, h, i: (b, kb + h)),
            pl.BlockSpec((seq, LANES), lambda b, h, i: (b, vb + h)),
            pl.BlockSpec((4, DIFF_QK_DIM), lambda b, h, i: (0, 0)),
            pl.BlockSpec((1, DIFF_V_DIM), lambda b, h, i: (0, 0)),
        ],
        out_specs=pl.BlockSpec((tile, LANES), lambda b, h, i: (b * nq + i, h)),
        out_shape=jax.ShapeDtypeStruct((t, DIFF_HEADS * DIFF_V_DIM), BF16),
        scratch_shapes=[
            pltpu.VMEM((nq, DIFF_V_DIM + ROWS_L, tile), BF16),
            pltpu.VMEM((2, 1, tile), F32),
            pltpu.VMEM((2, DIFF_V_DIM + ROWS_L, tile), F32),
        ],
        compiler_params=_params("parallel", "parallel", "arbitrary"),
        name="diff_attn",
    )(slopes, proj, proj, proj, lp, sg)


def _compress_kernel(kv_ref, pos_ref, w1_ref, w2_ref, o_ref, ot_ref):
    half = w1_ref.shape[0] // 2
    kv = kv_ref[...]
    top = _dot(kv, w1_ref[:half, :])
    bot = _dot(kv, w1_ref[half:, :])
    rows = kv.shape[0]
    bot_next = pltpu.roll(bot, rows - 1, 0)
    pos_term = _dot(pos_ref[...], w1_ref[...])[0:1]
    hid = jax.nn.gelu(top + bot_next + pos_term)
    out = _dot(hid.astype(BF16), w2_ref[...])
    o_ref[...] = out.astype(BF16)
    ot_ref[...] = out.T.astype(BF16)


def _compress(kvr, pos, w1, w2):
    _, groups, batch, rows, width = kvr.shape
    d = w2.shape[-1]
    return pl.pallas_call(
        _compress_kernel,
        grid=(2, groups, batch),
        in_specs=[
            pl.BlockSpec((None, None, None, rows, width), lambda a, g, b: (a, g, b, 0, 0)),
            pl.BlockSpec((None, 8, 2 * width), lambda a, g, b: (a, 0, 0)),
            pl.BlockSpec((None, 2 * width, d), lambda a, g, b: (a, 0, 0)),
            pl.BlockSpec((None, d, d), lambda a, g, b: (a, 0, 0)),
        ],
        out_specs=[
            pl.BlockSpec((None, None, None, rows, d), lambda a, g, b: (a, b, g, 0, 0)),
            pl.BlockSpec((None, None, None, d, rows), lambda a, g, b: (a, b, g, 0, 0)),
        ],
        out_shape=[
            jax.ShapeDtypeStruct((2, batch, groups, rows, d), BF16),
            jax.ShapeDtypeStruct((2, batch, groups, d, rows), BF16),
        ],
        compiler_params=_params("parallel", "parallel", "parallel"),
        name="compress",
    )(kvr, pos, w1, w2)


def _nsa_kernel(slopes_ref, q_ref, gate_ref, kc_ref, vct_ref, ks_ref, vs_ref, kw_ref, vw_ref,
                ovt_ref, o_ref, vst_ref, vwt_ref, m_ref, acc_ref, *, tq, tk, seq):
    g = pl.program_id(1)
    qi = pl.program_id(2)
    q0 = qi * tq
    hp = NSA_HPG
    n = hp * tq
    dv = NSA_DIM
    span = WIN_LEN + tq
    n_slc = seq // SLC_LEN
    slope_row = jnp.concatenate(
        [jnp.full((1, tq), slopes_ref[g * hp + h], F32) for h in range(hp)], axis=1)

    @pl.when(qi == 0)
    def _():
        w_s = [_bias_factor_row(slopes_ref[g * hp + h], tk) for h in range(hp)]
        w_w = [_bias_factor_row(slopes_ref[g * hp + h], tq) for h in range(hp)]
        for t in range(seq // tk):
            vt = vs_ref[t * tk:(t + 1) * tk, :].astype(F32).T
            for h in range(hp):
                vst_ref[h, t] = _scaled_vt(vt, w_s[h])
        for t in range(seq // tq):
            vt = vw_ref[t * tq:(t + 1) * tq, :].astype(F32).T
            for h in range(hp):
                vwt_ref[h, t] = _scaled_vt(vt, w_w[h])

    q = q_ref[...]
    q4t = jnp.concatenate(
        [q[:, h * NSA_DIM:(h + 1) * NSA_DIM].astype(F32).T for h in range(hp)], axis=1).astype(BF16)
    qpos_all = q0 + (lax.broadcasted_iota(jnp.int32, (1, n), 1) & (tq - 1))
    qpos = q0 + lax.broadcasted_iota(jnp.int32, (1, tq), 1)

    def tile_heads(a):
        return jnp.concatenate([a] * hp, axis=1)

    n_cmp = kc_ref.shape[0]
    cmp_end = lax.broadcasted_iota(jnp.int32, (n_cmp, 1), 0) * CMP_STRIDE + (CMP_LEN - 1)
    mask_c = cmp_end <= qpos_all
    z_c = jnp.where(mask_c, _dot(kc_ref[...], q4t), NEG_INF)
    m_c = jnp.max(z_c, axis=0, keepdims=True)
    e_c = jnp.where(mask_c, jnp.exp2(z_c - m_c), 0.0)
    p_c = e_c * (1.0 / jnp.maximum(jnp.sum(e_c, axis=0, keepdims=True), 1e-30))
    o_c = _dot(vct_ref[...], p_c.astype(BF16))

    p_sum = p_c[:, 0:tq]
    for h in range(1, hp):
        p_sum = p_sum + p_c[:, h * tq:(h + 1) * tq]
    ovt = ovt_ref[...]
    p_hi = p_sum.astype(BF16)
    r1 = p_sum - p_hi.astype(F32)
    p_mid = r1.astype(BF16)
    p_lo = (r1 - p_mid.astype(F32)).astype(BF16)
    imp = _dot(ovt, p_hi) + _dot(ovt, p_mid) + _dot(ovt, p_lo)
    jblk = lax.broadcasted_iota(jnp.int32, (n_slc, tq), 0)
    cur = qpos // SLC_LEN
    forced = (jblk == 0) | (jblk == cur) | (jblk == cur - 1)
    imp = jnp.where(forced, SEL_BIG, jnp.where(jblk > cur, -SEL_BIG, imp))
    jblk_f = jblk.astype(F32)
    sel = jnp.zeros((n_slc, tq), F32)
    for _ in range(min(SLC_TOPK, n_slc)):
        mx = jnp.max(imp, axis=0, keepdims=True)
        first = jnp.min(jnp.where(imp == mx, jblk_f, float(n_slc)), axis=0, keepdims=True)
        hit = jblk_f == first
        sel = jnp.where(hit, 1.0, sel)
        imp = jnp.where(hit, -jnp.inf, imp)
    sel_bf = sel.astype(BF16)

    m_ref[...] = jnp.full_like(m_ref, NEG_INF)
    acc_ref[...] = jnp.zeros_like(acc_ref)
    exp_blk = lax.broadcasted_iota(jnp.int32, (tk, n_slc), 0) // SLC_LEN
    exp_col = lax.broadcasted_iota(jnp.int32, (tk, n_slc), 1)
    krow = lax.broadcasted_iota(jnp.int32, (tk, 1), 0)
    step_row = slope_row * float(SUB)
    assert hp == 4
    half = 2 * tq

    def step(t, carry):
        ks = pl.multiple_of(t * tk, tk)
        expand = jnp.where(exp_col == exp_blk + t * (tk // SLC_LEN), 1.0, 0.0).astype(BF16)
        chosen = _dot(expand, sel_bf)
        pen = jnp.where((chosen > 0.5) & ((krow + ks) <= qpos), 0.0, NEG_INF)
        k = ks_ref[pl.ds(ks, tk), :]
        off0 = slope_row * (ks - q0).astype(F32)
        pen2 = jnp.concatenate([pen, pen], axis=1)
        zs = [_dot(k, q4t[:, c:c + half]) + pen2 for c in (0, half)]
        m_old = [m_ref[:, c:c + half] for c in (0, half)]
        mp = [_block_weights(zs[i], off0[:, c:c + half], step_row[:, c:c + half], m_old[i])
              for i, c in enumerate((0, half))]
        for i, c in enumerate((0, half)):
            m_new, p = mp[i]
            pv = jnp.concatenate(
                [_dot(vst_ref[2 * i + j, t], p[:, j * tq:(j + 1) * tq]) for j in range(2)], axis=1)
            acc_ref[:, c:c + half] = jnp.exp2(m_old[i] - m_new) * acc_ref[:, c:c + half] + pv
            m_ref[:, c:c + half] = m_new
        return carry

    lax.fori_loop(0, q0 // tk + 1, step, 0)
    o_s = _split_acc(acc_ref[...], dv)

    wt = jnp.maximum(qi - WIN_LEN // tq, 0)
    w0 = pl.multiple_of(wt * tq, tq)
    dist = qpos - (w0 + lax.broadcasted_iota(jnp.int32, (span, 1), 0))
    pen_w = jnp.where((dist >= 0) & (dist < WIN_LEN), 0.0, NEG_INF)
    z_w = _dot(kw_ref[pl.ds(w0, span), :], q4t) + tile_heads(pen_w)
    _, p_w = _block_weights(z_w, 0.0, step_row, jnp.full((1, n), NEG_INF, F32))
    heads = []
    for h in range(hp):
        acc_w = _dot(vwt_ref[h, wt], p_w[0:tq, h * tq:(h + 1) * tq])
        for j in range(1, span // tq):
            acc_w = acc_w + _dot(vwt_ref[h, wt + j], p_w[j * tq:(j + 1) * tq, h * tq:(h + 1) * tq])
        heads.append(acc_w)
    o_w = _split_acc(jnp.concatenate(heads, axis=1), dv)

    def gate_row(branch):
        return jnp.concatenate(
            [gate_ref[pl.ds((g * hp + h) * 3 + branch, 1), :] for h in range(hp)], axis=1)

    ot = gate_row(0) * o_c + gate_row(1) * o_s + gate_row(2) * o_w
    o_ref[...] = jnp.concatenate(
        [ot[:, h * tq:(h + 1) * tq].T for h in range(hp)], axis=1).astype(BF16)


def _nsa_attention(proj, gates_t, k_cmp, vt_cmp, overlap_t, batch, seq):
    t = proj.shape[0]
    tq = _pick(seq, 128)
    tk = _pick(seq, 512)
    assert seq >= WIN_LEN + tq and WIN_LEN % tq == 0 and tk % tq == 0 and tq == SUB
    nq = seq // tq
    hp = NSA_HPG
    qb = C_NQ // (hp * NSA_DIM)
    kvb = C_NKV // LANES
    n_cmp = k_cmp.shape[3]
    slopes2 = _alibi_slopes(NSA_HEADS) * np.float32(LOG2E)
    _assert_slopes_fit(slopes2)
    slopes = jnp.asarray(slopes2)

    def kv_spec(branch, which):
        base = kvb + (branch * 2 + which) * NSA_GROUPS
        return pl.BlockSpec((seq, LANES), lambda b, g, i: (b, base + g))

    return pl.pallas_call(
        functools.partial(_nsa_kernel, tq=tq, tk=tk, seq=seq),
        grid=(batch, NSA_GROUPS, nq),
        in_specs=[
            pl.BlockSpec(memory_space=pltpu.SMEM),
            pl.BlockSpec((tq, hp * NSA_DIM), lambda b, g, i: (b * nq + i, qb + g)),
            pl.BlockSpec((LANES, tq), lambda b, g, i: (0, b * nq + i)),
            pl.BlockSpec((None, None, None, n_cmp, NSA_DIM), lambda b, g, i: (0, b, g, 0, 0)),
            pl.BlockSpec((None, None, None, NSA_DIM, n_cmp), lambda b, g, i: (1, b, g, 0, 0)),
            kv_spec(1, 0), kv_spec(1, 1),
            kv_spec(2, 0), kv_spec(2, 1),
            pl.BlockSpec(overlap_t.shape, lambda b, g, i: (0, 0)),
        ],
        out_specs=pl.BlockSpec((tq, hp * NSA_DIM), lambda b, g, i: (b * nq + i, g)),
        out_shape=jax.ShapeDtypeStruct((t, NSA_HEADS * NSA_DIM), BF16),
        scratch_shapes=[
            pltpu.VMEM((hp, seq // tk, NSA_DIM + ROWS_L, tk), BF16),
            pltpu.VMEM((hp, seq // tq, NSA_DIM + ROWS_L, tq), BF16),
            pltpu.VMEM((1, hp * tq), F32),
            pltpu.VMEM((NSA_DIM + ROWS_L, hp * tq), F32),
        ],
        compiler_params=_params("parallel", "parallel", "arbitrary"),
        name="nsa_attn",
    )(slopes, proj, gates_t, k_cmp, vt_cmp, proj, proj, proj, proj, overlap_t)


def _memkv_kernel(mem_ref, g_ref, w_ref, o_ref):
    mn = _rmsnorm(mem_ref[...], g_ref[...]).astype(BF16)
    o_ref[...] = _dot(mn, w_ref[...]).astype(BF16)


def _memkv(mem, g, w):
    r, d = mem.shape
    n = w.shape[1]
    tm = _pick(r, 256)
    return pl.pallas_call(
        _memkv_kernel,
        grid=(r // tm,),
        in_specs=[
            pl.BlockSpec((tm, d), lambda i: (i, 0)),
            pl.BlockSpec((1, d), lambda i: (0, 0)),
            pl.BlockSpec((d, n), lambda i: (0, 0)),
        ],
        out_specs=pl.BlockSpec((tm, n), lambda i: (i, 0)),
        out_shape=jax.ShapeDtypeStruct((r, n), BF16),
        compiler_params=_params("parallel"),
        name="mem_kv",
    )(mem, g, w)


def _memattn_kernel(q_ref, k_ref, v_ref, o_ref):
    scale = MEM_DIM ** -0.5
    q = q_ref[...]
    k = k_ref[...]
    v = v_ref[...]
    outs = []
    for h in range(MEM_HEADS):
        sl = slice(h * MEM_DIM, (h + 1) * MEM_DIM)
        s = _dot_nt(q[:, sl], k[:, sl]) * scale
        m = jnp.max(s, axis=-1, keepdims=True)
        e = jnp.exp(s - m)
        p = e / jnp.sum(e, axis=-1, keepdims=True)
        outs.append(_dot(p.astype(BF16), v[:, sl]))
    o_ref[...] = jnp.concatenate(outs, axis=-1).astype(BF16)


def _mem_attention(proj, memkv, batch, seq, mem_len):
    t = proj.shape[0]
    tq = _pick(seq, 512)
    nq = seq // tq
    width = MEM_HEADS * MEM_DIM
    qb = P_MQ // width
    return pl.pallas_call(
        _memattn_kernel,
        grid=(batch, nq),
        in_specs=[
            pl.BlockSpec((tq, width), lambda b, i: (b * nq + i, qb)),
            pl.BlockSpec((mem_len, width), lambda b, i: (b, 0)),
            pl.BlockSpec((mem_len, width), lambda b, i: (b, 1)),
        ],
        out_specs=pl.BlockSpec((tq, width), lambda b, i: (b * nq + i, 0)),
        out_shape=jax.ShapeDtypeStruct((t, width), BF16),
        compiler_params=_params("parallel", "parallel"),
        name="mem_attn",
    )(proj, memkv, memkv)


def _merge_kernel(x_ref, hn_ref, od_ref, on_ref, om_ref, wgd_ref, wgn_ref, wgm_ref,
                  wud_ref, wun_ref, wum_ref, wo_ref, o_ref, acc_ref):
    j = pl.program_id(1)

    @pl.when(j == 0)
    def _():
        acc_ref[...] = jnp.zeros_like(acc_ref)

    hn = hn_ref[...]
    merged = jax.nn.sigmoid(_dot(hn, wgd_ref[...])) * _dot(od_ref[...], wud_ref[...])
    merged += jax.nn.sigmoid(_dot(hn, wgn_ref[...])) * _dot(on_ref[...], wun_ref[...])
    merged += jax.nn.sigmoid(_dot(hn, wgm_ref[...])) * _dot(om_ref[...], wum_ref[...])
    acc_ref[...] += _dot(merged.astype(BF16), wo_ref[...])

    @pl.when(j == pl.num_programs(1) - 1)
    def _():
        o_ref[...] = x_ref[...] + acc_ref[...]


def _merge(x, hn, od, on, om, wg, wud, wun, wum, wo):
    t, d = x.shape
    tm = _pick(t, 512)
    tj = _pick(d, 512)
    nj = d // tj

    def gate_spec(branch):
        return pl.BlockSpec((d, tj), lambda i, j: (0, branch * nj + j))

    return pl.pallas_call(
        _merge_kernel,
        grid=(t // tm, nj),
        in_specs=[
            pl.BlockSpec((tm, d), lambda i, j: (i, 0)),
            pl.BlockSpec((tm, d), lambda i, j: (i, 0)),
            pl.BlockSpec((tm, od.shape[1]), lambda i, j: (i, 0)),
            pl.BlockSpec((tm, on.shape[1]), lambda i, j: (i, 0)),
            pl.BlockSpec((tm, om.shape[1]), lambda i, j: (i, 0)),
            gate_spec(0), gate_spec(1), gate_spec(2),
            pl.BlockSpec((wud.shape[0], tj), lambda i, j: (0, j)),
            pl.BlockSpec((wun.shape[0], tj), lambda i, j: (0, j)),
            pl.BlockSpec((wum.shape[0], tj), lambda i, j: (0, j)),
            pl.BlockSpec((tj, d), lambda i, j: (j, 0)),
        ],
        out_specs=pl.BlockSpec((tm, d), lambda i, j: (i, 0)),
        out_shape=jax.ShapeDtypeStruct((t, d), F32),
        scratch_shapes=[pltpu.VMEM((tm, d), F32)],
        compiler_params=_params("parallel", "arbitrary"),
        name="merge",
    )(x, hn, od, on, om, wg, wg, wg, wud, wun, wum, wo)


def _overlap_matrix_t(n_slc, n_cmp):
    cmp_start = np.arange(n_cmp)[None, :] * CMP_STRIDE
    slc_start = np.arange(n_slc)[:, None] * SLC_LEN
    return ((cmp_start < slc_start + SLC_LEN) & (cmp_start + CMP_LEN > slc_start)).astype(np.float32)


def kernel(x, mem, ffn1_norm, ffn1_w_gate, ffn1_w_up, ffn1_w_down, mix_norm, w_in, diff_lambda, diff_subln, nsa_cmp_pos, nsa_cmp_w1, nsa_cmp_w2, mem_norm, w_mem_kv, w_up_diff, w_up_nsa, w_up_mem, w_out, ffn2_norm, ffn2_w_gate, ffn2_w_up, ffn2_w_down, final_norm):
    batch, seq, d = x.shape
    mem_len = mem.shape[1]
    depth = w_in.shape[0]
    t = batch * seq
    assert seq % CMP_STRIDE == 0 and CMP_LEN == 2 * CMP_STRIDE
    n_rows = seq // CMP_STRIDE
    n_slc = seq // SLC_LEN
    overlap_t = jnp.asarray(_overlap_matrix_t(n_slc, n_rows), BF16)

    col_scale = np.ones((1, P_COLS), np.float32)
    col_scale[0, C_DQ:C_DK] = (DIFF_QK_DIM ** -0.5) * LOG2E
    col_scale[0, C_NQ:C_NKV] = (NSA_DIM ** -0.5) * LOG2E
    col_scale = jnp.asarray(col_scale)

    xt = x.reshape(t, d)
    memt = mem.reshape(batch * mem_len, d)
    fg = final_norm.reshape(1, d)
    bf = lambda a: a.astype(BF16)

    for l in range(depth):
        lam_init = 0.8 - 0.6 * math.exp(-0.3 * l)
        xt = _ffn(xt, ffn1_norm[l].reshape(1, d), ffn1_w_gate, ffn1_w_up, ffn1_w_down, fg, l, False)

        w = w_in[l]
        w_mq = bf(w[:, C_MQ:C_MG])
        w_ng = bf(jnp.pad(w[:, C_NG:C_MQ], ((0, 0), (0, LANES - NSA_GATE_COLS))))
        proj, hn, ngt = _inproj(xt, mix_norm[l].reshape(1, d), w_in, l, w_mq, w_ng, col_scale)

        o_diff = _diff_attention(proj, diff_lambda[l], diff_subln[l].reshape(1, DIFF_V_DIM),
                                 batch, seq, lam_init)

        ckv = proj[:, C_NKV:C_NKV + 2 * NSA_GROUPS * NSA_DIM]
        ckv = ckv.reshape(batch, n_rows, CMP_STRIDE, 2, NSA_GROUPS, NSA_DIM)
        ckv = ckv.transpose(3, 4, 0, 1, 2, 5).reshape(2, NSA_GROUPS, batch, n_rows, CMP_STRIDE * NSA_DIM)
        pos = bf(jnp.broadcast_to(nsa_cmp_pos[l].reshape(2, 1, CMP_LEN * NSA_DIM), (2, 8, CMP_LEN * NSA_DIM)))
        kv_cmp, kvt_cmp = _compress(ckv, pos, bf(nsa_cmp_w1[l]), bf(nsa_cmp_w2[l]))
        o_nsa = _nsa_attention(proj, ngt, kv_cmp, kvt_cmp, overlap_t, batch, seq)

        memkv = _memkv(memt, mem_norm[l].reshape(1, d), bf(w_mem_kv[l]))
        o_mem = _mem_attention(proj, memkv, batch, seq, mem_len)

        w_mg = bf(w[:, C_MG:])
        xt = _merge(xt, hn, o_diff, o_nsa, o_mem, w_mg, bf(w_up_diff[l]), bf(w_up_nsa[l]),
                    bf(w_up_mem[l]), bf(w_out[l]))

        xt = _ffn(xt, ffn2_norm[l].reshape(1, d), ffn2_w_gate, ffn2_w_up, ffn2_w_down, fg, l,
                  l == depth - 1)
    return xt.reshape(batch, seq, d)
```

```python
import functools
import math

import jax
import jax.numpy as jnp
import numpy as np
from jax import lax
from jax.experimental import pallas as pl
from jax.experimental.pallas import tpu as pltpu

F32 = jnp.float32
BF16 = jnp.bfloat16

NORM_EPS = 1e-6
NEG_INF = -1e30
SEL_BIG = 1e9
LOG2E = math.log2(math.e)

DIFF_HEADS = 8
DIFF_QK_DIM = 64
DIFF_V_DIM = 2 * DIFF_QK_DIM

NSA_HEADS = 8
NSA_GROUPS = 2
NSA_HPG = NSA_HEADS // NSA_GROUPS
NSA_DIM = 128
CMP_LEN = 32
CMP_STRIDE = 16
SLC_LEN = 64
SLC_TOPK = 8
WIN_LEN = 512

MEM_HEADS = 4
MEM_DIM = 128
N_BRANCH = 3

LANES = 128
VMEM_LIMIT = 56 * 1024 * 1024

DIFF_QK_COLS = DIFF_HEADS * 2 * DIFF_QK_DIM
DIFF_V_COLS = DIFF_HEADS * DIFF_V_DIM
NSA_Q_COLS = NSA_HEADS * NSA_DIM
NSA_KV_COLS = 3 * 2 * NSA_GROUPS * NSA_DIM
NSA_GATE_COLS = NSA_HEADS * 3
MEM_Q_COLS = MEM_HEADS * MEM_DIM
C_DQ = 0
C_DK = C_DQ + DIFF_QK_COLS
C_DV = C_DK + DIFF_QK_COLS
C_NQ = C_DV + DIFF_V_COLS
C_NKV = C_NQ + NSA_Q_COLS
C_NG = C_NKV + NSA_KV_COLS
C_MQ = C_NG + NSA_GATE_COLS
C_MG = C_MQ + MEM_Q_COLS
P_MQ = C_NG
P_COLS = P_MQ + MEM_Q_COLS


def _alibi_slopes(n_heads):
    return np.array([2.0 ** (-8.0 * (h + 1) / n_heads) for h in range(n_heads)], dtype=np.float32)


def _rmsnorm(x, g):
    y = x * lax.rsqrt(jnp.mean(x * x, axis=-1, keepdims=True) + NORM_EPS)
    return y * g


def _dot(a, b):
    return jnp.dot(a, b, preferred_element_type=F32)


def _dot_nt(a, b):
    return lax.dot_general(a, b, (((1,), (1,)), ((), ())), preferred_element_type=F32)


def _transpose_bf16(x):
    return x.astype(F32).T.astype(BF16)


def _params(*sem):
    return pltpu.CompilerParams(dimension_semantics=sem, vmem_limit_bytes=VMEM_LIMIT)


def _pick(n, pref):
    t = min(n, pref)
    assert n % t == 0, (n, t)
    return t


def _ffn_kernel(x_ref, g_ref, wg_ref, wu_ref, wd_ref, fg_ref, o_ref, xn_ref, *, final_norm):
    j = pl.program_id(1)

    @pl.when(j == 0)
    def _():
        xn_ref[...] = _rmsnorm(x_ref[...], g_ref[...]).astype(BF16)
        o_ref[...] = jnp.zeros_like(o_ref)

    xn = xn_ref[...]
    a = _dot(xn, wg_ref[...].astype(BF16))
    u = _dot(xn, wu_ref[...].astype(BF16))
    h = (a * jax.nn.sigmoid(a)) * u
    o_ref[...] += _dot(h.astype(BF16), wd_ref[...].astype(BF16))

    @pl.when(j == pl.num_programs(1) - 1)
    def _():
        y = x_ref[...] + 0.5 * o_ref[...]
        if final_norm:
            y = _rmsnorm(y, fg_ref[...])
        o_ref[...] = y


def _ffn(x, g, wg, wu, wd, fg, layer, final_norm):
    t, d = x.shape
    f = wg.shape[2]
    tm = _pick(t, 1024)
    tf = _pick(f, 256)
    return pl.pallas_call(
        functools.partial(_ffn_kernel, final_norm=final_norm),
        grid=(t // tm, f // tf),
        in_specs=[
            pl.BlockSpec((tm, d), lambda i, j: (i, 0), pipeline_mode=pl.Buffered(1)),
            pl.BlockSpec((1, d), lambda i, j: (0, 0)),
            pl.BlockSpec((None, d, tf), lambda i, j: (layer, 0, j)),
            pl.BlockSpec((None, d, tf), lambda i, j: (layer, 0, j)),
            pl.BlockSpec((None, tf, d), lambda i, j: (layer, j, 0)),
            pl.BlockSpec((1, d), lambda i, j: (0, 0)),
        ],
        out_specs=pl.BlockSpec((tm, d), lambda i, j: (i, 0)),
        out_shape=jax.ShapeDtypeStruct((t, d), F32),
        scratch_shapes=[pltpu.VMEM((tm, d), BF16)],
        compiler_params=_params("parallel", "arbitrary"),
        name="ffn",
    )(x, g, wg, wu, wd, fg)


def _inproj_kernel(x_ref, g_ref, w_ref, wmq_ref, wng_ref, cs_ref, proj_ref, hn_ref, ngt_ref, xn_ref,
                   *, n_main):
    j = pl.program_id(1)

    @pl.when(j == 0)
    def _():
        xn = _rmsnorm(x_ref[...], g_ref[...]).astype(BF16)
        xn_ref[...] = xn
        hn_ref[...] = xn
        ngt_ref[...] = jax.nn.sigmoid(_dot_nt(wng_ref[...], xn))

    @pl.when(j < n_main)
    def _():
        proj_ref[...] = (_dot_nt(xn_ref[...], w_ref[...].astype(BF16)) * cs_ref[...]).astype(BF16)

    @pl.when(j >= n_main)
    def _():
        proj_ref[...] = (_dot_nt(xn_ref[...], wmq_ref[...]) * cs_ref[...]).astype(BF16)


def _inproj(x, g, w_in_t, layer, wmq, wng, cs):
    t, d = x.shape
    n = P_COLS
    tm = _pick(t, 1024)
    tn = MEM_Q_COLS
    assert P_MQ % tn == 0 and wmq.shape == (tn, d)
    n_main = P_MQ // tn
    return pl.pallas_call(
        functools.partial(_inproj_kernel, n_main=n_main),
        grid=(t // tm, n // tn),
        in_specs=[
            pl.BlockSpec((tm, d), lambda i, j: (i, 0)),
            pl.BlockSpec((1, d), lambda i, j: (0, 0)),
            pl.BlockSpec((None, tn, d), lambda i, j: (layer, jnp.minimum(j, n_main - 1), 0)),
            pl.BlockSpec((tn, d), lambda i, j: (0, 0)),
            pl.BlockSpec((LANES, d), lambda i, j: (0, 0)),
            pl.BlockSpec((1, tn), lambda i, j: (0, j)),
        ],
        out_specs=[
            pl.BlockSpec((tm, tn), lambda i, j: (i, j)),
            pl.BlockSpec((tm, d), lambda i, j: (i, 0)),
            pl.BlockSpec((LANES, tm), lambda i, j: (0, i)),
        ],
        out_shape=[
            jax.ShapeDtypeStruct((t, n), BF16),
            jax.ShapeDtypeStruct((t, d), BF16),
            jax.ShapeDtypeStruct((LANES, t), F32),
        ],
        scratch_shapes=[pltpu.VMEM((tm, d), BF16)],
        compiler_params=_params("parallel", "arbitrary"),
        name="inproj",
    )(x, g, w_in_t, wmq, wng, cs)


SUB = 128
ROWS_L = 16


def _assert_slopes_fit(slopes2):
    assert float(np.max(slopes2)) * SUB < 100.0, "ALiBi factor per bias block leaves the safe f32 range"


def _bias_factor_row(slope2, width):
    r = lax.broadcasted_iota(jnp.int32, (1, width), 1) & (SUB - 1)
    return jnp.exp2(slope2 * r.astype(F32))


def _scaled_vt(vt, w_row):
    ones = jnp.broadcast_to(w_row, (ROWS_L, w_row.shape[1]))
    return jnp.concatenate([vt * w_row, ones], axis=0).astype(BF16)


def _block_weights(z, off0, step, m_prev):
    nb = z.shape[0] // SUB
    blocks = [z[j * SUB:(j + 1) * SUB] for j in range(nb)]
    offs = [off0 + j * step for j in range(nb)]
    m_new = m_prev
    for zb, off in zip(blocks, offs):
        m_new = jnp.maximum(m_new, jnp.max(zb, axis=0, keepdims=True) + off)
    p = [jnp.exp2(zb - (m_new - off)).astype(BF16) for zb, off in zip(blocks, offs)]
    return m_new, jnp.concatenate(p, axis=0)


def _split_acc(acc, dv):
    return acc[0:dv] * (1.0 / jnp.maximum(acc[dv:dv + 1], 1e-30))


def _diff_kernel(slopes_ref, q_ref, k_ref, v_ref, lp_ref, sg_ref, o_ref,
                 vt_ref, m_ref, acc_ref, *, tile, n_tiles, lam_init):
    h = pl.program_id(1)
    qi = pl.program_id(2)
    slope2 = slopes_ref[h]

    @pl.when(qi == 0)
    def _():
        w_row = _bias_factor_row(slope2, tile)
        for t in range(n_tiles):
            vt_ref[t] = _scaled_vt(v_ref[t * tile:(t + 1) * tile, :].astype(F32).T, w_row)

    qt = _transpose_bf16(q_ref[...])
    feat = lax.broadcasted_iota(jnp.int32, qt.shape, 0)
    qts = [jnp.where((feat >= m * DIFF_QK_DIM) & (feat < (m + 1) * DIFF_QK_DIM), qt, 0.0).astype(BF16)
           for m in range(2)]

    m_ref[...] = jnp.full_like(m_ref, NEG_INF)
    acc_ref[...] = jnp.zeros_like(acc_ref)
    step = slope2 * float(SUB)

    def tile_step(t, causal):
        ks = pl.multiple_of(t * tile, tile)
        k = k_ref[pl.ds(ks, tile), :]
        vt = vt_ref[t]
        off0 = slope2 * ((t - qi) * tile).astype(F32)
        zs = [_dot(k, qts[m]) for m in range(2)]
        if causal:
            row = lax.broadcasted_iota(jnp.int32, zs[0].shape, 0)
            col = lax.broadcasted_iota(jnp.int32, zs[0].shape, 1)
            zs = [jnp.where(row <= col, z, NEG_INF) for z in zs]
        m_old = [m_ref[m] for m in range(2)]
        mp = [_block_weights(zs[m], off0, step, m_old[m]) for m in range(2)]
        for m in range(2):
            m_new, p = mp[m]
            acc_ref[m] = jnp.exp2(m_old[m] - m_new) * acc_ref[m] + _dot(vt, p)
            m_ref[m] = m_new

    def body(t, carry):
        tile_step(t, False)
        return carry

    lax.fori_loop(0, qi, body, 0)
    tile_step(qi, True)

    o1 = _split_acc(acc_ref[0], DIFF_V_DIM)
    o2 = _split_acc(acc_ref[1], DIFF_V_DIM)
    lp = lp_ref[...]
    lam = (jnp.exp(jnp.sum(lp[0:1] * lp[1:2], axis=-1, keepdims=True))
           - jnp.exp(jnp.sum(lp[2:3] * lp[3:4], axis=-1, keepdims=True)) + lam_init)
    ot = o1 - lam * o2
    ot = ot * lax.rsqrt(jnp.mean(ot * ot, axis=0, keepdims=True) + NORM_EPS)
    o_ref[...] = (ot.T * sg_ref[...] * (1.0 - lam_init)).astype(BF16)


def _diff_attention(proj, lp, sg, batch, seq, lam_init):
    t = proj.shape[0]
    tile = _pick(seq, 512)
    nq = seq // tile
    kb, vb = C_DK // LANES, C_DV // LANES
    slopes2 = _alibi_slopes(DIFF_HEADS) * np.float32(LOG2E)
    _assert_slopes_fit(slopes2)
    slopes = jnp.asarray(slopes2)
    assert tile % SUB == 0
    return pl.pallas_call(
        functools.partial(_diff_kernel, tile=tile, n_tiles=nq, lam_init=lam_init),
        grid=(batch, DIFF_HEADS, nq),
        in_specs=[
            pl.BlockSpec(memory_space=pltpu.SMEM),
            pl.BlockSpec((tile, LANES), lambda b, h, i: (b * nq + i, h)),
            pl.BlockSpec((seq, LANES), lambda b, h, i: (b, kb + h)),
            pl.BlockSpec((seq, LANES), lambda b, h, i: (b, vb + h)),
            pl.BlockSpec((4, DIFF_QK_DIM), lambda b, h, i: (0, 0)),
            pl.BlockSpec((1, DIFF_V_DIM), lambda b, h, i: (0, 0)),
        ],
        out_specs=pl.BlockSpec((tile, LANES), lambda b, h, i: (b * nq + i, h)),
        out_shape=jax.ShapeDtypeStruct((t, DIFF_HEADS * DIFF_V_DIM), BF16),
        scratch_shapes=[
            pltpu.VMEM((nq, DIFF_V_DIM + ROWS_L, tile), BF16),
            pltpu.VMEM((2, 1, tile), F32),
            pltpu.VMEM((2, DIFF_V_DIM + ROWS_L, tile), F32),
        ],
        compiler_params=_params("parallel", "parallel", "arbitrary"),
        name="diff_attn",
    )(slopes, proj, proj, proj, lp, sg)


def _compress_kernel(kv_ref, pos_ref, w1_ref, w2_ref, o_ref, ot_ref):
    half = w1_ref.shape[0] // 2
    kv = kv_ref[...]
    top = _dot(kv, w1_ref[:half, :])
    bot = _dot(kv, w1_ref[half:, :])
    rows = kv.shape[0]
    bot_next = pltpu.roll(bot, rows - 1, 0)
    pos_term = _dot(pos_ref[...], w1_ref[...])[0:1]
    hid = jax.nn.gelu(top + bot_next + pos_term)
    out = _dot(hid.astype(BF16), w2_ref[...])
    o_ref[...] = out.astype(BF16)
    ot_ref[...] = out.T.astype(BF16)


def _compress(kvr, pos, w1, w2):
    _, groups, batch, rows, width = kvr.shape
    d = w2.shape[-1]
    return pl.pallas_call(
        _compress_kernel,
        grid=(2, groups, batch),
        in_specs=[
            pl.BlockSpec((None, None, None, rows, width), lambda a, g, b: (a, g, b, 0, 0)),
            pl.BlockSpec((None, 8, 2 * width), lambda a, g, b: (a, 0, 0)),
            pl.BlockSpec((None, 2 * width, d), lambda a, g, b: (a, 0, 0)),
            pl.BlockSpec((None, d, d), lambda a, g, b: (a, 0, 0)),
        ],
        out_specs=[
            pl.BlockSpec((None, None, None, rows, d), lambda a, g, b: (a, b, g, 0, 0)),
            pl.BlockSpec((None, None, None, d, rows), lambda a, g, b: (a, b, g, 0, 0)),
        ],
        out_shape=[
            jax.ShapeDtypeStruct((2, batch, groups, rows, d), BF16),
            jax.ShapeDtypeStruct((2, batch, groups, d, rows), BF16),
        ],
        compiler_params=_params("parallel", "parallel", "parallel"),
        name="compress",
    )(kvr, pos, w1, w2)


def _nsa_kernel(slopes_ref, q_ref, gate_ref, kc_ref, vct_ref, ks_ref, vs_ref, kw_ref, vw_ref,
                ovt_ref, o_ref, vst_ref, vwt_ref, m_ref, acc_ref, *, tq, tk, seq):
    g = pl.program_id(1)
    qi = pl.program_id(2)
    q0 = qi * tq
    hp = NSA_HPG
    n = hp * tq
    dv = NSA_DIM
    span = WIN_LEN + tq
    n_slc = seq // SLC_LEN
    slope_row = jnp.concatenate(
        [jnp.full((1, tq), slopes_ref[g * hp + h], F32) for h in range(hp)], axis=1)

    @pl.when(qi == 0)
    def _():
        w_s = [_bias_factor_row(slopes_ref[g * hp + h], tk) for h in range(hp)]
        w_w = [_bias_factor_row(slopes_ref[g * hp + h], tq) for h in range(hp)]
        for t in range(seq // tk):
            vt = vs_ref[t * tk:(t + 1) * tk, :].astype(F32).T
            for h in range(hp):
                vst_ref[h, t] = _scaled_vt(vt, w_s[h])
        for t in range(seq // tq):
            vt = vw_ref[t * tq:(t + 1) * tq, :].astype(F32).T
            for h in range(hp):
                vwt_ref[h, t] = _scaled_vt(vt, w_w[h])

    q = q_ref[...]
    q4t = jnp.concatenate(
        [q[:, h * NSA_DIM:(h + 1) * NSA_DIM].astype(F32).T for h in range(hp)], axis=1).astype(BF16)
    qpos_all = q0 + (lax.broadcasted_iota(jnp.int32, (1, n), 1) & (tq - 1))
    qpos = q0 + lax.broadcasted_iota(jnp.int32, (1, tq), 1)

    def tile_heads(a):
        return jnp.concatenate([a] * hp, axis=1)

    n_cmp = kc_ref.shape[0]
    cmp_end = lax.broadcasted_iota(jnp.int32, (n_cmp, 1), 0) * CMP_STRIDE + (CMP_LEN - 1)
    mask_c = cmp_end <= qpos_all
    z_c = jnp.where(mask_c, _dot(kc_ref[...], q4t), NEG_INF)
    m_c = jnp.max(z_c, axis=0, keepdims=True)
    e_c = jnp.where(mask_c, jnp.exp2(z_c - m_c), 0.0)
    p_c = e_c * (1.0 / jnp.maximum(jnp.sum(e_c, axis=0, keepdims=True), 1e-30))
    o_c = _dot(vct_ref[...], p_c.astype(BF16))

    p_sum = p_c[:, 0:tq]
    for h in range(1, hp):
        p_sum = p_sum + p_c[:, h * tq:(h + 1) * tq]
    ovt = ovt_ref[...]
    p_hi = p_sum.astype(BF16)
    r1 = p_sum - p_hi.astype(F32)
    p_mid = r1.astype(BF16)
    p_lo = (r1 - p_mid.astype(F32)).astype(BF16)
    imp = _dot(ovt, p_hi) + _dot(ovt, p_mid) + _dot(ovt, p_lo)
    jblk = lax.broadcasted_iota(jnp.int32, (n_slc, tq), 0)
    cur = qpos // SLC_LEN
    forced = (jblk == 0) | (jblk == cur) | (jblk == cur - 1)
    imp = jnp.where(forced, SEL_BIG, jnp.where(jblk > cur, -SEL_BIG, imp))
    jblk_f = jblk.astype(F32)
    sel = jnp.zeros((n_slc, tq), F32)
    for _ in range(min(SLC_TOPK, n_slc)):
        mx = jnp.max(imp, axis=0, keepdims=True)
        first = jnp.min(jnp.where(imp == mx, jblk_f, float(n_slc)), axis=0, keepdims=True)
        hit = jblk_f == first
        sel = jnp.where(hit, 1.0, sel)
        imp = jnp.where(hit, -jnp.inf, imp)
    sel_bf = sel.astype(BF16)

    m_ref[...] = jnp.full_like(m_ref, NEG_INF)
    acc_ref[...] = jnp.zeros_like(acc_ref)
    exp_blk = lax.broadcasted_iota(jnp.int32, (tk, n_slc), 0) // SLC_LEN
    exp_col = lax.broadcasted_iota(jnp.int32, (tk, n_slc), 1)
    krow = lax.broadcasted_iota(jnp.int32, (tk, 1), 0)
    step_row = slope_row * float(SUB)
    assert hp == 4
    half = 2 * tq

    def step(t, carry):
        ks = pl.multiple_of(t * tk, tk)
        expand = jnp.where(exp_col == exp_blk + t * (tk // SLC_LEN), 1.0, 0.0).astype(BF16)
        chosen = _dot(expand, sel_bf)
        pen = jnp.where((chosen > 0.5) & ((krow + ks) <= qpos), 0.0, NEG_INF)
        k = ks_ref[pl.ds(ks, tk), :]
        off0 = slope_row * (ks - q0).astype(F32)
        pen2 = jnp.concatenate([pen, pen], axis=1)
        zs = [_dot(k, q4t[:, c:c + half]) + pen2 for c in (0, half)]
        m_old = [m_ref[:, c:c + half] for c in (0, half)]
        mp = [_block_weights(zs[i], off0[:, c:c + half], step_row[:, c:c + half], m_old[i])
              for i, c in enumerate((0, half))]
        for i, c in enumerate((0, half)):
            m_new, p = mp[i]
            pv = jnp.concatenate(
                [_dot(vst_ref[2 * i + j, t], p[:, j * tq:(j + 1) * tq]) for j in range(2)], axis=1)
            acc_ref[:, c:c + half] = jnp.exp2(m_old[i] - m_new) * acc_ref[:, c:c + half] + pv
            m_ref[:, c:c + half] = m_new
        return carry

    lax.fori_loop(0, q0 // tk + 1, step, 0)
    o_s = _split_acc(acc_ref[...], dv)

    wt = jnp.maximum(qi - WIN_LEN // tq, 0)
    w0 = pl.multiple_of(wt * tq, tq)
    dist = qpos - (w0 + lax.broadcasted_iota(jnp.int32, (span, 1), 0))
    pen_w = jnp.where((dist >= 0) & (dist < WIN_LEN), 0.0, NEG_INF)
    z_w = _dot(kw_ref[pl.ds(w0, span), :], q4t) + tile_heads(pen_w)
    _, p_w = _block_weights(z_w, 0.0, step_row, jnp.full((1, n), NEG_INF, F32))
    heads = []
    for h in range(hp):
        acc_w = _dot(vwt_ref[h, wt], p_w[0:tq, h * tq:(h + 1) * tq])
        for j in range(1, span // tq):
            acc_w = acc_w + _dot(vwt_ref[h, wt + j], p_w[j * tq:(j + 1) * tq, h * tq:(h + 1) * tq])
        heads.append(acc_w)
    o_w = _split_acc(jnp.concatenate(heads, axis=1), dv)

    def gate_row(branch):
        return jnp.concatenate(
            [gate_ref[pl.ds((g * hp + h) * 3 + branch, 1), :] for h in range(hp)], axis=1)

    ot = gate_row(0) * o_c + gate_row(1) * o_s + gate_row(2) * o_w
    o_ref[...] = jnp.concatenate(
        [ot[:, h * tq:(h + 1) * tq].T for h in range(hp)], axis=1).astype(BF16)


def _nsa_attention(proj, gates_t, k_cmp, vt_cmp, overlap_t, batch, seq):
    t = proj.shape[0]
    tq = _pick(seq, 128)
    tk = _pick(seq, 512)
    assert seq >= WIN_LEN + tq and WIN_LEN % tq == 0 and tk % tq == 0 and tq == SUB
    nq = seq // tq
    hp = NSA_HPG
    qb = C_NQ // (hp * NSA_DIM)
    kvb = C_NKV // LANES
    n_cmp = k_cmp.shape[3]
    slopes2 = _alibi_slopes(NSA_HEADS) * np.float32(LOG2E)
    _assert_slopes_fit(slopes2)
    slopes = jnp.asarray(slopes2)

    def kv_spec(branch, which):
        base = kvb + (branch * 2 + which) * NSA_GROUPS
        return pl.BlockSpec((seq, LANES), lambda b, g, i: (b, base + g))

    return pl.pallas_call(
        functools.partial(_nsa_kernel, tq=tq, tk=tk, seq=seq),
        grid=(batch, NSA_GROUPS, nq),
        in_specs=[
            pl.BlockSpec(memory_space=pltpu.SMEM),
            pl.BlockSpec((tq, hp * NSA_DIM), lambda b, g, i: (b * nq + i, qb + g)),
            pl.BlockSpec((LANES, tq), lambda b, g, i: (0, b * nq + i)),
            pl.BlockSpec((None, None, None, n_cmp, NSA_DIM), lambda b, g, i: (0, b, g, 0, 0)),
            pl.BlockSpec((None, None, None, NSA_DIM, n_cmp), lambda b, g, i: (1, b, g, 0, 0)),
            kv_spec(1, 0), kv_spec(1, 1),
            kv_spec(2, 0), kv_spec(2, 1),
            pl.BlockSpec(overlap_t.shape, lambda b, g, i: (0, 0)),
        ],
        out_specs=pl.BlockSpec((tq, hp * NSA_DIM), lambda b, g, i: (b * nq + i, g)),
        out_shape=jax.ShapeDtypeStruct((t, NSA_HEADS * NSA_DIM), BF16),
        scratch_shapes=[
            pltpu.VMEM((hp, seq // tk, NSA_DIM + ROWS_L, tk), BF16),
            pltpu.VMEM((hp, seq // tq, NSA_DIM + ROWS_L, tq), BF16),
            pltpu.VMEM((1, hp * tq), F32),
            pltpu.VMEM((NSA_DIM + ROWS_L, hp * tq), F32),
        ],
        compiler_params=_params("parallel", "parallel", "arbitrary"),
        name="nsa_attn",
    )(slopes, proj, gates_t, k_cmp, vt_cmp, proj, proj, proj, proj, overlap_t)


def _memkv_kernel(mem_ref, g_ref, w_ref, o_ref):
    mn = _rmsnorm(mem_ref[...], g_ref[...]).astype(BF16)
    o_ref[...] = _dot(mn, w_ref[...]).astype(BF16)


def _memkv(mem, g, w):
    r, d = mem.shape
    n = w.shape[1]
    tm = _pick(r, 256)
    return pl.pallas_call(
        _memkv_kernel,
        grid=(r // tm,),
        in_specs=[
            pl.BlockSpec((tm, d), lambda i: (i, 0)),
            pl.BlockSpec((1, d), lambda i: (0, 0)),
            pl.BlockSpec((d, n), lambda i: (0, 0)),
        ],
        out_specs=pl.BlockSpec((tm, n), lambda i: (i, 0)),
        out_shape=jax.ShapeDtypeStruct((r, n), BF16),
        compiler_params=_params("parallel"),
        name="mem_kv",
    )(mem, g, w)


def _memattn_kernel(q_ref, k_ref, v_ref, o_ref):
    scale = MEM_DIM ** -0.5
    q = q_ref[...]
    k = k_ref[...]
    v = v_ref[...]
    outs = []
    for h in range(MEM_HEADS):
        sl = slice(h * MEM_DIM, (h + 1) * MEM_DIM)
        s = _dot_nt(q[:, sl], k[:, sl]) * scale
        m = jnp.max(s, axis=-1, keepdims=True)
        e = jnp.exp(s - m)
        p = e / jnp.sum(e, axis=-1, keepdims=True)
        outs.append(_dot(p.astype(BF16), v[:, sl]))
    o_ref[...] = jnp.concatenate(outs, axis=-1).astype(BF16)


def _mem_attention(proj, memkv, batch, seq, mem_len):
    t = proj.shape[0]
    tq = _pick(seq, 512)
    nq = seq // tq
    width = MEM_HEADS * MEM_DIM
    qb = P_MQ // width
    return pl.pallas_call(
        _memattn_kernel,
        grid=(batch, nq),
        in_specs=[
            pl.BlockSpec((tq, width), lambda b, i: (b * nq + i, qb)),
            pl.BlockSpec((mem_len, width), lambda b, i: (b, 0)),
            pl.BlockSpec((mem_len, width), lambda b, i: (b, 1)),
        ],
        out_specs=pl.BlockSpec((tq, width), lambda b, i: (b * nq + i, 0)),
        out_shape=jax.ShapeDtypeStruct((t, width), BF16),
        compiler_params=_params("parallel", "parallel"),
        name="mem_attn",
    )(proj, memkv, memkv)


def _merge_kernel(x_ref, hn_ref, od_ref, on_ref, om_ref, wgd_ref, wgn_ref, wgm_ref,
                  wud_ref, wun_ref, wum_ref, wo_ref, o_ref, acc_ref):
    j = pl.program_id(1)

    @pl.when(j == 0)
    def _():
        acc_ref[...] = jnp.zeros_like(acc_ref)

    hn = hn_ref[...]
    merged = jax.nn.sigmoid(_dot_nt(hn, wgd_ref[...])) * _dot(od_ref[...], wud_ref[...])
    merged += jax.nn.sigmoid(_dot_nt(hn, wgn_ref[...])) * _dot(on_ref[...], wun_ref[...])
    merged += jax.nn.sigmoid(_dot_nt(hn, wgm_ref[...])) * _dot(om_ref[...], wum_ref[...])
    acc_ref[...] += _dot(merged.astype(BF16), wo_ref[...])

    @pl.when(j == pl.num_programs(1) - 1)
    def _():
        o_ref[...] = x_ref[...] + acc_ref[...]


def _merge(x, hn, od, on, om, wg, wud, wun, wum, wo):
    t, d = x.shape
    tm = _pick(t, 512)
    tj = _pick(d, 512)
    nj = d // tj

    def gate_spec(branch):
        return pl.BlockSpec((tj, d), lambda i, j: (branch * nj + j, 0))

    return pl.pallas_call(
        _merge_kernel,
        grid=(t // tm, nj),
        in_specs=[
            pl.BlockSpec((tm, d), lambda i, j: (i, 0)),
            pl.BlockSpec((tm, d), lambda i, j: (i, 0)),
            pl.BlockSpec((tm, od.shape[1]), lambda i, j: (i, 0)),
            pl.BlockSpec((tm, on.shape[1]), lambda i, j: (i, 0)),
            pl.BlockSpec((tm, om.shape[1]), lambda i, j: (i, 0)),
            gate_spec(0), gate_spec(1), gate_spec(2),
            pl.BlockSpec((wud.shape[0], tj), lambda i, j: (0, j)),
            pl.BlockSpec((wun.shape[0], tj), lambda i, j: (0, j)),
            pl.BlockSpec((wum.shape[0], tj), lambda i, j: (0, j)),
            pl.BlockSpec((tj, d), lambda i, j: (j, 0)),
        ],
        out_specs=pl.BlockSpec((tm, d), lambda i, j: (i, 0)),
        out_shape=jax.ShapeDtypeStruct((t, d), F32),
        scratch_shapes=[pltpu.VMEM((tm, d), F32)],
        compiler_params=_params("parallel", "arbitrary"),
        name="merge",
    )(x, hn, od, on, om, wg, wg, wg, wud, wun, wum, wo)


def _overlap_matrix_t(n_slc, n_cmp):
    cmp_start = np.arange(n_cmp)[None, :] * CMP_STRIDE
    slc_start = np.arange(n_slc)[:, None] * SLC_LEN
    return ((cmp_start < slc_start + SLC_LEN) & (cmp_start + CMP_LEN > slc_start)).astype(np.float32)


def kernel(x, mem, ffn1_norm, ffn1_w_gate, ffn1_w_up, ffn1_w_down, mix_norm, w_in, diff_lambda, diff_subln, nsa_cmp_pos, nsa_cmp_w1, nsa_cmp_w2, mem_norm, w_mem_kv, w_up_diff, w_up_nsa, w_up_mem, w_out, ffn2_norm, ffn2_w_gate, ffn2_w_up, ffn2_w_down, final_norm):
    batch, seq, d = x.shape
    mem_len = mem.shape[1]
    depth = w_in.shape[0]
    t = batch * seq
    assert seq % CMP_STRIDE == 0 and CMP_LEN == 2 * CMP_STRIDE
    n_rows = seq // CMP_STRIDE
    n_slc = seq // SLC_LEN
    overlap_t = jnp.asarray(_overlap_matrix_t(n_slc, n_rows), BF16)

    col_scale = np.ones((1, P_COLS), np.float32)
    col_scale[0, C_DQ:C_DK] = (DIFF_QK_DIM ** -0.5) * LOG2E
    col_scale[0, C_NQ:C_NKV] = (NSA_DIM ** -0.5) * LOG2E
    col_scale = jnp.asarray(col_scale)

    w_in_t = jnp.swapaxes(w_in, 1, 2)

    xt = x.reshape(t, d)
    memt = mem.reshape(batch * mem_len, d)
    fg = final_norm.reshape(1, d)
    bf = lambda a: a.astype(BF16)

    for l in range(depth):
        lam_init = 0.8 - 0.6 * math.exp(-0.3 * l)
        xt = _ffn(xt, ffn1_norm[l].reshape(1, d), ffn1_w_gate, ffn1_w_up, ffn1_w_down, fg, l, False)

        w = w_in_t[l]
        w_mq = bf(w[C_MQ:C_MG])
        w_ng = bf(jnp.pad(w[C_NG:C_MQ], ((0, LANES - NSA_GATE_COLS), (0, 0))))
        proj, hn, ngt = _inproj(xt, mix_norm[l].reshape(1, d), w_in_t, l, w_mq, w_ng, col_scale)

        o_diff = _diff_attention(proj, diff_lambda[l], diff_subln[l].reshape(1, DIFF_V_DIM),
                                 batch, seq, lam_init)

        ckv = proj[:, C_NKV:C_NKV + 2 * NSA_GROUPS * NSA_DIM]
        ckv = ckv.reshape(batch, n_rows, CMP_STRIDE, 2, NSA_GROUPS, NSA_DIM)
        ckv = ckv.transpose(3, 4, 0, 1, 2, 5).reshape(2, NSA_GROUPS, batch, n_rows, CMP_STRIDE * NSA_DIM)
        pos = bf(jnp.broadcast_to(nsa_cmp_pos[l].reshape(2, 1, CMP_LEN * NSA_DIM), (2, 8, CMP_LEN * NSA_DIM)))
        kv_cmp, kvt_cmp = _compress(ckv, pos, bf(nsa_cmp_w1[l]), bf(nsa_cmp_w2[l]))
        o_nsa = _nsa_attention(proj, ngt, kv_cmp, kvt_cmp, overlap_t, batch, seq)

        memkv = _memkv(memt, mem_norm[l].reshape(1, d), bf(w_mem_kv[l]))
        o_mem = _mem_attention(proj, memkv, batch, seq, mem_len)

        w_mg = bf(w[C_MG:])
        xt = _merge(xt, hn, o_diff, o_nsa, o_mem, w_mg, bf(w_up_diff[l]), bf(w_up_nsa[l]),
                    bf(w_up_mem[l]), bf(w_out[l]))

        xt = _ffn(xt, ffn2_norm[l].reshape(1, d), ffn2_w_gate, ffn2_w_up, ffn2_w_down, fg, l,
                  l == depth - 1)
    return xt.reshape(batch, seq, d)
```

```python
import functools
import math

import jax
import jax.numpy as jnp
import numpy as np
from jax import lax
from jax.experimental import pallas as pl
from jax.experimental.pallas import tpu as pltpu

F32 = jnp.float32
BF16 = jnp.bfloat16

NORM_EPS = 1e-6
NEG_INF = -1e30
SEL_BIG = 1e9
LOG2E = math.log2(math.e)

DIFF_HEADS = 8
DIFF_QK_DIM = 64
DIFF_V_DIM = 2 * DIFF_QK_DIM

NSA_HEADS = 8
NSA_GROUPS = 2
NSA_HPG = NSA_HEADS // NSA_GROUPS
NSA_DIM = 128
CMP_LEN = 32
CMP_STRIDE = 16
SLC_LEN = 64
SLC_TOPK = 8
WIN_LEN = 512

MEM_HEADS = 4
MEM_DIM = 128
N_BRANCH = 3

LANES = 128
VMEM_LIMIT = 56 * 1024 * 1024

DIFF_QK_COLS = DIFF_HEADS * 2 * DIFF_QK_DIM
DIFF_V_COLS = DIFF_HEADS * DIFF_V_DIM
NSA_Q_COLS = NSA_HEADS * NSA_DIM
NSA_KV_COLS = 3 * 2 * NSA_GROUPS * NSA_DIM
NSA_GATE_COLS = NSA_HEADS * 3
MEM_Q_COLS = MEM_HEADS * MEM_DIM
C_DQ = 0
C_DK = C_DQ + DIFF_QK_COLS
C_DV = C_DK + DIFF_QK_COLS
C_NQ = C_DV + DIFF_V_COLS
C_NKV = C_NQ + NSA_Q_COLS
C_NG = C_NKV + NSA_KV_COLS
C_MQ = C_NG + NSA_GATE_COLS
C_MG = C_MQ + MEM_Q_COLS
P_MQ = C_NG
P_COLS = P_MQ + MEM_Q_COLS


def _alibi_slopes(n_heads):
    return np.array([2.0 ** (-8.0 * (h + 1) / n_heads) for h in range(n_heads)], dtype=np.float32)


def _rmsnorm(x, g):
    y = x * lax.rsqrt(jnp.mean(x * x, axis=-1, keepdims=True) + NORM_EPS)
    return y * g


def _dot(a, b):
    return jnp.dot(a, b, preferred_element_type=F32)


def _dot_nt(a, b):
    return lax.dot_general(a, b, (((1,), (1,)), ((), ())), preferred_element_type=F32)


def _transpose_bf16(x):
    return x.astype(F32).T.astype(BF16)


def _params(*sem):
    return pltpu.CompilerParams(dimension_semantics=sem, vmem_limit_bytes=VMEM_LIMIT)


def _pick(n, pref):
    t = min(n, pref)
    assert n % t == 0, (n, t)
    return t


def _ffn_kernel(x_ref, g_ref, wg_ref, wu_ref, wd_ref, fg_ref, o_ref, xn_ref, *, final_norm):
    j = pl.program_id(1)

    @pl.when(j == 0)
    def _():
        xn_ref[...] = _rmsnorm(x_ref[...], g_ref[...]).astype(BF16)
        o_ref[...] = jnp.zeros_like(o_ref)

    xn = xn_ref[...]
    a = _dot(xn, wg_ref[...].astype(BF16))
    u = _dot(xn, wu_ref[...].astype(BF16))
    h = (a * jax.nn.sigmoid(a)) * u
    o_ref[...] += _dot(h.astype(BF16), wd_ref[...].astype(BF16))

    @pl.when(j == pl.num_programs(1) - 1)
    def _():
        y = x_ref[...] + 0.5 * o_ref[...]
        if final_norm:
            y = _rmsnorm(y, fg_ref[...])
        o_ref[...] = y


def _ffn(x, g, wg, wu, wd, fg, layer, final_norm):
    t, d = x.shape
    f = wg.shape[2]
    tm = _pick(t, 1024)
    tf = _pick(f, 256)
    return pl.pallas_call(
        functools.partial(_ffn_kernel, final_norm=final_norm),
        grid=(t // tm, f // tf),
        in_specs=[
            pl.BlockSpec((tm, d), lambda i, j: (i, 0), pipeline_mode=pl.Buffered(1)),
            pl.BlockSpec((1, d), lambda i, j: (0, 0)),
            pl.BlockSpec((None, d, tf), lambda i, j: (layer, 0, j)),
            pl.BlockSpec((None, d, tf), lambda i, j: (layer, 0, j)),
            pl.BlockSpec((None, tf, d), lambda i, j: (layer, j, 0)),
            pl.BlockSpec((1, d), lambda i, j: (0, 0)),
        ],
        out_specs=pl.BlockSpec((tm, d), lambda i, j: (i, 0)),
        out_shape=jax.ShapeDtypeStruct((t, d), F32),
        scratch_shapes=[pltpu.VMEM((tm, d), BF16)],
        compiler_params=_params("parallel", "arbitrary"),
        name="ffn",
    )(x, g, wg, wu, wd, fg)


def _inproj_kernel(x_ref, g_ref, w_ref, wmq_ref, wng_ref, cs_ref, proj_ref, hn_ref, ngt_ref, xn_ref,
                   *, n_main):
    j = pl.program_id(1)

    @pl.when(j == 0)
    def _():
        xn = _rmsnorm(x_ref[...], g_ref[...]).astype(BF16)
        xn_ref[...] = xn
        hn_ref[...] = xn
        ngt_ref[...] = jax.nn.sigmoid(_dot_nt(wng_ref[...], xn))

    @pl.when(j < n_main)
    def _():
        proj_ref[...] = (_dot_nt(xn_ref[...], w_ref[...].astype(BF16)) * cs_ref[...]).astype(BF16)

    @pl.when(j >= n_main)
    def _():
        proj_ref[...] = (_dot_nt(xn_ref[...], wmq_ref[...]) * cs_ref[...]).astype(BF16)


def _inproj(x, g, w_in_t, layer, wmq, wng, cs):
    t, d = x.shape
    n = P_COLS
    tm = _pick(t, 1024)
    tn = MEM_Q_COLS
    assert P_MQ % tn == 0 and wmq.shape == (tn, d)
    n_main = P_MQ // tn
    return pl.pallas_call(
        functools.partial(_inproj_kernel, n_main=n_main),
        grid=(t // tm, n // tn),
        in_specs=[
            pl.BlockSpec((tm, d), lambda i, j: (i, 0)),
            pl.BlockSpec((1, d), lambda i, j: (0, 0)),
            pl.BlockSpec((None, tn, d), lambda i, j: (layer, jnp.minimum(j, n_main - 1), 0)),
            pl.BlockSpec((tn, d), lambda i, j: (0, 0)),
            pl.BlockSpec((LANES, d), lambda i, j: (0, 0)),
            pl.BlockSpec((1, tn), lambda i, j: (0, j)),
        ],
        out_specs=[
            pl.BlockSpec((tm, tn), lambda i, j: (i, j)),
            pl.BlockSpec((tm, d), lambda i, j: (i, 0)),
            pl.BlockSpec((LANES, tm), lambda i, j: (0, i)),
        ],
        out_shape=[
            jax.ShapeDtypeStruct((t, n), BF16),
            jax.ShapeDtypeStruct((t, d), BF16),
            jax.ShapeDtypeStruct((LANES, t), F32),
        ],
        scratch_shapes=[pltpu.VMEM((tm, d), BF16)],
        compiler_params=_params("parallel", "arbitrary"),
        name="inproj",
    )(x, g, w_in_t, wmq, wng, cs)


SUB = 128
ROWS_L = 16


def _assert_slopes_fit(slopes2):
    assert float(np.max(slopes2)) * SUB < 100.0, "ALiBi factor per bias block leaves the safe f32 range"


def _bias_factor_row(slope2, width):
    r = lax.broadcasted_iota(jnp.int32, (1, width), 1) & (SUB - 1)
    return jnp.exp2(slope2 * r.astype(F32))


def _scaled_vt(vt, w_row):
    ones = jnp.broadcast_to(w_row, (ROWS_L, w_row.shape[1]))
    return jnp.concatenate([vt * w_row, ones], axis=0).astype(BF16)


def _block_weights(z, off0, step, m_prev):
    nb = z.shape[0] // SUB
    blocks = [z[j * SUB:(j + 1) * SUB] for j in range(nb)]
    offs = [off0 + j * step for j in range(nb)]
    m_new = m_prev
    for zb, off in zip(blocks, offs):
        m_new = jnp.maximum(m_new, jnp.max(zb, axis=0, keepdims=True) + off)
    p = [jnp.exp2(zb - (m_new - off)).astype(BF16) for zb, off in zip(blocks, offs)]
    return m_new, jnp.concatenate(p, axis=0)


def _split_acc(acc, dv):
    return acc[0:dv] * (1.0 / jnp.maximum(acc[dv:dv + 1], 1e-30))


def _diff_kernel(slopes_ref, q_ref, k_ref, v_ref, lp_ref, sg_ref, o_ref,
                 vt_ref, pen_ref, z_ref, mrow_ref, alpha_ref, m_ref, acc_ref, *, tile, n_tiles, lam_init):
    h = pl.program_id(1)
    qi = pl.program_id(2)
    slope2 = slopes_ref[h]
    nb = tile // SUB

    @pl.when(qi == 0)
    def _():
        w_row = _bias_factor_row(slope2, tile)
        for t in range(n_tiles):
            vt_ref[t] = _scaled_vt(v_ref[t * tile:(t + 1) * tile, :].astype(F32).T, w_row)
        row = lax.broadcasted_iota(jnp.int32, (tile, tile), 0)
        col = lax.broadcasted_iota(jnp.int32, (tile, tile), 1)
        pen_ref[0] = jnp.zeros((tile, tile), F32)
        pen_ref[1] = jnp.where(row <= col, 0.0, NEG_INF)

    qt = _transpose_bf16(q_ref[...])
    feat = lax.broadcasted_iota(jnp.int32, qt.shape, 0)
    qts = [jnp.where((feat >= m * DIFF_QK_DIM) & (feat < (m + 1) * DIFF_QK_DIM), qt, 0.0).astype(BF16)
           for m in range(2)]

    m_ref[...] = jnp.full_like(m_ref, NEG_INF)
    acc_ref[...] = jnp.zeros_like(acc_ref)
    step = slope2 * float(SUB)

    def block_off(t, j):
        return slope2 * ((t - qi) * tile).astype(F32) + j * step

    def scores(t, m):
        ks = pl.multiple_of(t * tile, tile)
        return _dot(k_ref[pl.ds(ks, tile), :], qts[m]) + pen_ref[(t == qi).astype(jnp.int32)]

    def bookkeep(t, slot, m, z):
        z_ref[slot, m] = z
        m_old = m_ref[m]
        m_new = m_old
        for j in range(nb):
            m_new = jnp.maximum(m_new, jnp.max(z[j * SUB:(j + 1) * SUB], axis=0, keepdims=True)
                                + block_off(t, j))
        m_ref[m] = m_new
        mrow_ref[slot, m] = m_new
        alpha_ref[slot, m] = jnp.exp2(m_old - m_new)

    def weights(t, slot, m):
        m_new = mrow_ref[slot, m]
        return jnp.concatenate(
            [jnp.exp2(z_ref[slot, m, j * SUB:(j + 1) * SUB, :] - (m_new - block_off(t, j))).astype(BF16)
             for j in range(nb)], axis=0)

    def accumulate(t, slot, m, p):
        acc_ref[m] = alpha_ref[slot, m] * acc_ref[m] + _dot(vt_ref[t], p)

    def overlapped(t, slot):
        for m in range(2):
            z_next = scores(t + 1, m)
            p = weights(t, slot, m)
            bookkeep(t + 1, 1 - slot, m, z_next)
            accumulate(t, slot, m, p)

    def drain(t, slot):
        for m in range(2):
            accumulate(t, slot, m, weights(t, slot, m))

    for m in range(2):
        bookkeep(0, 0, m, scores(0, m))

    def body(u, carry):
        overlapped(2 * u, 0)
        overlapped(2 * u + 1, 1)
        return carry

    lax.fori_loop(0, qi // 2, body, 0)

    @pl.when(qi % 2 == 1)
    def _():
        overlapped(qi - 1, 0)
        drain(qi, 1)

    @pl.when(qi % 2 == 0)
    def _():
        drain(qi, 0)

    o1 = _split_acc(acc_ref[0], DIFF_V_DIM)
    o2 = _split_acc(acc_ref[1], DIFF_V_DIM)
    lp = lp_ref[...]
    lam = (jnp.exp(jnp.sum(lp[0:1] * lp[1:2], axis=-1, keepdims=True))
           - jnp.exp(jnp.sum(lp[2:3] * lp[3:4], axis=-1, keepdims=True)) + lam_init)
    ot = o1 - lam * o2
    ot = ot * lax.rsqrt(jnp.mean(ot * ot, axis=0, keepdims=True) + NORM_EPS)
    o_ref[...] = (ot.T * sg_ref[...] * (1.0 - lam_init)).astype(BF16)


def _diff_attention(proj, lp, sg, batch, seq, lam_init):
    t = proj.shape[0]
    tile = _pick(seq, 512)
    nq = seq // tile
    kb, vb = C_DK // LANES, C_DV // LANES
    slopes2 = _alibi_slopes(DIFF_HEADS) * np.float32(LOG2E)
    _assert_slopes_fit(slopes2)
    slopes = jnp.asarray(slopes2)
    assert tile % SUB == 0
    return pl.pallas_call(
        functools.partial(_diff_kernel, tile=tile, n_tiles=nq, lam_init=lam_init),
        grid=(batch, DIFF_HEADS, nq),
        in_specs=[
            pl.BlockSpec(memory_space=pltpu.SMEM),
            pl.BlockSpec((tile, LANES), lambda b, h, i: (b * nq + i, h)),
            pl.BlockSpec((seq, LANES), lambda b, h, i: (b, kb + h)),
            pl.BlockSpec((seq, LANES), lambda b, h, i: (b, vb + h)),
            pl.BlockSpec((4, DIFF_QK_DIM), lambda b, h, i: (0, 0)),
            pl.BlockSpec((1, DIFF_V_DIM), lambda b, h, i: (0, 0)),
        ],
        out_specs=pl.BlockSpec((tile, LANES), lambda b, h, i: (b * nq + i, h)),
        out_shape=jax.ShapeDtypeStruct((t, DIFF_HEADS * DIFF_V_DIM), BF16),
        scratch_shapes=[
            pltpu.VMEM((nq, DIFF_V_DIM + ROWS_L, tile), BF16),
            pltpu.VMEM((2, tile, tile), F32),
            pltpu.VMEM((2, 2, tile, tile), F32),
            pltpu.VMEM((2, 2, 1, tile), F32),
            pltpu.VMEM((2, 2, 1, tile), F32),
            pltpu.VMEM((2, 1, tile), F32),
            pltpu.VMEM((2, DIFF_V_DIM + ROWS_L, tile), F32),
        ],
        compiler_params=_params("parallel", "parallel", "arbitrary"),
        name="diff_attn",
    )(slopes, proj, proj, proj, lp, sg)


def _compress_kernel(kv_ref, pos_ref, w1_ref, w2_ref, o_ref, ot_ref):
    half = w1_ref.shape[0] // 2
    kv = kv_ref[...]
    top = _dot(kv, w1_ref[:half, :])
    bot = _dot(kv, w1_ref[half:, :])
    rows = kv.shape[0]
    bot_next = pltpu.roll(bot, rows - 1, 0)
    pos_term = _dot(pos_ref[...], w1_ref[...])[0:1]
    hid = jax.nn.gelu(top + bot_next + pos_term)
    out = _dot(hid.astype(BF16), w2_ref[...])
    o_ref[...] = out.astype(BF16)
    ot_ref[...] = out.T.astype(BF16)


def _compress(kvr, pos, w1, w2):
    _, groups, batch, rows, width = kvr.shape
    d = w2.shape[-1]
    return pl.pallas_call(
        _compress_kernel,
        grid=(2, groups, batch),
        in_specs=[
            pl.BlockSpec((None, None, None, rows, width), lambda a, g, b: (a, g, b, 0, 0)),
            pl.BlockSpec((None, 8, 2 * width), lambda a, g, b: (a, 0, 0)),
            pl.BlockSpec((None, 2 * width, d), lambda a, g, b: (a, 0, 0)),
            pl.BlockSpec((None, d, d), lambda a, g, b: (a, 0, 0)),
        ],
        out_specs=[
            pl.BlockSpec((None, None, None, rows, d), lambda a, g, b: (a, b, g, 0, 0)),
            pl.BlockSpec((None, None, None, d, rows), lambda a, g, b: (a, b, g, 0, 0)),
        ],
        out_shape=[
            jax.ShapeDtypeStruct((2, batch, groups, rows, d), BF16),
            jax.ShapeDtypeStruct((2, batch, groups, d, rows), BF16),
        ],
        compiler_params=_params("parallel", "parallel", "parallel"),
        name="compress",
    )(kvr, pos, w1, w2)


def _nsa_kernel(slopes_ref, q_ref, gate_ref, kc_ref, vct_ref, ks_ref, vs_ref, kw_ref, vw_ref,
                ovt_ref, o_ref, vst_ref, vwt_ref, z_ref, mrow_ref, alpha_ref, m_ref, acc_ref,
                *, tq, tk, seq):
    g = pl.program_id(1)
    qi = pl.program_id(2)
    q0 = qi * tq
    hp = NSA_HPG
    n = hp * tq
    dv = NSA_DIM
    span = WIN_LEN + tq
    n_slc = seq // SLC_LEN
    slope_row = jnp.concatenate(
        [jnp.full((1, tq), slopes_ref[g * hp + h], F32) for h in range(hp)], axis=1)

    @pl.when(qi == 0)
    def _():
        w_s = [_bias_factor_row(slopes_ref[g * hp + h], tk) for h in range(hp)]
        w_w = [_bias_factor_row(slopes_ref[g * hp + h], tq) for h in range(hp)]
        for t in range(seq // tk):
            vt = vs_ref[t * tk:(t + 1) * tk, :].astype(F32).T
            for h in range(hp):
                vst_ref[h, t] = _scaled_vt(vt, w_s[h])
        for t in range(seq // tq):
            vt = vw_ref[t * tq:(t + 1) * tq, :].astype(F32).T
            for h in range(hp):
                vwt_ref[h, t] = _scaled_vt(vt, w_w[h])

    q = q_ref[...]
    q4t = jnp.concatenate(
        [q[:, h * NSA_DIM:(h + 1) * NSA_DIM].astype(F32).T for h in range(hp)], axis=1).astype(BF16)
    qpos_all = q0 + (lax.broadcasted_iota(jnp.int32, (1, n), 1) & (tq - 1))
    qpos = q0 + lax.broadcasted_iota(jnp.int32, (1, tq), 1)

    def tile_heads(a):
        return jnp.concatenate([a] * hp, axis=1)

    n_cmp = kc_ref.shape[0]
    cmp_end = lax.broadcasted_iota(jnp.int32, (n_cmp, 1), 0) * CMP_STRIDE + (CMP_LEN - 1)
    mask_c = cmp_end <= qpos_all
    z_c = jnp.where(mask_c, _dot(kc_ref[...], q4t), NEG_INF)
    m_c = jnp.max(z_c, axis=0, keepdims=True)
    e_c = jnp.where(mask_c, jnp.exp2(z_c - m_c), 0.0)
    p_c = e_c * (1.0 / jnp.maximum(jnp.sum(e_c, axis=0, keepdims=True), 1e-30))
    o_c = _dot(vct_ref[...], p_c.astype(BF16))

    p_sum = p_c[:, 0:tq]
    for h in range(1, hp):
        p_sum = p_sum + p_c[:, h * tq:(h + 1) * tq]
    ovt = ovt_ref[...]
    p_hi = p_sum.astype(BF16)
    r1 = p_sum - p_hi.astype(F32)
    p_mid = r1.astype(BF16)
    p_lo = (r1 - p_mid.astype(F32)).astype(BF16)
    imp = _dot(ovt, p_hi) + _dot(ovt, p_mid) + _dot(ovt, p_lo)
    jblk = lax.broadcasted_iota(jnp.int32, (n_slc, tq), 0)
    cur = qpos // SLC_LEN
    forced = (jblk == 0) | (jblk == cur) | (jblk == cur - 1)
    imp = jnp.where(forced, SEL_BIG, jnp.where(jblk > cur, -SEL_BIG, imp))
    jblk_f = jblk.astype(F32)
    sel = jnp.zeros((n_slc, tq), F32)
    for _ in range(min(SLC_TOPK, n_slc)):
        mx = jnp.max(imp, axis=0, keepdims=True)
        first = jnp.min(jnp.where(imp == mx, jblk_f, float(n_slc)), axis=0, keepdims=True)
        hit = jblk_f == first
        sel = jnp.where(hit, 1.0, sel)
        imp = jnp.where(hit, -jnp.inf, imp)
    sel_bf = sel.astype(BF16)

    m_ref[...] = jnp.full_like(m_ref, NEG_INF)
    acc_ref[...] = jnp.zeros_like(acc_ref)
    exp_blk = lax.broadcasted_iota(jnp.int32, (tk, n_slc), 0) // SLC_LEN
    exp_col = lax.broadcasted_iota(jnp.int32, (tk, n_slc), 1)
    krow = lax.broadcasted_iota(jnp.int32, (tk, 1), 0)
    step_row = slope_row * float(SUB)
    assert hp == 4
    half = 2 * tq

    nbk = tk // SUB

    def block_off(t, i, j):
        return slope_row[:, i * half:(i + 1) * half] * (t * tk - q0 + j * SUB).astype(F32)

    def penalty(t):
        expand = jnp.where(exp_col == exp_blk + t * (tk // SLC_LEN), 1.0, 0.0).astype(BF16)
        chosen = _dot(expand, sel_bf)
        pen = jnp.where((chosen > 0.5) & ((krow + t * tk) <= qpos), 0.0, NEG_INF)
        return jnp.concatenate([pen, pen], axis=1)

    def scores(t, i, pen2):
        ks = pl.multiple_of(t * tk, tk)
        return _dot(ks_ref[pl.ds(ks, tk), :], q4t[:, i * half:(i + 1) * half]) + pen2

    def bookkeep(t, slot, i, z):
        z_ref[slot, i] = z
        m_old = m_ref[i]
        m_new = m_old
        for j in range(nbk):
            m_new = jnp.maximum(m_new, jnp.max(z[j * SUB:(j + 1) * SUB], axis=0, keepdims=True)
                                + block_off(t, i, j))
        m_ref[i] = m_new
        mrow_ref[slot, i] = m_new
        alpha_ref[slot, i] = jnp.exp2(m_old - m_new)

    def weights(t, slot, i):
        m_new = mrow_ref[slot, i]
        return jnp.concatenate(
            [jnp.exp2(z_ref[slot, i, j * SUB:(j + 1) * SUB, :] - (m_new - block_off(t, i, j))).astype(BF16)
             for j in range(nbk)], axis=0)

    def accumulate(t, slot, i, p):
        pv = jnp.concatenate(
            [_dot(vst_ref[2 * i + j, t], p[:, j * tq:(j + 1) * tq]) for j in range(2)], axis=1)
        acc_ref[i] = alpha_ref[slot, i] * acc_ref[i] + pv

    def overlapped(t, slot):
        pen2 = penalty(t + 1)
        for i in range(2):
            z_next = scores(t + 1, i, pen2)
            p = weights(t, slot, i)
            bookkeep(t + 1, 1 - slot, i, z_next)
            accumulate(t, slot, i, p)

    def drain(t, slot):
        for i in range(2):
            accumulate(t, slot, i, weights(t, slot, i))

    pen2 = penalty(0)
    for i in range(2):
        bookkeep(0, 0, i, scores(0, i, pen2))
    n_last = q0 // tk

    def body(u, carry):
        overlapped(2 * u, 0)
        overlapped(2 * u + 1, 1)
        return carry

    lax.fori_loop(0, n_last // 2, body, 0)

    @pl.when(n_last % 2 == 1)
    def _():
        overlapped(n_last - 1, 0)
        drain(n_last, 1)

    @pl.when(n_last % 2 == 0)
    def _():
        drain(n_last, 0)

    o_s = jnp.concatenate([_split_acc(acc_ref[i], dv) for i in range(2)], axis=1)

    wt = jnp.maximum(qi - WIN_LEN // tq, 0)
    w0 = pl.multiple_of(wt * tq, tq)
    dist = qpos - (w0 + lax.broadcasted_iota(jnp.int32, (span, 1), 0))
    pen_w = jnp.where((dist >= 0) & (dist < WIN_LEN), 0.0, NEG_INF)
    z_w = _dot(kw_ref[pl.ds(w0, span), :], q4t) + tile_heads(pen_w)
    _, p_w = _block_weights(z_w, 0.0, step_row, jnp.full((1, n), NEG_INF, F32))
    heads = []
    for h in range(hp):
        acc_w = _dot(vwt_ref[h, wt], p_w[0:tq, h * tq:(h + 1) * tq])
        for j in range(1, span // tq):
            acc_w = acc_w + _dot(vwt_ref[h, wt + j], p_w[j * tq:(j + 1) * tq, h * tq:(h + 1) * tq])
        heads.append(acc_w)
    o_w = _split_acc(jnp.concatenate(heads, axis=1), dv)

    def gate_row(branch):
        return jnp.concatenate(
            [gate_ref[pl.ds((g * hp + h) * 3 + branch, 1), :] for h in range(hp)], axis=1)

    ot = gate_row(0) * o_c + gate_row(1) * o_s + gate_row(2) * o_w
    o_ref[...] = jnp.concatenate(
        [ot[:, h * tq:(h + 1) * tq].T for h in range(hp)], axis=1).astype(BF16)


def _nsa_attention(proj, gates_t, k_cmp, vt_cmp, overlap_t, batch, seq):
    t = proj.shape[0]
    tq = _pick(seq, 128)
    tk = _pick(seq, 512)
    assert seq >= WIN_LEN + tq and WIN_LEN % tq == 0 and tk % tq == 0 and tq == SUB
    nq = seq // tq
    hp = NSA_HPG
    qb = C_NQ // (hp * NSA_DIM)
    kvb = C_NKV // LANES
    n_cmp = k_cmp.shape[3]
    slopes2 = _alibi_slopes(NSA_HEADS) * np.float32(LOG2E)
    _assert_slopes_fit(slopes2)
    slopes = jnp.asarray(slopes2)

    def kv_spec(branch, which):
        base = kvb + (branch * 2 + which) * NSA_GROUPS
        return pl.BlockSpec((seq, LANES), lambda b, g, i: (b, base + g))

    return pl.pallas_call(
        functools.partial(_nsa_kernel, tq=tq, tk=tk, seq=seq),
        grid=(batch, NSA_GROUPS, nq),
        in_specs=[
            pl.BlockSpec(memory_space=pltpu.SMEM),
            pl.BlockSpec((tq, hp * NSA_DIM), lambda b, g, i: (b * nq + i, qb + g)),
            pl.BlockSpec((LANES, tq), lambda b, g, i: (0, b * nq + i)),
            pl.BlockSpec((None, None, None, n_cmp, NSA_DIM), lambda b, g, i: (0, b, g, 0, 0)),
            pl.BlockSpec((None, None, None, NSA_DIM, n_cmp), lambda b, g, i: (1, b, g, 0, 0)),
            kv_spec(1, 0), kv_spec(1, 1),
            kv_spec(2, 0), kv_spec(2, 1),
            pl.BlockSpec(overlap_t.shape, lambda b, g, i: (0, 0)),
        ],
        out_specs=pl.BlockSpec((tq, hp * NSA_DIM), lambda b, g, i: (b * nq + i, g)),
        out_shape=jax.ShapeDtypeStruct((t, NSA_HEADS * NSA_DIM), BF16),
        scratch_shapes=[
            pltpu.VMEM((hp, seq // tk, NSA_DIM + ROWS_L, tk), BF16),
            pltpu.VMEM((hp, seq // tq, NSA_DIM + ROWS_L, tq), BF16),
            pltpu.VMEM((2, 2, tk, 2 * tq), F32),
            pltpu.VMEM((2, 2, 1, 2 * tq), F32),
            pltpu.VMEM((2, 2, 1, 2 * tq), F32),
            pltpu.VMEM((2, 1, 2 * tq), F32),
            pltpu.VMEM((2, NSA_DIM + ROWS_L, 2 * tq), F32),
        ],
        compiler_params=_params("parallel", "parallel", "arbitrary"),
        name="nsa_attn",
    )(slopes, proj, gates_t, k_cmp, vt_cmp, proj, proj, proj, proj, overlap_t)


def _memkv_kernel(mem_ref, g_ref, w_ref, o_ref):
    mn = _rmsnorm(mem_ref[...], g_ref[...]).astype(BF16)
    o_ref[...] = _dot(mn, w_ref[...]).astype(BF16)


def _memkv(mem, g, w):
    r, d = mem.shape
    n = w.shape[1]
    tm = _pick(r, 256)
    return pl.pallas_call(
        _memkv_kernel,
        grid=(r // tm,),
        in_specs=[
            pl.BlockSpec((tm, d), lambda i: (i, 0)),
            pl.BlockSpec((1, d), lambda i: (0, 0)),
            pl.BlockSpec((d, n), lambda i: (0, 0)),
        ],
        out_specs=pl.BlockSpec((tm, n), lambda i: (i, 0)),
        out_shape=jax.ShapeDtypeStruct((r, n), BF16),
        compiler_params=_params("parallel"),
        name="mem_kv",
    )(mem, g, w)


def _memattn_kernel(q_ref, k_ref, v_ref, o_ref):
    scale = MEM_DIM ** -0.5
    q = q_ref[...]
    k = k_ref[...]
    v = v_ref[...]
    outs = []
    for h in range(MEM_HEADS):
        sl = slice(h * MEM_DIM, (h + 1) * MEM_DIM)
        s = _dot_nt(q[:, sl], k[:, sl]) * scale
        m = jnp.max(s, axis=-1, keepdims=True)
        e = jnp.exp(s - m)
        p = e / jnp.sum(e, axis=-1, keepdims=True)
        outs.append(_dot(p.astype(BF16), v[:, sl]))
    o_ref[...] = jnp.concatenate(outs, axis=-1).astype(BF16)


def _mem_attention(proj, memkv, batch, seq, mem_len):
    t = proj.shape[0]
    tq = _pick(seq, 512)
    nq = seq // tq
    width = MEM_HEADS * MEM_DIM
    qb = P_MQ // width
    return pl.pallas_call(
        _memattn_kernel,
        grid=(batch, nq),
        in_specs=[
            pl.BlockSpec((tq, width), lambda b, i: (b * nq + i, qb)),
            pl.BlockSpec((mem_len, width), lambda b, i: (b, 0)),
            pl.BlockSpec((mem_len, width), lambda b, i: (b, 1)),
        ],
        out_specs=pl.BlockSpec((tq, width), lambda b, i: (b * nq + i, 0)),
        out_shape=jax.ShapeDtypeStruct((t, width), BF16),
        compiler_params=_params("parallel", "parallel"),
        name="mem_attn",
    )(proj, memkv, memkv)


def _merge_kernel(x_ref, hn_ref, od_ref, on_ref, om_ref, wgd_ref, wgn_ref, wgm_ref,
                  wud_ref, wun_ref, wum_ref, wo_ref, o_ref, acc_ref):
    j = pl.program_id(1)

    @pl.when(j == 0)
    def _():
        acc_ref[...] = jnp.zeros_like(acc_ref)

    hn = hn_ref[...]
    merged = jax.nn.sigmoid(_dot_nt(hn, wgd_ref[...])) * _dot(od_ref[...], wud_ref[...])
    merged += jax.nn.sigmoid(_dot_nt(hn, wgn_ref[...])) * _dot(on_ref[...], wun_ref[...])
    merged += jax.nn.sigmoid(_dot_nt(hn, wgm_ref[...])) * _dot(om_ref[...], wum_ref[...])
    acc_ref[...] += _dot(merged.astype(BF16), wo_ref[...])

    @pl.when(j == pl.num_programs(1) - 1)
    def _():
        o_ref[...] = x_ref[...] + acc_ref[...]


def _merge(x, hn, od, on, om, wg, wud, wun, wum, wo):
    t, d = x.shape
    tm = _pick(t, 512)
    tj = _pick(d, 512)
    nj = d // tj

    def gate_spec(branch):
        return pl.BlockSpec((tj, d), lambda i, j: (branch * nj + j, 0))

    return pl.pallas_call(
        _merge_kernel,
        grid=(t // tm, nj),
        in_specs=[
            pl.BlockSpec((tm, d), lambda i, j: (i, 0)),
            pl.BlockSpec((tm, d), lambda i, j: (i, 0)),
            pl.BlockSpec((tm, od.shape[1]), lambda i, j: (i, 0)),
            pl.BlockSpec((tm, on.shape[1]), lambda i, j: (i, 0)),
            pl.BlockSpec((tm, om.shape[1]), lambda i, j: (i, 0)),
            gate_spec(0), gate_spec(1), gate_spec(2),
            pl.BlockSpec((wud.shape[0], tj), lambda i, j: (0, j)),
            pl.BlockSpec((wun.shape[0], tj), lambda i, j: (0, j)),
            pl.BlockSpec((wum.shape[0], tj), lambda i, j: (0, j)),
            pl.BlockSpec((tj, d), lambda i, j: (j, 0)),
        ],
        out_specs=pl.BlockSpec((tm, d), lambda i, j: (i, 0)),
        out_shape=jax.ShapeDtypeStruct((t, d), F32),
        scratch_shapes=[pltpu.VMEM((tm, d), F32)],
        compiler_params=_params("parallel", "arbitrary"),
        name="merge",
    )(x, hn, od, on, om, wg, wg, wg, wud, wun, wum, wo)


def _overlap_matrix_t(n_slc, n_cmp):
    cmp_start = np.arange(n_cmp)[None, :] * CMP_STRIDE
    slc_start = np.arange(n_slc)[:, None] * SLC_LEN
    return ((cmp_start < slc_start + SLC_LEN) & (cmp_start + CMP_LEN > slc_start)).astype(np.float32)


def kernel(x, mem, ffn1_norm, ffn1_w_gate, ffn1_w_up, ffn1_w_down, mix_norm, w_in, diff_lambda, diff_subln, nsa_cmp_pos, nsa_cmp_w1, nsa_cmp_w2, mem_norm, w_mem_kv, w_up_diff, w_up_nsa, w_up_mem, w_out, ffn2_norm, ffn2_w_gate, ffn2_w_up, ffn2_w_down, final_norm):
    batch, seq, d = x.shape
    mem_len = mem.shape[1]
    depth = w_in.shape[0]
    t = batch * seq
    assert seq % CMP_STRIDE == 0 and CMP_LEN == 2 * CMP_STRIDE
    n_rows = seq // CMP_STRIDE
    n_slc = seq // SLC_LEN
    overlap_t = jnp.asarray(_overlap_matrix_t(n_slc, n_rows), BF16)

    col_scale = np.ones((1, P_COLS), np.float32)
    col_scale[0, C_DQ:C_DK] = (DIFF_QK_DIM ** -0.5) * LOG2E
    col_scale[0, C_NQ:C_NKV] = (NSA_DIM ** -0.5) * LOG2E
    col_scale = jnp.asarray(col_scale)

    w_in_t = jnp.swapaxes(w_in, 1, 2)

    xt = x.reshape(t, d)
    memt = mem.reshape(batch * mem_len, d)
    fg = final_norm.reshape(1, d)
    bf = lambda a: a.astype(BF16)

    for l in range(depth):
        lam_init = 0.8 - 0.6 * math.exp(-0.3 * l)
        xt = _ffn(xt, ffn1_norm[l].reshape(1, d), ffn1_w_gate, ffn1_w_up, ffn1_w_down, fg, l, False)

        w = w_in_t[l]
        w_mq = bf(w[C_MQ:C_MG])
        w_ng = bf(jnp.pad(w[C_NG:C_MQ], ((0, LANES - NSA_GATE_COLS), (0, 0))))
        proj, hn, ngt = _inproj(xt, mix_norm[l].reshape(1, d), w_in_t, l, w_mq, w_ng, col_scale)

        o_diff = _diff_attention(proj, diff_lambda[l], diff_subln[l].reshape(1, DIFF_V_DIM),
                                 batch, seq, lam_init)

        ckv = proj[:, C_NKV:C_NKV + 2 * NSA_GROUPS * NSA_DIM]
        ckv = ckv.reshape(batch, n_rows, CMP_STRIDE, 2, NSA_GROUPS, NSA_DIM)
        ckv = ckv.transpose(3, 4, 0, 1, 2, 5).reshape(2, NSA_GROUPS, batch, n_rows, CMP_STRIDE * NSA_DIM)
        pos = bf(jnp.broadcast_to(nsa_cmp_pos[l].reshape(2, 1, CMP_LEN * NSA_DIM), (2, 8, CMP_LEN * NSA_DIM)))
        kv_cmp, kvt_cmp = _compress(ckv, pos, bf(nsa_cmp_w1[l]), bf(nsa_cmp_w2[l]))
        o_nsa = _nsa_attention(proj, ngt, kv_cmp, kvt_cmp, overlap_t, batch, seq)

        memkv = _memkv(memt, mem_norm[l].reshape(1, d), bf(w_mem_kv[l]))
        o_mem = _mem_attention(proj, memkv, batch, seq, mem_len)

        w_mg = bf(w[C_MG:])
        xt = _merge(xt, hn, o_diff, o_nsa, o_mem, w_mg, bf(w_up_diff[l]), bf(w_up_nsa[l]),
                    bf(w_up_mem[l]), bf(w_out[l]))

        xt = _ffn(xt, ffn2_norm[l].reshape(1, d), ffn2_w_gate, ffn2_w_up, ffn2_w_down, fg, l,
                  l == depth - 1)
    return xt.reshape(batch, seq, d)
```

```python
import functools
import math

import jax
import jax.numpy as jnp
import numpy as np
from jax import lax
from jax.experimental import pallas as pl
from jax.experimental.pallas import tpu as pltpu

F32 = jnp.float32
BF16 = jnp.bfloat16

NORM_EPS = 1e-6
NEG_INF = -1e30
SEL_BIG = 1e9
LOG2E = math.log2(math.e)

DIFF_HEADS = 8
DIFF_QK_DIM = 64
DIFF_V_DIM = 2 * DIFF_QK_DIM

NSA_HEADS = 8
NSA_GROUPS = 2
NSA_HPG = NSA_HEADS // NSA_GROUPS
NSA_DIM = 128
CMP_LEN = 32
CMP_STRIDE = 16
SLC_LEN = 64
SLC_TOPK = 8
WIN_LEN = 512

MEM_HEADS = 4
MEM_DIM = 128
N_BRANCH = 3

LANES = 128
VMEM_LIMIT = 56 * 1024 * 1024

DIFF_QK_COLS = DIFF_HEADS * 2 * DIFF_QK_DIM
DIFF_V_COLS = DIFF_HEADS * DIFF_V_DIM
NSA_Q_COLS = NSA_HEADS * NSA_DIM
NSA_KV_COLS = 3 * 2 * NSA_GROUPS * NSA_DIM
NSA_GATE_COLS = NSA_HEADS * 3
MEM_Q_COLS = MEM_HEADS * MEM_DIM
C_DQ = 0
C_DK = C_DQ + DIFF_QK_COLS
C_DV = C_DK + DIFF_QK_COLS
C_NQ = C_DV + DIFF_V_COLS
C_NKV = C_NQ + NSA_Q_COLS
C_NG = C_NKV + NSA_KV_COLS
C_MQ = C_NG + NSA_GATE_COLS
C_MG = C_MQ + MEM_Q_COLS
P_MQ = C_NG
P_COLS = P_MQ + MEM_Q_COLS


def _alibi_slopes(n_heads):
    return np.array([2.0 ** (-8.0 * (h + 1) / n_heads) for h in range(n_heads)], dtype=np.float32)


def _rmsnorm(x, g):
    y = x * lax.rsqrt(jnp.mean(x * x, axis=-1, keepdims=True) + NORM_EPS)
    return y * g


def _dot(a, b):
    return jnp.dot(a, b, preferred_element_type=F32)


def _dot_nt(a, b):
    return lax.dot_general(a, b, (((1,), (1,)), ((), ())), preferred_element_type=F32)


def _transpose_bf16(x):
    return x.astype(F32).T.astype(BF16)


def _params(*sem):
    return pltpu.CompilerParams(dimension_semantics=sem, vmem_limit_bytes=VMEM_LIMIT)


def _pick(n, pref):
    t = min(n, pref)
    assert n % t == 0, (n, t)
    return t


def _ffn_kernel(x_ref, g_ref, wg_ref, wu_ref, wd_ref, fg_ref, o_ref, *rest, tail):
    xn_ref = rest[-1]
    j = pl.program_id(1)

    @pl.when(j == 0)
    def _():
        xn_ref[...] = _rmsnorm(x_ref[...], g_ref[...]).astype(BF16)
        o_ref[...] = jnp.zeros_like(o_ref)

    xn = xn_ref[...]
    a = _dot(xn, wg_ref[...].astype(BF16))
    u = _dot(xn, wu_ref[...].astype(BF16))
    h = (a * jax.nn.sigmoid(a)) * u
    o_ref[...] += _dot(h.astype(BF16), wd_ref[...].astype(BF16))

    @pl.when(j == pl.num_programs(1) - 1)
    def _():
        rows = 128

        def chunk(c, carry):
            r = pl.ds(pl.multiple_of(c * rows, rows), rows)
            y = x_ref[r, :] + 0.5 * o_ref[r, :]
            if tail == "final":
                y = _rmsnorm(y, fg_ref[...])
            if tail == "normed":
                rest[0][r, :] = _rmsnorm(y, fg_ref[...]).astype(BF16)
            o_ref[r, :] = y
            return carry

        lax.fori_loop(0, o_ref.shape[0] // rows, chunk, 0)


def _ffn(x, g, wg, wu, wd, fg, layer, tail):
    t, d = x.shape
    f = wg.shape[2]
    tm = _pick(t, 1024)
    tf = _pick(f, 256)
    out_specs = [pl.BlockSpec((tm, d), lambda i, j: (i, 0))]
    out_shape = [jax.ShapeDtypeStruct((t, d), F32)]
    if tail == "normed":
        out_specs.append(pl.BlockSpec((tm, d), lambda i, j: (i, 0)))
        out_shape.append(jax.ShapeDtypeStruct((t, d), BF16))
    return pl.pallas_call(
        functools.partial(_ffn_kernel, tail=tail),
        grid=(t // tm, f // tf),
        in_specs=[
            pl.BlockSpec((tm, d), lambda i, j: (i, 0), pipeline_mode=pl.Buffered(1)),
            pl.BlockSpec((1, d), lambda i, j: (0, 0)),
            pl.BlockSpec((None, d, tf), lambda i, j: (layer, 0, j)),
            pl.BlockSpec((None, d, tf), lambda i, j: (layer, 0, j)),
            pl.BlockSpec((None, tf, d), lambda i, j: (layer, j, 0)),
            pl.BlockSpec((1, d), lambda i, j: (0, 0)),
        ],
        out_specs=out_specs,
        out_shape=out_shape,
        scratch_shapes=[] if tail == "normed" else [pltpu.VMEM((tm, d), BF16)],
        compiler_params=_params("parallel", "arbitrary"),
        name="ffn",
    )(x, g, wg, wu, wd, fg)


def _inproj_kernel(hn_ref, w_ref, wmq_ref, wng_ref, cs_ref, proj_ref, ngt_ref, *, n_main):
    j = pl.program_id(1)

    @pl.when(j == 0)
    def _():
        ngt_ref[...] = jax.nn.sigmoid(_dot_nt(wng_ref[...], hn_ref[...]))

    @pl.when(j < n_main)
    def _():
        proj_ref[...] = (_dot_nt(hn_ref[...], w_ref[...].astype(BF16)) * cs_ref[...]).astype(BF16)

    @pl.when(j >= n_main)
    def _():
        proj_ref[...] = (_dot_nt(hn_ref[...], wmq_ref[...]) * cs_ref[...]).astype(BF16)


def _inproj(hn, w_in_t, layer, wmq, wng, cs):
    t, d = hn.shape
    n = P_COLS
    tm = _pick(t, 2048)
    tn = MEM_Q_COLS
    assert P_MQ % tn == 0 and wmq.shape == (tn, d)
    n_main = P_MQ // tn
    return pl.pallas_call(
        functools.partial(_inproj_kernel, n_main=n_main),
        grid=(t // tm, n // tn),
        in_specs=[
            pl.BlockSpec((tm, d), lambda i, j: (i, 0)),
            pl.BlockSpec((None, tn, d), lambda i, j: (layer, jnp.minimum(j, n_main - 1), 0)),
            pl.BlockSpec((tn, d), lambda i, j: (0, 0)),
            pl.BlockSpec((LANES, d), lambda i, j: (0, 0)),
            pl.BlockSpec((1, tn), lambda i, j: (0, j)),
        ],
        out_specs=[
            pl.BlockSpec((tm, tn), lambda i, j: (i, j)),
            pl.BlockSpec((LANES, tm), lambda i, j: (0, i)),
        ],
        out_shape=[
            jax.ShapeDtypeStruct((t, n), BF16),
            jax.ShapeDtypeStruct((LANES, t), F32),
        ],
        compiler_params=_params("parallel", "arbitrary"),
        name="inproj",
    )(hn, w_in_t, wmq, wng, cs)


SUB = 128
ROWS_L = 16


def _assert_slopes_fit(slopes2):
    assert float(np.max(slopes2)) * SUB < 100.0, "ALiBi factor per bias block leaves the safe f32 range"


def _bias_factor_row(slope2, width):
    r = lax.broadcasted_iota(jnp.int32, (1, width), 1) & (SUB - 1)
    return jnp.exp2(slope2 * r.astype(F32))


def _scaled_vt(vt, w_row):
    ones = jnp.broadcast_to(w_row, (ROWS_L, w_row.shape[1]))
    return jnp.concatenate([vt * w_row, ones], axis=0).astype(BF16)


def _block_weights(z, off0, step, m_prev):
    nb = z.shape[0] // SUB
    blocks = [z[j * SUB:(j + 1) * SUB] for j in range(nb)]
    offs = [off0 + j * step for j in range(nb)]
    m_new = m_prev
    for zb, off in zip(blocks, offs):
        m_new = jnp.maximum(m_new, jnp.max(zb, axis=0, keepdims=True) + off)
    p = [jnp.exp2(zb - (m_new - off)).astype(BF16) for zb, off in zip(blocks, offs)]
    return m_new, jnp.concatenate(p, axis=0)


def _split_acc(acc, dv):
    return acc[0:dv] * (1.0 / jnp.maximum(acc[dv:dv + 1], 1e-30))


def _diff_kernel(slopes_ref, q_ref, k_ref, v_ref, lp_ref, sg_ref, o_ref,
                 vt_ref, pen_ref, z_ref, mrow_ref, alpha_ref, m_ref, acc_ref, *, tile, n_tiles, lam_init):
    h = pl.program_id(1)
    qi = pl.program_id(2)
    slope2 = slopes_ref[h]
    nb = tile // SUB

    @pl.when(qi == 0)
    def _():
        w_row = _bias_factor_row(slope2, tile)
        for t in range(n_tiles):
            vt_ref[t] = _scaled_vt(v_ref[t * tile:(t + 1) * tile, :].astype(F32).T, w_row)
        row = lax.broadcasted_iota(jnp.int32, (tile, tile), 0)
        col = lax.broadcasted_iota(jnp.int32, (tile, tile), 1)
        pen_ref[0] = jnp.zeros((tile, tile), F32)
        pen_ref[1] = jnp.where(row <= col, 0.0, NEG_INF)

    qt = _transpose_bf16(q_ref[...])
    feat = lax.broadcasted_iota(jnp.int32, qt.shape, 0)
    qts = [jnp.where((feat >= m * DIFF_QK_DIM) & (feat < (m + 1) * DIFF_QK_DIM), qt, 0.0).astype(BF16)
           for m in range(2)]

    m_ref[...] = jnp.full_like(m_ref, NEG_INF)
    acc_ref[...] = jnp.zeros_like(acc_ref)
    step = slope2 * float(SUB)

    def block_off(t, j):
        return slope2 * ((t - qi) * tile).astype(F32) + j * step

    def scores(t, m):
        ks = pl.multiple_of(t * tile, tile)
        return _dot(k_ref[pl.ds(ks, tile), :], qts[m]) + pen_ref[(t == qi).astype(jnp.int32)]

    def bookkeep(t, slot, m, z):
        z_ref[slot, m] = z
        m_old = m_ref[m]
        m_new = m_old
        for j in range(nb):
            m_new = jnp.maximum(m_new, jnp.max(z[j * SUB:(j + 1) * SUB], axis=0, keepdims=True)
                                + block_off(t, j))
        m_ref[m] = m_new
        mrow_ref[slot, m] = m_new
        alpha_ref[slot, m] = jnp.exp2(m_old - m_new)

    def weights(t, slot, m):
        m_new = mrow_ref[slot, m]
        return jnp.concatenate(
            [jnp.exp2(z_ref[slot, m, j * SUB:(j + 1) * SUB, :] - (m_new - block_off(t, j))).astype(BF16)
             for j in range(nb)], axis=0)

    def accumulate(t, slot, m, p):
        acc_ref[m] = alpha_ref[slot, m] * acc_ref[m] + _dot(vt_ref[t], p)

    def overlapped(t, slot):
        for m in range(2):
            z_next = scores(t + 1, m)
            p = weights(t, slot, m)
            bookkeep(t + 1, 1 - slot, m, z_next)
            accumulate(t, slot, m, p)

    def drain(t, slot):
        for m in range(2):
            accumulate(t, slot, m, weights(t, slot, m))

    for m in range(2):
        bookkeep(0, 0, m, scores(0, m))

    def body(u, carry):
        overlapped(2 * u, 0)
        overlapped(2 * u + 1, 1)
        return carry

    lax.fori_loop(0, qi // 2, body, 0)

    @pl.when(qi % 2 == 1)
    def _():
        overlapped(qi - 1, 0)
        drain(qi, 1)

    @pl.when(qi % 2 == 0)
    def _():
        drain(qi, 0)

    o1 = _split_acc(acc_ref[0], DIFF_V_DIM)
    o2 = _split_acc(acc_ref[1], DIFF_V_DIM)
    lp = lp_ref[...]
    lam = (jnp.exp(jnp.sum(lp[0:1] * lp[1:2], axis=-1, keepdims=True))
           - jnp.exp(jnp.sum(lp[2:3] * lp[3:4], axis=-1, keepdims=True)) + lam_init)
    ot = o1 - lam * o2
    ot = ot * lax.rsqrt(jnp.mean(ot * ot, axis=0, keepdims=True) + NORM_EPS)
    o_ref[...] = (ot.T * sg_ref[...] * (1.0 - lam_init)).astype(BF16)


def _diff_attention(proj, lp, sg, batch, seq, lam_init):
    t = proj.shape[0]
    tile = _pick(seq, 512)
    nq = seq // tile
    kb, vb = C_DK // LANES, C_DV // LANES
    slopes2 = _alibi_slopes(DIFF_HEADS) * np.float32(LOG2E)
    _assert_slopes_fit(slopes2)
    slopes = jnp.asarray(slopes2)
    assert tile % SUB == 0
    return pl.pallas_call(
        functools.partial(_diff_kernel, tile=tile, n_tiles=nq, lam_init=lam_init),
        grid=(batch, DIFF_HEADS, nq),
        in_specs=[
            pl.BlockSpec(memory_space=pltpu.SMEM),
            pl.BlockSpec((tile, LANES), lambda b, h, i: (b * nq + i, h)),
            pl.BlockSpec((seq, LANES), lambda b, h, i: (b, kb + h)),
            pl.BlockSpec((seq, LANES), lambda b, h, i: (b, vb + h)),
            pl.BlockSpec((4, DIFF_QK_DIM), lambda b, h, i: (0, 0)),
            pl.BlockSpec((1, DIFF_V_DIM), lambda b, h, i: (0, 0)),
        ],
        out_specs=pl.BlockSpec((tile, LANES), lambda b, h, i: (b * nq + i, h)),
        out_shape=jax.ShapeDtypeStruct((t, DIFF_HEADS * DIFF_V_DIM), BF16),
        scratch_shapes=[
            pltpu.VMEM((nq, DIFF_V_DIM + ROWS_L, tile), BF16),
            pltpu.VMEM((2, tile, tile), F32),
            pltpu.VMEM((2, 2, tile, tile), F32),
            pltpu.VMEM((2, 2, 1, tile), F32),
            pltpu.VMEM((2, 2, 1, tile), F32),
            pltpu.VMEM((2, 1, tile), F32),
            pltpu.VMEM((2, DIFF_V_DIM + ROWS_L, tile), F32),
        ],
        compiler_params=_params("parallel", "parallel", "arbitrary"),
        name="diff_attn",
    )(slopes, proj, proj, proj, lp, sg)


def _compress_kernel(kv_ref, pos_ref, w1_ref, w2_ref, o_ref, ot_ref):
    half = w1_ref.shape[0] // 2
    kv = kv_ref[...]
    top = _dot(kv, w1_ref[:half, :])
    bot = _dot(kv, w1_ref[half:, :])
    rows = kv.shape[0]
    bot_next = pltpu.roll(bot, rows - 1, 0)
    pos_term = _dot(pos_ref[...], w1_ref[...])[0:1]
    hid = jax.nn.gelu(top + bot_next + pos_term)
    out = _dot(hid.astype(BF16), w2_ref[...])
    o_ref[...] = out.astype(BF16)
    ot_ref[...] = out.T.astype(BF16)


def _compress(kvr, pos, w1, w2):
    _, groups, batch, rows, width = kvr.shape
    d = w2.shape[-1]
    return pl.pallas_call(
        _compress_kernel,
        grid=(2, groups, batch),
        in_specs=[
            pl.BlockSpec((None, None, None, rows, width), lambda a, g, b: (a, g, b, 0, 0)),
            pl.BlockSpec((None, 8, 2 * width), lambda a, g, b: (a, 0, 0)),
            pl.BlockSpec((None, 2 * width, d), lambda a, g, b: (a, 0, 0)),
            pl.BlockSpec((None, d, d), lambda a, g, b: (a, 0, 0)),
        ],
        out_specs=[
            pl.BlockSpec((None, None, None, rows, d), lambda a, g, b: (a, b, g, 0, 0)),
            pl.BlockSpec((None, None, None, d, rows), lambda a, g, b: (a, b, g, 0, 0)),
        ],
        out_shape=[
            jax.ShapeDtypeStruct((2, batch, groups, rows, d), BF16),
            jax.ShapeDtypeStruct((2, batch, groups, d, rows), BF16),
        ],
        compiler_params=_params("parallel", "parallel", "parallel"),
        name="compress",
    )(kvr, pos, w1, w2)


def _nsa_kernel(slopes_ref, q_ref, gate_ref, kc_ref, vct_ref, ks_ref, vs_ref, kw_ref, vw_ref,
                ovt_ref, o_ref, vst_ref, vwt_ref, base_ref, z_ref, mrow_ref, alpha_ref, m_ref, acc_ref,
                *, tq, tk, seq):
    g = pl.program_id(1)
    qi = pl.program_id(2)
    q0 = qi * tq
    hp = NSA_HPG
    n = hp * tq
    dv = NSA_DIM
    span = WIN_LEN + tq
    n_slc = seq // SLC_LEN
    slope_row = jnp.concatenate(
        [jnp.full((1, tq), slopes_ref[g * hp + h], F32) for h in range(hp)], axis=1)

    @pl.when(qi == 0)
    def _():
        ones_s = jnp.ones((1, tk), F32)
        ones_w = jnp.ones((1, tq), F32)
        for t in range(seq // tk):
            vst_ref[t] = _scaled_vt(vs_ref[t * tk:(t + 1) * tk, :].astype(F32).T, ones_s)
        for t in range(seq // tq):
            vwt_ref[t] = _scaled_vt(vw_ref[t * tq:(t + 1) * tq, :].astype(F32).T, ones_w)
        key_row = lax.broadcasted_iota(jnp.int32, (span, tq), 0).astype(F32)
        for h in range(hp):
            base_ref[h] = slopes_ref[g * hp + h] * key_row

    q = q_ref[...]
    q4t = jnp.concatenate(
        [q[:, h * NSA_DIM:(h + 1) * NSA_DIM].astype(F32).T for h in range(hp)], axis=1).astype(BF16)
    qpos_all = q0 + (lax.broadcasted_iota(jnp.int32, (1, n), 1) & (tq - 1))
    qpos = q0 + lax.broadcasted_iota(jnp.int32, (1, tq), 1)

    def tile_heads(a):
        return jnp.concatenate([a] * hp, axis=1)

    n_cmp = kc_ref.shape[0]
    cmp_end = lax.broadcasted_iota(jnp.int32, (n_cmp, 1), 0) * CMP_STRIDE + (CMP_LEN - 1)
    mask_c = cmp_end <= qpos_all
    z_c = jnp.where(mask_c, _dot(kc_ref[...], q4t), NEG_INF)
    m_c = jnp.max(z_c, axis=0, keepdims=True)
    e_c = jnp.where(mask_c, jnp.exp2(z_c - m_c), 0.0)
    p_c = e_c * (1.0 / jnp.maximum(jnp.sum(e_c, axis=0, keepdims=True), 1e-30))
    o_c = _dot(vct_ref[...], p_c.astype(BF16))

    wt = jnp.maximum(qi - WIN_LEN // tq, 0)
    w0 = pl.multiple_of(wt * tq, tq)
    dist = qpos - (w0 + lax.broadcasted_iota(jnp.int32, (span, 1), 0))
    in_win = (dist >= 0) & (dist < WIN_LEN)
    pen_w = jnp.concatenate([jnp.where(in_win, base_ref[h], NEG_INF) for h in range(hp)], axis=1)
    z_w = _dot(kw_ref[pl.ds(w0, span), :], q4t) + pen_w
    p_w = jnp.exp2(z_w - jnp.max(z_w, axis=0, keepdims=True)).astype(BF16)
    vw_cat = jnp.concatenate([vwt_ref[wt + j] for j in range(span // tq)], axis=1)
    o_w = _split_acc(_dot(vw_cat, p_w), dv)

    p_sum = p_c[:, 0:tq]
    for h in range(1, hp):
        p_sum = p_sum + p_c[:, h * tq:(h + 1) * tq]
    ovt = ovt_ref[...]
    p_hi = p_sum.astype(BF16)
    r1 = p_sum - p_hi.astype(F32)
    p_mid = r1.astype(BF16)
    p_lo = (r1 - p_mid.astype(F32)).astype(BF16)
    imp = _dot(ovt, p_hi) + _dot(ovt, p_mid) + _dot(ovt, p_lo)
    jblk = lax.broadcasted_iota(jnp.int32, (n_slc, tq), 0)
    cur = qpos // SLC_LEN
    forced = (jblk == 0) | (jblk == cur) | (jblk == cur - 1)
    imp = jnp.where(forced, SEL_BIG, jnp.where(jblk > cur, -SEL_BIG, imp))
    jblk_f = jblk.astype(F32)
    sel = jnp.zeros((n_slc, tq), F32)
    for _ in range(min(SLC_TOPK, n_slc)):
        mx = jnp.max(imp, axis=0, keepdims=True)
        first = jnp.min(jnp.where(imp == mx, jblk_f, float(n_slc)), axis=0, keepdims=True)
        hit = jblk_f == first
        sel = jnp.where(hit, 1.0, sel)
        imp = jnp.where(hit, -jnp.inf, imp)
    sel_bf = sel.astype(BF16)

    m_ref[...] = jnp.full_like(m_ref, NEG_INF)
    acc_ref[...] = jnp.zeros_like(acc_ref)
    exp_blk = lax.broadcasted_iota(jnp.int32, (tk, n_slc), 0) // SLC_LEN
    exp_col = lax.broadcasted_iota(jnp.int32, (tk, n_slc), 1)
    krow = lax.broadcasted_iota(jnp.int32, (tk, 1), 0)
    assert hp == 4
    half = 2 * tq

    def tile_off(t, i):
        return slope_row[:, i * half:(i + 1) * half] * (t * tk - q0).astype(F32)

    def penalty(t):
        expand = jnp.where(exp_col == exp_blk + t * (tk // SLC_LEN), 1.0, 0.0).astype(BF16)
        chosen = _dot(expand, sel_bf)
        keep = (chosen > 0.5) & ((krow + t * tk) <= qpos)
        return [jnp.concatenate([jnp.where(keep, base_ref[2 * i + j, 0:tk, :], NEG_INF)
                                 for j in range(2)], axis=1) for i in range(2)]

    def scores(t, i, pen):
        ks = pl.multiple_of(t * tk, tk)
        return _dot(ks_ref[pl.ds(ks, tk), :], q4t[:, i * half:(i + 1) * half]) + pen[i]

    def bookkeep(t, slot, i, z):
        z_ref[slot, i] = z
        m_old = m_ref[i]
        m_new = jnp.maximum(m_old, jnp.max(z, axis=0, keepdims=True) + tile_off(t, i))
        m_ref[i] = m_new
        mrow_ref[slot, i] = m_new
        alpha_ref[slot, i] = jnp.exp2(m_old - m_new)

    def weights(t, slot, i):
        return jnp.exp2(z_ref[slot, i] - (mrow_ref[slot, i] - tile_off(t, i))).astype(BF16)

    def accumulate(t, slot, i, p):
        acc_ref[i] = alpha_ref[slot, i] * acc_ref[i] + _dot(vst_ref[t], p)

    def overlapped(t, slot):
        pen = penalty(t + 1)
        for i in range(2):
            z_next = scores(t + 1, i, pen)
            p = weights(t, slot, i)
            bookkeep(t + 1, 1 - slot, i, z_next)
            accumulate(t, slot, i, p)

    def drain(t, slot):
        for i in range(2):
            accumulate(t, slot, i, weights(t, slot, i))

    pen = penalty(0)
    for i in range(2):
        bookkeep(0, 0, i, scores(0, i, pen))
    n_last = q0 // tk

    def body(u, carry):
        overlapped(2 * u, 0)
        overlapped(2 * u + 1, 1)
        return carry

    lax.fori_loop(0, n_last // 2, body, 0)

    @pl.when(n_last % 2 == 1)
    def _():
        overlapped(n_last - 1, 0)
        drain(n_last, 1)

    @pl.when(n_last % 2 == 0)
    def _():
        drain(n_last, 0)

    o_s = jnp.concatenate([_split_acc(acc_ref[i], dv) for i in range(2)], axis=1)

    def gate_row(branch):
        return jnp.concatenate(
            [gate_ref[pl.ds((g * hp + h) * 3 + branch, 1), :] for h in range(hp)], axis=1)

    ot = gate_row(0) * o_c + gate_row(1) * o_s + gate_row(2) * o_w
    o_ref[...] = jnp.concatenate(
        [ot[:, h * tq:(h + 1) * tq].T for h in range(hp)], axis=1).astype(BF16)


def _nsa_attention(proj, gates_t, k_cmp, vt_cmp, overlap_t, batch, seq):
    t = proj.shape[0]
    tq = _pick(seq, 128)
    tk = _pick(seq, 512)
    assert seq >= WIN_LEN + tq and WIN_LEN % tq == 0 and tk % tq == 0 and tq == SUB
    nq = seq // tq
    hp = NSA_HPG
    qb = C_NQ // (hp * NSA_DIM)
    kvb = C_NKV // LANES
    n_cmp = k_cmp.shape[3]
    slopes2 = _alibi_slopes(NSA_HEADS) * np.float32(LOG2E)
    _assert_slopes_fit(slopes2)
    slopes = jnp.asarray(slopes2)

    def kv_spec(branch, which):
        base = kvb + (branch * 2 + which) * NSA_GROUPS
        return pl.BlockSpec((seq, LANES), lambda b, g, i: (b, base + g))

    return pl.pallas_call(
        functools.partial(_nsa_kernel, tq=tq, tk=tk, seq=seq),
        grid=(batch, NSA_GROUPS, nq),
        in_specs=[
            pl.BlockSpec(memory_space=pltpu.SMEM),
            pl.BlockSpec((tq, hp * NSA_DIM), lambda b, g, i: (b * nq + i, qb + g)),
            pl.BlockSpec((LANES, tq), lambda b, g, i: (0, b * nq + i)),
            pl.BlockSpec((None, None, None, n_cmp, NSA_DIM), lambda b, g, i: (0, b, g, 0, 0)),
            pl.BlockSpec((None, None, None, NSA_DIM, n_cmp), lambda b, g, i: (1, b, g, 0, 0)),
            kv_spec(1, 0), kv_spec(1, 1),
            kv_spec(2, 0), kv_spec(2, 1),
            pl.BlockSpec(overlap_t.shape, lambda b, g, i: (0, 0)),
        ],
        out_specs=pl.BlockSpec((tq, hp * NSA_DIM), lambda b, g, i: (b * nq + i, g)),
        out_shape=jax.ShapeDtypeStruct((t, NSA_HEADS * NSA_DIM), BF16),
        scratch_shapes=[
            pltpu.VMEM((seq // tk, NSA_DIM + ROWS_L, tk), BF16),
            pltpu.VMEM((seq // tq, NSA_DIM + ROWS_L, tq), BF16),
            pltpu.VMEM((hp, WIN_LEN + tq, tq), F32),
            pltpu.VMEM((2, 2, tk, 2 * tq), F32),
            pltpu.VMEM((2, 2, 1, 2 * tq), F32),
            pltpu.VMEM((2, 2, 1, 2 * tq), F32),
            pltpu.VMEM((2, 1, 2 * tq), F32),
            pltpu.VMEM((2, NSA_DIM + ROWS_L, 2 * tq), F32),
        ],
        compiler_params=_params("parallel", "parallel", "arbitrary"),
        name="nsa_attn",
    )(slopes, proj, gates_t, k_cmp, vt_cmp, proj, proj, proj, proj, overlap_t)


def _memkv_kernel(mem_ref, g_ref, w_ref, o_ref):
    mn = _rmsnorm(mem_ref[...], g_ref[...]).astype(BF16)
    o_ref[...] = _dot(mn, w_ref[...]).astype(BF16)


def _memkv(mem, g, w):
    r, d = mem.shape
    n = w.shape[1]
    tm = _pick(r, 256)
    return pl.pallas_call(
        _memkv_kernel,
        grid=(r // tm,),
        in_specs=[
            pl.BlockSpec((tm, d), lambda i: (i, 0)),
            pl.BlockSpec((1, d), lambda i: (0, 0)),
            pl.BlockSpec((d, n), lambda i: (0, 0)),
        ],
        out_specs=pl.BlockSpec((tm, n), lambda i: (i, 0)),
        out_shape=jax.ShapeDtypeStruct((r, n), BF16),
        compiler_params=_params("parallel"),
        name="mem_kv",
    )(mem, g, w)


def _memattn_kernel(q_ref, k_ref, v_ref, o_ref):
    scale = MEM_DIM ** -0.5
    q = q_ref[...]
    k = k_ref[...]
    v = v_ref[...]
    outs = []
    for h in range(MEM_HEADS):
        sl = slice(h * MEM_DIM, (h + 1) * MEM_DIM)
        s = _dot_nt(q[:, sl], k[:, sl]) * scale
        m = jnp.max(s, axis=-1, keepdims=True)
        e = jnp.exp(s - m)
        p = e / jnp.sum(e, axis=-1, keepdims=True)
        outs.append(_dot(p.astype(BF16), v[:, sl]))
    o_ref[...] = jnp.concatenate(outs, axis=-1).astype(BF16)


def _mem_attention(proj, memkv, batch, seq, mem_len):
    t = proj.shape[0]
    tq = _pick(seq, 512)
    nq = seq // tq
    width = MEM_HEADS * MEM_DIM
    qb = P_MQ // width
    return pl.pallas_call(
        _memattn_kernel,
        grid=(batch, nq),
        in_specs=[
            pl.BlockSpec((tq, width), lambda b, i: (b * nq + i, qb)),
            pl.BlockSpec((mem_len, width), lambda b, i: (b, 0)),
            pl.BlockSpec((mem_len, width), lambda b, i: (b, 1)),
        ],
        out_specs=pl.BlockSpec((tq, width), lambda b, i: (b * nq + i, 0)),
        out_shape=jax.ShapeDtypeStruct((t, width), BF16),
        compiler_params=_params("parallel", "parallel"),
        name="mem_attn",
    )(proj, memkv, memkv)


def _merge_kernel(x_ref, hn_ref, od_ref, on_ref, om_ref, wgd_ref, wgn_ref, wgm_ref,
                  wud_ref, wun_ref, wum_ref, wo_ref, o_ref, acc_ref):
    j = pl.program_id(1)

    @pl.when(j == 0)
    def _():
        acc_ref[...] = jnp.zeros_like(acc_ref)

    hn = hn_ref[...]
    merged = jax.nn.sigmoid(_dot_nt(hn, wgd_ref[...])) * _dot(od_ref[...], wud_ref[...])
    merged += jax.nn.sigmoid(_dot_nt(hn, wgn_ref[...])) * _dot(on_ref[...], wun_ref[...])
    merged += jax.nn.sigmoid(_dot_nt(hn, wgm_ref[...])) * _dot(om_ref[...], wum_ref[...])
    acc_ref[...] += _dot(merged.astype(BF16), wo_ref[...])

    @pl.when(j == pl.num_programs(1) - 1)
    def _():
        o_ref[...] = x_ref[...] + acc_ref[...]


def _merge(x, hn, od, on, om, wg, wud, wun, wum, wo):
    t, d = x.shape
    tm = _pick(t, 512)
    tj = _pick(d, 512)
    nj = d // tj

    def gate_spec(branch):
        return pl.BlockSpec((tj, d), lambda i, j: (branch * nj + j, 0))

    return pl.pallas_call(
        _merge_kernel,
        grid=(t // tm, nj),
        in_specs=[
            pl.BlockSpec((tm, d), lambda i, j: (i, 0)),
            pl.BlockSpec((tm, d), lambda i, j: (i, 0)),
            pl.BlockSpec((tm, od.shape[1]), lambda i, j: (i, 0)),
            pl.BlockSpec((tm, on.shape[1]), lambda i, j: (i, 0)),
            pl.BlockSpec((tm, om.shape[1]), lambda i, j: (i, 0)),
            gate_spec(0), gate_spec(1), gate_spec(2),
            pl.BlockSpec((wud.shape[0], tj), lambda i, j: (0, j)),
            pl.BlockSpec((wun.shape[0], tj), lambda i, j: (0, j)),
            pl.BlockSpec((wum.shape[0], tj), lambda i, j: (0, j)),
            pl.BlockSpec((tj, d), lambda i, j: (j, 0)),
        ],
        out_specs=pl.BlockSpec((tm, d), lambda i, j: (i, 0)),
        out_shape=jax.ShapeDtypeStruct((t, d), F32),
        scratch_shapes=[pltpu.VMEM((tm, d), F32)],
        compiler_params=_params("parallel", "arbitrary"),
        name="merge",
    )(x, hn, od, on, om, wg, wg, wg, wud, wun, wum, wo)


def _overlap_matrix_t(n_slc, n_cmp):
    cmp_start = np.arange(n_cmp)[None, :] * CMP_STRIDE
    slc_start = np.arange(n_slc)[:, None] * SLC_LEN
    return ((cmp_start < slc_start + SLC_LEN) & (cmp_start + CMP_LEN > slc_start)).astype(np.float32)


def kernel(x, mem, ffn1_norm, ffn1_w_gate, ffn1_w_up, ffn1_w_down, mix_norm, w_in, diff_lambda, diff_subln, nsa_cmp_pos, nsa_cmp_w1, nsa_cmp_w2, mem_norm, w_mem_kv, w_up_diff, w_up_nsa, w_up_mem, w_out, ffn2_norm, ffn2_w_gate, ffn2_w_up, ffn2_w_down, final_norm):
    batch, seq, d = x.shape
    mem_len = mem.shape[1]
    depth = w_in.shape[0]
    t = batch * seq
    assert seq % CMP_STRIDE == 0 and CMP_LEN == 2 * CMP_STRIDE
    n_rows = seq // CMP_STRIDE
    n_slc = seq // SLC_LEN
    overlap_t = jnp.asarray(_overlap_matrix_t(n_slc, n_rows), BF16)

    col_scale = np.ones((1, P_COLS), np.float32)
    col_scale[0, C_DQ:C_DK] = (DIFF_QK_DIM ** -0.5) * LOG2E
    col_scale[0, C_NQ:C_NKV] = (NSA_DIM ** -0.5) * LOG2E
    col_scale = jnp.asarray(col_scale)

    w_in_t = jnp.swapaxes(w_in, 1, 2)

    xt = x.reshape(t, d)
    memt = mem.reshape(batch * mem_len, d)
    fg = final_norm.reshape(1, d)
    bf = lambda a: a.astype(BF16)

    for l in range(depth):
        lam_init = 0.8 - 0.6 * math.exp(-0.3 * l)
        xt, hn = _ffn(xt, ffn1_norm[l].reshape(1, d), ffn1_w_gate, ffn1_w_up, ffn1_w_down,
                      mix_norm[l].reshape(1, d), l, "normed")

        w = w_in_t[l]
        w_mq = bf(w[C_MQ:C_MG])
        w_ng = bf(jnp.pad(w[C_NG:C_MQ], ((0, LANES - NSA_GATE_COLS), (0, 0))))
        proj, ngt = _inproj(hn, w_in_t, l, w_mq, w_ng, col_scale)

        o_diff = _diff_attention(proj, diff_lambda[l], diff_subln[l].reshape(1, DIFF_V_DIM),
                                 batch, seq, lam_init)

        ckv = proj[:, C_NKV:C_NKV + 2 * NSA_GROUPS * NSA_DIM]
        ckv = ckv.reshape(batch, n_rows, CMP_STRIDE, 2, NSA_GROUPS, NSA_DIM)
        ckv = ckv.transpose(3, 4, 0, 1, 2, 5).reshape(2, NSA_GROUPS, batch, n_rows, CMP_STRIDE * NSA_DIM)
        pos = bf(jnp.broadcast_to(nsa_cmp_pos[l].reshape(2, 1, CMP_LEN * NSA_DIM), (2, 8, CMP_LEN * NSA_DIM)))
        kv_cmp, kvt_cmp = _compress(ckv, pos, bf(nsa_cmp_w1[l]), bf(nsa_cmp_w2[l]))
        o_nsa = _nsa_attention(proj, ngt, kv_cmp, kvt_cmp, overlap_t, batch, seq)

        memkv = _memkv(memt, mem_norm[l].reshape(1, d), bf(w_mem_kv[l]))
        o_mem = _mem_attention(proj, memkv, batch, seq, mem_len)

        w_mg = bf(w[C_MG:])
        xt = _merge(xt, hn, o_diff, o_nsa, o_mem, w_mg, bf(w_up_diff[l]), bf(w_up_nsa[l]),
                    bf(w_up_mem[l]), bf(w_out[l]))

        (xt,) = _ffn(xt, ffn2_norm[l].reshape(1, d), ffn2_w_gate, ffn2_w_up, ffn2_w_down, fg, l,
                     "final" if l == depth - 1 else "plain")
    return xt.reshape(batch, seq, d)
```

```python
import functools
import math

import jax
import jax.numpy as jnp
import numpy as np
from jax import lax
from jax.experimental import pallas as pl
from jax.experimental.pallas import tpu as pltpu

F32 = jnp.float32
BF16 = jnp.bfloat16

NORM_EPS = 1e-6
NEG_INF = -1e30
SEL_BIG = 1e9
LOG2E = math.log2(math.e)

DIFF_HEADS = 8
DIFF_QK_DIM = 64
DIFF_V_DIM = 2 * DIFF_QK_DIM

NSA_HEADS = 8
NSA_GROUPS = 2
NSA_HPG = NSA_HEADS // NSA_GROUPS
NSA_DIM = 128
CMP_LEN = 32
CMP_STRIDE = 16
SLC_LEN = 64
SLC_TOPK = 8
WIN_LEN = 512

MEM_HEADS = 4
MEM_DIM = 128
N_BRANCH = 3

LANES = 128
VMEM_LIMIT = 56 * 1024 * 1024

DIFF_QK_COLS = DIFF_HEADS * 2 * DIFF_QK_DIM
DIFF_V_COLS = DIFF_HEADS * DIFF_V_DIM
NSA_Q_COLS = NSA_HEADS * NSA_DIM
NSA_KV_COLS = 3 * 2 * NSA_GROUPS * NSA_DIM
NSA_GATE_COLS = NSA_HEADS * 3
MEM_Q_COLS = MEM_HEADS * MEM_DIM
C_DQ = 0
C_DK = C_DQ + DIFF_QK_COLS
C_DV = C_DK + DIFF_QK_COLS
C_NQ = C_DV + DIFF_V_COLS
C_NKV = C_NQ + NSA_Q_COLS
C_NG = C_NKV + NSA_KV_COLS
C_MQ = C_NG + NSA_GATE_COLS
C_MG = C_MQ + MEM_Q_COLS
P_MQ = C_NG
P_COLS = P_MQ + MEM_Q_COLS


def _alibi_slopes(n_heads):
    return np.array([2.0 ** (-8.0 * (h + 1) / n_heads) for h in range(n_heads)], dtype=np.float32)


def _rmsnorm(x, g):
    y = x * lax.rsqrt(jnp.mean(x * x, axis=-1, keepdims=True) + NORM_EPS)
    return y * g


def _dot(a, b):
    return jnp.dot(a, b, preferred_element_type=F32)


def _dot_nt(a, b):
    return lax.dot_general(a, b, (((1,), (1,)), ((), ())), preferred_element_type=F32)


def _transpose_bf16(x):
    return x.astype(F32).T.astype(BF16)


def _params(*sem):
    return pltpu.CompilerParams(dimension_semantics=sem, vmem_limit_bytes=VMEM_LIMIT)


def _pick(n, pref):
    t = min(n, pref)
    assert n % t == 0, (n, t)
    return t


def _ffn_kernel(x_ref, g_ref, wg_ref, wu_ref, wd_ref, fg_ref, o_ref, *rest, tail):
    xn_ref = rest[-1]
    j = pl.program_id(1)

    @pl.when(j == 0)
    def _():
        xn_ref[...] = _rmsnorm(x_ref[...], g_ref[...]).astype(BF16)
        o_ref[...] = jnp.zeros_like(o_ref)

    xn = xn_ref[...]
    a = _dot(xn, wg_ref[...].astype(BF16))
    u = _dot(xn, wu_ref[...].astype(BF16))
    h = (a * jax.nn.sigmoid(a)) * u
    o_ref[...] += _dot(h.astype(BF16), wd_ref[...].astype(BF16))

    @pl.when(j == pl.num_programs(1) - 1)
    def _():
        rows = 128

        def chunk(c, carry):
            r = pl.ds(pl.multiple_of(c * rows, rows), rows)
            y = x_ref[r, :] + 0.5 * o_ref[r, :]
            if tail == "final":
                y = _rmsnorm(y, fg_ref[...])
            if tail == "normed":
                rest[0][r, :] = _rmsnorm(y, fg_ref[...]).astype(BF16)
            o_ref[r, :] = y
            return carry

        lax.fori_loop(0, o_ref.shape[0] // rows, chunk, 0)


def _ffn(x, g, wg, wu, wd, fg, layer, tail):
    t, d = x.shape
    f = wg.shape[2]
    tm = _pick(t, 1024)
    tf = _pick(f, 256)
    out_specs = [pl.BlockSpec((tm, d), lambda i, j: (i, 0))]
    out_shape = [jax.ShapeDtypeStruct((t, d), F32)]
    if tail == "normed":
        out_specs.append(pl.BlockSpec((tm, d), lambda i, j: (i, 0)))
        out_shape.append(jax.ShapeDtypeStruct((t, d), BF16))
    return pl.pallas_call(
        functools.partial(_ffn_kernel, tail=tail),
        grid=(t // tm, f // tf),
        in_specs=[
            pl.BlockSpec((tm, d), lambda i, j: (i, 0),
                         pipeline_mode=pl.Buffered(1 if tail == "normed" else 2)),
            pl.BlockSpec((1, d), lambda i, j: (0, 0)),
            pl.BlockSpec((None, d, tf), lambda i, j: (layer, 0, j)),
            pl.BlockSpec((None, d, tf), lambda i, j: (layer, 0, j)),
            pl.BlockSpec((None, tf, d), lambda i, j: (layer, j, 0)),
            pl.BlockSpec((1, d), lambda i, j: (0, 0)),
        ],
        out_specs=out_specs,
        out_shape=out_shape,
        scratch_shapes=[] if tail == "normed" else [pltpu.VMEM((tm, d), BF16)],
        compiler_params=_params("parallel", "arbitrary"),
        name="ffn",
    )(x, g, wg, wu, wd, fg)


def _inproj_kernel(hn_ref, w_ref, wmq_ref, wng_ref, cs_ref, proj_ref, ngt_ref, *, n_main):
    j = pl.program_id(1)

    @pl.when(j == 0)
    def _():
        ngt_ref[...] = jax.nn.sigmoid(_dot_nt(wng_ref[...], hn_ref[...]))

    @pl.when(j < n_main)
    def _():
        proj_ref[...] = (_dot_nt(hn_ref[...], w_ref[...].astype(BF16)) * cs_ref[...]).astype(BF16)

    @pl.when(j >= n_main)
    def _():
        proj_ref[...] = (_dot_nt(hn_ref[...], wmq_ref[...]) * cs_ref[...]).astype(BF16)


def _inproj(hn, w_in_t, layer, wmq, wng, cs):
    t, d = hn.shape
    n = P_COLS
    tm = _pick(t, 2048)
    tn = MEM_Q_COLS
    assert P_MQ % tn == 0 and wmq.shape == (tn, d)
    n_main = P_MQ // tn
    return pl.pallas_call(
        functools.partial(_inproj_kernel, n_main=n_main),
        grid=(t // tm, n // tn),
        in_specs=[
            pl.BlockSpec((tm, d), lambda i, j: (i, 0)),
            pl.BlockSpec((None, tn, d), lambda i, j: (layer, jnp.minimum(j, n_main - 1), 0)),
            pl.BlockSpec((tn, d), lambda i, j: (0, 0)),
            pl.BlockSpec((LANES, d), lambda i, j: (0, 0)),
            pl.BlockSpec((1, tn), lambda i, j: (0, j)),
        ],
        out_specs=[
            pl.BlockSpec((tm, tn), lambda i, j: (i, j)),
            pl.BlockSpec((LANES, tm), lambda i, j: (0, i)),
        ],
        out_shape=[
            jax.ShapeDtypeStruct((t, n), BF16),
            jax.ShapeDtypeStruct((LANES, t), F32),
        ],
        compiler_params=_params("parallel", "arbitrary"),
        name="inproj",
    )(hn, w_in_t, wmq, wng, cs)


SUB = 128
ROWS_L = 16


def _assert_slopes_fit(slopes2):
    assert float(np.max(slopes2)) * SUB < 100.0, "ALiBi factor per bias block leaves the safe f32 range"


def _bias_factor_row(slope2, width):
    r = lax.broadcasted_iota(jnp.int32, (1, width), 1) & (SUB - 1)
    return jnp.exp2(slope2 * r.astype(F32))


def _scaled_vt(vt, w_row):
    ones = jnp.broadcast_to(w_row, (ROWS_L, w_row.shape[1]))
    return jnp.concatenate([vt * w_row, ones], axis=0).astype(BF16)


def _block_weights(z, off0, step, m_prev):
    nb = z.shape[0] // SUB
    blocks = [z[j * SUB:(j + 1) * SUB] for j in range(nb)]
    offs = [off0 + j * step for j in range(nb)]
    m_new = m_prev
    for zb, off in zip(blocks, offs):
        m_new = jnp.maximum(m_new, jnp.max(zb, axis=0, keepdims=True) + off)
    p = [jnp.exp2(zb - (m_new - off)).astype(BF16) for zb, off in zip(blocks, offs)]
    return m_new, jnp.concatenate(p, axis=0)


def _split_acc(acc, dv):
    return acc[0:dv] * (1.0 / jnp.maximum(acc[dv:dv + 1], 1e-30))


def _diff_kernel(slopes_ref, q_ref, k_ref, v_ref, lp_ref, sg_ref, o_ref,
                 vt_ref, pen_ref, z_ref, mrow_ref, alpha_ref, m_ref, acc_ref, *, tile, n_tiles, lam_init):
    h = pl.program_id(1)
    qi = pl.program_id(2)
    slope2 = slopes_ref[h]
    nb = tile // SUB

    @pl.when(qi == 0)
    def _():
        w_row = _bias_factor_row(slope2, tile)
        for t in range(n_tiles):
            vt_ref[t] = _scaled_vt(v_ref[t * tile:(t + 1) * tile, :].astype(F32).T, w_row)
        row = lax.broadcasted_iota(jnp.int32, (tile, tile), 0)
        col = lax.broadcasted_iota(jnp.int32, (tile, tile), 1)
        pen_ref[0] = jnp.zeros((tile, tile), F32)
        pen_ref[1] = jnp.where(row <= col, 0.0, NEG_INF)

    qt = _transpose_bf16(q_ref[...])
    feat = lax.broadcasted_iota(jnp.int32, qt.shape, 0)
    qts = [jnp.where((feat >= m * DIFF_QK_DIM) & (feat < (m + 1) * DIFF_QK_DIM), qt, 0.0).astype(BF16)
           for m in range(2)]

    m_ref[...] = jnp.full_like(m_ref, NEG_INF)
    acc_ref[...] = jnp.zeros_like(acc_ref)
    step = slope2 * float(SUB)

    def block_off(t, j):
        return slope2 * ((t - qi) * tile).astype(F32) + j * step

    def scores(t, m):
        ks = pl.multiple_of(t * tile, tile)
        return _dot(k_ref[pl.ds(ks, tile), :], qts[m]) + pen_ref[(t == qi).astype(jnp.int32)]

    def bookkeep(t, slot, m, z):
        z_ref[slot, m] = z
        m_old = m_ref[m]
        m_new = m_old
        for j in range(nb):
            m_new = jnp.maximum(m_new, jnp.max(z[j * SUB:(j + 1) * SUB], axis=0, keepdims=True)
                                + block_off(t, j))
        m_ref[m] = m_new
        mrow_ref[slot, m] = m_new
        alpha_ref[slot, m] = jnp.exp2(m_old - m_new)

    def weights(t, slot, m):
        m_new = mrow_ref[slot, m]
        return jnp.concatenate(
            [jnp.exp2(z_ref[slot, m, j * SUB:(j + 1) * SUB, :] - (m_new - block_off(t, j))).astype(BF16)
             for j in range(nb)], axis=0)

    def accumulate(t, slot, m, p):
        acc_ref[m] = alpha_ref[slot, m] * acc_ref[m] + _dot(vt_ref[t], p)

    def overlapped(t, slot):
        for m in range(2):
            z_next = scores(t + 1, m)
            p = weights(t, slot, m)
            bookkeep(t + 1, 1 - slot, m, z_next)
            accumulate(t, slot, m, p)

    def drain(t, slot):
        for m in range(2):
            accumulate(t, slot, m, weights(t, slot, m))

    for m in range(2):
        bookkeep(0, 0, m, scores(0, m))

    def body(u, carry):
        overlapped(2 * u, 0)
        overlapped(2 * u + 1, 1)
        return carry

    lax.fori_loop(0, qi // 2, body, 0)

    @pl.when(qi % 2 == 1)
    def _():
        overlapped(qi - 1, 0)
        drain(qi, 1)

    @pl.when(qi % 2 == 0)
    def _():
        drain(qi, 0)

    o1 = _split_acc(acc_ref[0], DIFF_V_DIM)
    o2 = _split_acc(acc_ref[1], DIFF_V_DIM)
    lp = lp_ref[...]
    lam = (jnp.exp(jnp.sum(lp[0:1] * lp[1:2], axis=-1, keepdims=True))
           - jnp.exp(jnp.sum(lp[2:3] * lp[3:4], axis=-1, keepdims=True)) + lam_init)
    ot = o1 - lam * o2
    ot = ot * lax.rsqrt(jnp.mean(ot * ot, axis=0, keepdims=True) + NORM_EPS)
    o_ref[...] = (ot.T * sg_ref[...] * (1.0 - lam_init)).astype(BF16)


def _diff_attention(proj, lp, sg, batch, seq, lam_init):
    t = proj.shape[0]
    tile = _pick(seq, 512)
    nq = seq // tile
    kb, vb = C_DK // LANES, C_DV // LANES
    slopes2 = _alibi_slopes(DIFF_HEADS) * np.float32(LOG2E)
    _assert_slopes_fit(slopes2)
    slopes = jnp.asarray(slopes2)
    assert tile % SUB == 0
    return pl.pallas_call(
        functools.partial(_diff_kernel, tile=tile, n_tiles=nq, lam_init=lam_init),
        grid=(batch, DIFF_HEADS, nq),
        in_specs=[
            pl.BlockSpec(memory_space=pltpu.SMEM),
            pl.BlockSpec((tile, LANES), lambda b, h, i: (b * nq + i, h)),
            pl.BlockSpec((seq, LANES), lambda b, h, i: (b, kb + h)),
            pl.BlockSpec((seq, LANES), lambda b, h, i: (b, vb + h)),
            pl.BlockSpec((4, DIFF_QK_DIM), lambda b, h, i: (0, 0)),
            pl.BlockSpec((1, DIFF_V_DIM), lambda b, h, i: (0, 0)),
        ],
        out_specs=pl.BlockSpec((tile, LANES), lambda b, h, i: (b * nq + i, h)),
        out_shape=jax.ShapeDtypeStruct((t, DIFF_HEADS * DIFF_V_DIM), BF16),
        scratch_shapes=[
            pltpu.VMEM((nq, DIFF_V_DIM + ROWS_L, tile), BF16),
            pltpu.VMEM((2, tile, tile), F32),
            pltpu.VMEM((2, 2, tile, tile), F32),
            pltpu.VMEM((2, 2, 1, tile), F32),
            pltpu.VMEM((2, 2, 1, tile), F32),
            pltpu.VMEM((2, 1, tile), F32),
            pltpu.VMEM((2, DIFF_V_DIM + ROWS_L, tile), F32),
        ],
        compiler_params=_params("parallel", "parallel", "arbitrary"),
        name="diff_attn",
    )(slopes, proj, proj, proj, lp, sg)


def _compress_kernel(kv_ref, pos_ref, w1_ref, w2_ref, o_ref, ot_ref):
    half = w1_ref.shape[0] // 2
    kv = kv_ref[...]
    top = _dot(kv, w1_ref[:half, :])
    bot = _dot(kv, w1_ref[half:, :])
    rows = kv.shape[0]
    bot_next = pltpu.roll(bot, rows - 1, 0)
    pos_term = _dot(pos_ref[...], w1_ref[...])[0:1]
    hid = jax.nn.gelu(top + bot_next + pos_term)
    out = _dot(hid.astype(BF16), w2_ref[...])
    o_ref[...] = out.astype(BF16)
    ot_ref[...] = out.T.astype(BF16)


def _compress(kvr, pos, w1, w2):
    _, groups, batch, rows, width = kvr.shape
    d = w2.shape[-1]
    return pl.pallas_call(
        _compress_kernel,
        grid=(2, groups, batch),
        in_specs=[
            pl.BlockSpec((None, None, None, rows, width), lambda a, g, b: (a, g, b, 0, 0)),
            pl.BlockSpec((None, 8, 2 * width), lambda a, g, b: (a, 0, 0)),
            pl.BlockSpec((None, 2 * width, d), lambda a, g, b: (a, 0, 0)),
            pl.BlockSpec((None, d, d), lambda a, g, b: (a, 0, 0)),
        ],
        out_specs=[
            pl.BlockSpec((None, None, None, rows, d), lambda a, g, b: (a, b, g, 0, 0)),
            pl.BlockSpec((None, None, None, d, rows), lambda a, g, b: (a, b, g, 0, 0)),
        ],
        out_shape=[
            jax.ShapeDtypeStruct((2, batch, groups, rows, d), BF16),
            jax.ShapeDtypeStruct((2, batch, groups, d, rows), BF16),
        ],
        compiler_params=_params("parallel", "parallel", "parallel"),
        name="compress",
    )(kvr, pos, w1, w2)


def _nsa_kernel(slopes_ref, q_ref, gate_ref, kc_ref, vct_ref, ks_ref, vs_ref, kw_ref, vw_ref,
                ovt_ref, o_ref, vst_ref, vwt_ref, base_ref, z_ref, mrow_ref, alpha_ref, m_ref, acc_ref,
                *, tq, tk, seq):
    g = pl.program_id(1)
    qi = pl.program_id(2)
    q0 = qi * tq
    hp = NSA_HPG
    n = hp * tq
    dv = NSA_DIM
    span = WIN_LEN + tq
    n_slc = seq // SLC_LEN
    slope_row = jnp.concatenate(
        [jnp.full((1, tq), slopes_ref[g * hp + h], F32) for h in range(hp)], axis=1)

    @pl.when(qi == 0)
    def _():
        ones_s = jnp.ones((1, tk), F32)
        ones_w = jnp.ones((1, tq), F32)
        for t in range(seq // tk):
            vst_ref[t] = _scaled_vt(vs_ref[t * tk:(t + 1) * tk, :].astype(F32).T, ones_s)
        for t in range(seq // tq):
            vwt_ref[t] = _scaled_vt(vw_ref[t * tq:(t + 1) * tq, :].astype(F32).T, ones_w)
        key_row = lax.broadcasted_iota(jnp.int32, (span, tq), 0).astype(F32)
        for h in range(hp):
            base_ref[h] = slopes_ref[g * hp + h] * key_row

    q = q_ref[...]
    q4t = jnp.concatenate(
        [q[:, h * NSA_DIM:(h + 1) * NSA_DIM].astype(F32).T for h in range(hp)], axis=1).astype(BF16)
    qpos_all = q0 + (lax.broadcasted_iota(jnp.int32, (1, n), 1) & (tq - 1))
    qpos = q0 + lax.broadcasted_iota(jnp.int32, (1, tq), 1)

    def tile_heads(a):
        return jnp.concatenate([a] * hp, axis=1)

    n_cmp = kc_ref.shape[0]
    cmp_end = lax.broadcasted_iota(jnp.int32, (n_cmp, 1), 0) * CMP_STRIDE + (CMP_LEN - 1)
    mask_c = cmp_end <= qpos_all
    z_c = jnp.where(mask_c, _dot(kc_ref[...], q4t), NEG_INF)
    m_c = jnp.max(z_c, axis=0, keepdims=True)
    e_c = jnp.where(mask_c, jnp.exp2(z_c - m_c), 0.0)
    p_c = e_c * (1.0 / jnp.maximum(jnp.sum(e_c, axis=0, keepdims=True), 1e-30))
    o_c = _dot(vct_ref[...], p_c.astype(BF16))

    wt = jnp.maximum(qi - WIN_LEN // tq, 0)
    w0 = pl.multiple_of(wt * tq, tq)
    dist = qpos - (w0 + lax.broadcasted_iota(jnp.int32, (span, 1), 0))
    in_win = (dist >= 0) & (dist < WIN_LEN)
    pen_w = jnp.concatenate([jnp.where(in_win, base_ref[h], NEG_INF) for h in range(hp)], axis=1)
    z_w = _dot(kw_ref[pl.ds(w0, span), :], q4t) + pen_w
    p_w = jnp.exp2(z_w - jnp.max(z_w, axis=0, keepdims=True)).astype(BF16)
    vw_cat = jnp.concatenate([vwt_ref[wt + j] for j in range(span // tq)], axis=1)
    o_w = _split_acc(_dot(vw_cat, p_w), dv)

    p_sum = p_c[:, 0:tq]
    for h in range(1, hp):
        p_sum = p_sum + p_c[:, h * tq:(h + 1) * tq]
    ovt = ovt_ref[...]
    p_hi = p_sum.astype(BF16)
    r1 = p_sum - p_hi.astype(F32)
    p_mid = r1.astype(BF16)
    p_lo = (r1 - p_mid.astype(F32)).astype(BF16)
    imp = _dot(ovt, p_hi) + _dot(ovt, p_mid) + _dot(ovt, p_lo)
    jblk = lax.broadcasted_iota(jnp.int32, (n_slc, tq), 0)
    cur = qpos // SLC_LEN
    forced = (jblk == 0) | (jblk == cur) | (jblk == cur - 1)
    imp = jnp.where(forced, SEL_BIG, jnp.where(jblk > cur, -SEL_BIG, imp))
    jblk_f = jblk.astype(F32)
    sel = jnp.zeros((n_slc, tq), F32)
    for _ in range(min(SLC_TOPK, n_slc)):
        mx = jnp.max(imp, axis=0, keepdims=True)
        first = jnp.min(jnp.where(imp == mx, jblk_f, float(n_slc)), axis=0, keepdims=True)
        hit = jblk_f == first
        sel = jnp.where(hit, 1.0, sel)
        imp = jnp.where(hit, -jnp.inf, imp)
    sel_bf = sel.astype(BF16)

    m_ref[...] = jnp.full_like(m_ref, NEG_INF)
    acc_ref[...] = jnp.zeros_like(acc_ref)
    exp_blk = lax.broadcasted_iota(jnp.int32, (tk, n_slc), 0) // SLC_LEN
    exp_col = lax.broadcasted_iota(jnp.int32, (tk, n_slc), 1)
    krow = lax.broadcasted_iota(jnp.int32, (tk, 1), 0)
    assert hp == 4
    half = 2 * tq

    def tile_off(t, i):
        return slope_row[:, i * half:(i + 1) * half] * (t * tk - q0).astype(F32)

    def penalty(t):
        expand = jnp.where(exp_col == exp_blk + t * (tk // SLC_LEN), 1.0, 0.0).astype(BF16)
        chosen = _dot(expand, sel_bf)
        keep = (chosen > 0.5) & ((krow + t * tk) <= qpos)
        return [jnp.concatenate([jnp.where(keep, base_ref[2 * i + j, 0:tk, :], NEG_INF)
                                 for j in range(2)], axis=1) for i in range(2)]

    def scores(t, i, pen):
        ks = pl.multiple_of(t * tk, tk)
        return _dot(ks_ref[pl.ds(ks, tk), :], q4t[:, i * half:(i + 1) * half]) + pen[i]

    def bookkeep(t, slot, i, z):
        z_ref[slot, i] = z
        m_old = m_ref[i]
        m_new = jnp.maximum(m_old, jnp.max(z, axis=0, keepdims=True) + tile_off(t, i))
        m_ref[i] = m_new
        mrow_ref[slot, i] = m_new
        alpha_ref[slot, i] = jnp.exp2(m_old - m_new)

    def weights(t, slot, i):
        return jnp.exp2(z_ref[slot, i] - (mrow_ref[slot, i] - tile_off(t, i))).astype(BF16)

    def accumulate(t, slot, i, p):
        acc_ref[i] = alpha_ref[slot, i] * acc_ref[i] + _dot(vst_ref[t], p)

    def overlapped(t, slot):
        pen = penalty(t + 1)
        for i in range(2):
            z_next = scores(t + 1, i, pen)
            p = weights(t, slot, i)
            bookkeep(t + 1, 1 - slot, i, z_next)
            accumulate(t, slot, i, p)

    def drain(t, slot):
        for i in range(2):
            accumulate(t, slot, i, weights(t, slot, i))

    pen = penalty(0)
    for i in range(2):
        bookkeep(0, 0, i, scores(0, i, pen))
    n_last = q0 // tk

    def body(u, carry):
        overlapped(2 * u, 0)
        overlapped(2 * u + 1, 1)
        return carry

    lax.fori_loop(0, n_last // 2, body, 0)

    @pl.when(n_last % 2 == 1)
    def _():
        overlapped(n_last - 1, 0)
        drain(n_last, 1)

    @pl.when(n_last % 2 == 0)
    def _():
        drain(n_last, 0)

    o_s = jnp.concatenate([_split_acc(acc_ref[i], dv) for i in range(2)], axis=1)

    def gate_row(branch):
        return jnp.concatenate(
            [gate_ref[pl.ds((g * hp + h) * 3 + branch, 1), :] for h in range(hp)], axis=1)

    ot = gate_row(0) * o_c + gate_row(1) * o_s + gate_row(2) * o_w
    o_ref[...] = jnp.concatenate(
        [ot[:, h * tq:(h + 1) * tq].T for h in range(hp)], axis=1).astype(BF16)


def _nsa_attention(proj, gates_t, k_cmp, vt_cmp, overlap_t, batch, seq):
    t = proj.shape[0]
    tq = _pick(seq, 256)
    tk = _pick(seq, 512)
    assert seq >= WIN_LEN + tq and WIN_LEN % tq == 0 and tk % tq == 0
    nq = seq // tq
    hp = NSA_HPG
    qb = C_NQ // (hp * NSA_DIM)
    kvb = C_NKV // LANES
    n_cmp = k_cmp.shape[3]
    slopes2 = _alibi_slopes(NSA_HEADS) * np.float32(LOG2E)
    _assert_slopes_fit(slopes2)
    slopes = jnp.asarray(slopes2)

    def kv_spec(branch, which):
        base = kvb + (branch * 2 + which) * NSA_GROUPS
        return pl.BlockSpec((seq, LANES), lambda b, g, i: (b, base + g))

    return pl.pallas_call(
        functools.partial(_nsa_kernel, tq=tq, tk=tk, seq=seq),
        grid=(batch, NSA_GROUPS, nq),
        in_specs=[
            pl.BlockSpec(memory_space=pltpu.SMEM),
            pl.BlockSpec((tq, hp * NSA_DIM), lambda b, g, i: (b * nq + i, qb + g)),
            pl.BlockSpec((LANES, tq), lambda b, g, i: (0, b * nq + i)),
            pl.BlockSpec((None, None, None, n_cmp, NSA_DIM), lambda b, g, i: (0, b, g, 0, 0)),
            pl.BlockSpec((None, None, None, NSA_DIM, n_cmp), lambda b, g, i: (1, b, g, 0, 0)),
            kv_spec(1, 0), kv_spec(1, 1),
            kv_spec(2, 0), kv_spec(2, 1),
            pl.BlockSpec(overlap_t.shape, lambda b, g, i: (0, 0)),
        ],
        out_specs=pl.BlockSpec((tq, hp * NSA_DIM), lambda b, g, i: (b * nq + i, g)),
        out_shape=jax.ShapeDtypeStruct((t, NSA_HEADS * NSA_DIM), BF16),
        scratch_shapes=[
            pltpu.VMEM((seq // tk, NSA_DIM + ROWS_L, tk), BF16),
            pltpu.VMEM((seq // tq, NSA_DIM + ROWS_L, tq), BF16),
            pltpu.VMEM((hp, WIN_LEN + tq, tq), F32),
            pltpu.VMEM((2, 2, tk, 2 * tq), F32),
            pltpu.VMEM((2, 2, 1, 2 * tq), F32),
            pltpu.VMEM((2, 2, 1, 2 * tq), F32),
            pltpu.VMEM((2, 1, 2 * tq), F32),
            pltpu.VMEM((2, NSA_DIM + ROWS_L, 2 * tq), F32),
        ],
        compiler_params=_params("parallel", "parallel", "arbitrary"),
        name="nsa_attn",
    )(slopes, proj, gates_t, k_cmp, vt_cmp, proj, proj, proj, proj, overlap_t)


def _memkv_kernel(mem_ref, g_ref, w_ref, o_ref):
    mn = _rmsnorm(mem_ref[...], g_ref[...]).astype(BF16)
    o_ref[...] = _dot(mn, w_ref[...]).astype(BF16)


def _memkv(mem, g, w):
    r, d = mem.shape
    n = w.shape[1]
    tm = _pick(r, 256)
    return pl.pallas_call(
        _memkv_kernel,
        grid=(r // tm,),
        in_specs=[
            pl.BlockSpec((tm, d), lambda i: (i, 0)),
            pl.BlockSpec((1, d), lambda i: (0, 0)),
            pl.BlockSpec((d, n), lambda i: (0, 0)),
        ],
        out_specs=pl.BlockSpec((tm, n), lambda i: (i, 0)),
        out_shape=jax.ShapeDtypeStruct((r, n), BF16),
        compiler_params=_params("parallel"),
        name="mem_kv",
    )(mem, g, w)


def _memattn_kernel(q_ref, k_ref, v_ref, o_ref):
    scale = MEM_DIM ** -0.5
    q = q_ref[...]
    k = k_ref[...]
    v = v_ref[...]
    outs = []
    for h in range(MEM_HEADS):
        sl = slice(h * MEM_DIM, (h + 1) * MEM_DIM)
        s = _dot_nt(q[:, sl], k[:, sl]) * scale
        m = jnp.max(s, axis=-1, keepdims=True)
        e = jnp.exp(s - m)
        p = e / jnp.sum(e, axis=-1, keepdims=True)
        outs.append(_dot(p.astype(BF16), v[:, sl]))
    o_ref[...] = jnp.concatenate(outs, axis=-1).astype(BF16)


def _mem_attention(proj, memkv, batch, seq, mem_len):
    t = proj.shape[0]
    tq = _pick(seq, 512)
    nq = seq // tq
    width = MEM_HEADS * MEM_DIM
    qb = P_MQ // width
    return pl.pallas_call(
        _memattn_kernel,
        grid=(batch, nq),
        in_specs=[
            pl.BlockSpec((tq, width), lambda b, i: (b * nq + i, qb)),
            pl.BlockSpec((mem_len, width), lambda b, i: (b, 0)),
            pl.BlockSpec((mem_len, width), lambda b, i: (b, 1)),
        ],
        out_specs=pl.BlockSpec((tq, width), lambda b, i: (b * nq + i, 0)),
        out_shape=jax.ShapeDtypeStruct((t, width), BF16),
        compiler_params=_params("parallel", "parallel"),
        name="mem_attn",
    )(proj, memkv, memkv)


def _merge_kernel(x_ref, hn_ref, od_ref, on_ref, om_ref, wgd_ref, wgn_ref, wgm_ref,
                  wud_ref, wun_ref, wum_ref, wo_ref, o_ref, acc_ref):
    j = pl.program_id(1)

    @pl.when(j == 0)
    def _():
        acc_ref[...] = jnp.zeros_like(acc_ref)

    hn = hn_ref[...]
    merged = jax.nn.sigmoid(_dot_nt(hn, wgd_ref[...])) * _dot(od_ref[...], wud_ref[...])
    merged += jax.nn.sigmoid(_dot_nt(hn, wgn_ref[...])) * _dot(on_ref[...], wun_ref[...])
    merged += jax.nn.sigmoid(_dot_nt(hn, wgm_ref[...])) * _dot(om_ref[...], wum_ref[...])
    acc_ref[...] += _dot(merged.astype(BF16), wo_ref[...])

    @pl.when(j == pl.num_programs(1) - 1)
    def _():
        o_ref[...] = x_ref[...] + acc_ref[...]


def _merge(x, hn, od, on, om, wg, wud, wun, wum, wo):
    t, d = x.shape
    tm = _pick(t, 512)
    tj = _pick(d, 512)
    nj = d // tj

    def gate_spec(branch):
        return pl.BlockSpec((tj, d), lambda i, j: (branch * nj + j, 0))

    return pl.pallas_call(
        _merge_kernel,
        grid=(t // tm, nj),
        in_specs=[
            pl.BlockSpec((tm, d), lambda i, j: (i, 0)),
            pl.BlockSpec((tm, d), lambda i, j: (i, 0)),
            pl.BlockSpec((tm, od.shape[1]), lambda i, j: (i, 0)),
            pl.BlockSpec((tm, on.shape[1]), lambda i, j: (i, 0)),
            pl.BlockSpec((tm, om.shape[1]), lambda i, j: (i, 0)),
            gate_spec(0), gate_spec(1), gate_spec(2),
            pl.BlockSpec((wud.shape[0], tj), lambda i, j: (0, j)),
            pl.BlockSpec((wun.shape[0], tj), lambda i, j: (0, j)),
            pl.BlockSpec((wum.shape[0], tj), lambda i, j: (0, j)),
            pl.BlockSpec((tj, d), lambda i, j: (j, 0)),
        ],
        out_specs=pl.BlockSpec((tm, d), lambda i, j: (i, 0)),
        out_shape=jax.ShapeDtypeStruct((t, d), F32),
        scratch_shapes=[pltpu.VMEM((tm, d), F32)],
        compiler_params=_params("parallel", "arbitrary"),
        name="merge",
    )(x, hn, od, on, om, wg, wg, wg, wud, wun, wum, wo)


def _overlap_matrix_t(n_slc, n_cmp):
    cmp_start = np.arange(n_cmp)[None, :] * CMP_STRIDE
    slc_start = np.arange(n_slc)[:, None] * SLC_LEN
    return ((cmp_start < slc_start + SLC_LEN) & (cmp_start + CMP_LEN > slc_start)).astype(np.float32)


def kernel(x, mem, ffn1_norm, ffn1_w_gate, ffn1_w_up, ffn1_w_down, mix_norm, w_in, diff_lambda, diff_subln, nsa_cmp_pos, nsa_cmp_w1, nsa_cmp_w2, mem_norm, w_mem_kv, w_up_diff, w_up_nsa, w_up_mem, w_out, ffn2_norm, ffn2_w_gate, ffn2_w_up, ffn2_w_down, final_norm):
    batch, seq, d = x.shape
    mem_len = mem.shape[1]
    depth = w_in.shape[0]
    t = batch * seq
    assert seq % CMP_STRIDE == 0 and CMP_LEN == 2 * CMP_STRIDE
    n_rows = seq // CMP_STRIDE
    n_slc = seq // SLC_LEN
    overlap_t = jnp.asarray(_overlap_matrix_t(n_slc, n_rows), BF16)

    col_scale = np.ones((1, P_COLS), np.float32)
    col_scale[0, C_DQ:C_DK] = (DIFF_QK_DIM ** -0.5) * LOG2E
    col_scale[0, C_NQ:C_NKV] = (NSA_DIM ** -0.5) * LOG2E
    col_scale = jnp.asarray(col_scale)

    w_in_t = jnp.swapaxes(w_in, 1, 2)

    xt = x.reshape(t, d)
    memt = mem.reshape(batch * mem_len, d)
    fg = final_norm.reshape(1, d)
    bf = lambda a: a.astype(BF16)

    for l in range(depth):
        lam_init = 0.8 - 0.6 * math.exp(-0.3 * l)
        xt, hn = _ffn(xt, ffn1_norm[l].reshape(1, d), ffn1_w_gate, ffn1_w_up, ffn1_w_down,
                      mix_norm[l].reshape(1, d), l, "normed")

        w = w_in_t[l]
        w_mq = bf(w[C_MQ:C_MG])
        w_ng = bf(jnp.pad(w[C_NG:C_MQ], ((0, LANES - NSA_GATE_COLS), (0, 0))))
        proj, ngt = _inproj(hn, w_in_t, l, w_mq, w_ng, col_scale)

        o_diff = _diff_attention(proj, diff_lambda[l], diff_subln[l].reshape(1, DIFF_V_DIM),
                                 batch, seq, lam_init)

        ckv = proj[:, C_NKV:C_NKV + 2 * NSA_GROUPS * NSA_DIM]
        ckv = ckv.reshape(batch, n_rows, CMP_STRIDE, 2, NSA_GROUPS, NSA_DIM)
        ckv = ckv.transpose(3, 4, 0, 1, 2, 5).reshape(2, NSA_GROUPS, batch, n_rows, CMP_STRIDE * NSA_DIM)
        pos = bf(jnp.broadcast_to(nsa_cmp_pos[l].reshape(2, 1, CMP_LEN * NSA_DIM), (2, 8, CMP_LEN * NSA_DIM)))
        kv_cmp, kvt_cmp = _compress(ckv, pos, bf(nsa_cmp_w1[l]), bf(nsa_cmp_w2[l]))
        o_nsa = _nsa_attention(proj, ngt, kv_cmp, kvt_cmp, overlap_t, batch, seq)

        memkv = _memkv(memt, mem_norm[l].reshape(1, d), bf(w_mem_kv[l]))
        o_mem = _mem_attention(proj, memkv, batch, seq, mem_len)

        w_mg = bf(w[C_MG:])
        xt = _merge(xt, hn, o_diff, o_nsa, o_mem, w_mg, bf(w_up_diff[l]), bf(w_up_nsa[l]),
                    bf(w_up_mem[l]), bf(w_out[l]))

        (xt,) = _ffn(xt, ffn2_norm[l].reshape(1, d), ffn2_w_gate, ffn2_w_up, ffn2_w_down, fg, l,
                     "final" if l == depth - 1 else "plain")
    return xt.reshape(batch, seq, d)
```

```python
import functools
import math

import jax
import jax.numpy as jnp
import numpy as np
from jax import lax
from jax.experimental import pallas as pl
from jax.experimental.pallas import tpu as pltpu

F32 = jnp.float32
BF16 = jnp.bfloat16

NORM_EPS = 1e-6
NEG_INF = -1e30
SEL_BIG = 1e9
LOG2E = math.log2(math.e)

DIFF_HEADS = 8
DIFF_QK_DIM = 64
DIFF_V_DIM = 2 * DIFF_QK_DIM

NSA_HEADS = 8
NSA_GROUPS = 2
NSA_HPG = NSA_HEADS // NSA_GROUPS
NSA_DIM = 128
CMP_LEN = 32
CMP_STRIDE = 16
SLC_LEN = 64
SLC_TOPK = 8
WIN_LEN = 512

MEM_HEADS = 4
MEM_DIM = 128
N_BRANCH = 3

LANES = 128
VMEM_LIMIT = 56 * 1024 * 1024

DIFF_QK_COLS = DIFF_HEADS * 2 * DIFF_QK_DIM
DIFF_V_COLS = DIFF_HEADS * DIFF_V_DIM
NSA_Q_COLS = NSA_HEADS * NSA_DIM
NSA_KV_COLS = 3 * 2 * NSA_GROUPS * NSA_DIM
NSA_GATE_COLS = NSA_HEADS * 3
MEM_Q_COLS = MEM_HEADS * MEM_DIM
C_DQ = 0
C_DK = C_DQ + DIFF_QK_COLS
C_DV = C_DK + DIFF_QK_COLS
C_NQ = C_DV + DIFF_V_COLS
C_NKV = C_NQ + NSA_Q_COLS
C_NG = C_NKV + NSA_KV_COLS
C_MQ = C_NG + NSA_GATE_COLS
C_MG = C_MQ + MEM_Q_COLS
P_MQ = C_NG
P_COLS = P_MQ + MEM_Q_COLS


def _alibi_slopes(n_heads):
    return np.array([2.0 ** (-8.0 * (h + 1) / n_heads) for h in range(n_heads)], dtype=np.float32)


def _rmsnorm(x, g):
    y = x * lax.rsqrt(jnp.mean(x * x, axis=-1, keepdims=True) + NORM_EPS)
    return y * g


def _dot(a, b):
    return jnp.dot(a, b, preferred_element_type=F32)


def _dot_nt(a, b):
    return lax.dot_general(a, b, (((1,), (1,)), ((), ())), preferred_element_type=F32)


def _transpose_bf16(x):
    return x.astype(F32).T.astype(BF16)


def _params(*sem):
    return pltpu.CompilerParams(dimension_semantics=sem, vmem_limit_bytes=VMEM_LIMIT)


def _pick(n, pref):
    t = min(n, pref)
    assert n % t == 0, (n, t)
    return t


def _ffn_kernel(x_ref, g_ref, wg_ref, wu_ref, wd_ref, fg_ref, o_ref, *rest, tail):
    xn_ref = rest[-1]
    j = pl.program_id(1)

    @pl.when(j == 0)
    def _():
        xn_ref[...] = _rmsnorm(x_ref[...], g_ref[...]).astype(BF16)
        o_ref[...] = jnp.zeros_like(o_ref)

    xn = xn_ref[...]
    a = _dot(xn, wg_ref[...].astype(BF16))
    u = _dot(xn, wu_ref[...].astype(BF16))
    h = (a * jax.nn.sigmoid(a)) * u
    o_ref[...] += _dot(h.astype(BF16), wd_ref[...].astype(BF16))

    @pl.when(j == pl.num_programs(1) - 1)
    def _():
        rows = 128

        def chunk(c, carry):
            r = pl.ds(pl.multiple_of(c * rows, rows), rows)
            y = x_ref[r, :] + 0.5 * o_ref[r, :]
            if tail == "final":
                y = _rmsnorm(y, fg_ref[...])
            if tail == "normed":
                rest[0][r, :] = _rmsnorm(y, fg_ref[...]).astype(BF16)
            o_ref[r, :] = y
            return carry

        lax.fori_loop(0, o_ref.shape[0] // rows, chunk, 0)


def _ffn(x, g, wg, wu, wd, fg, layer, tail):
    t, d = x.shape
    f = wg.shape[2]
    tm = _pick(t, 1024)
    tf = _pick(f, 256)
    out_specs = [pl.BlockSpec((tm, d), lambda i, j: (i, 0))]
    out_shape = [jax.ShapeDtypeStruct((t, d), F32)]
    if tail == "normed":
        out_specs.append(pl.BlockSpec((tm, d), lambda i, j: (i, 0)))
        out_shape.append(jax.ShapeDtypeStruct((t, d), BF16))
    return pl.pallas_call(
        functools.partial(_ffn_kernel, tail=tail),
        grid=(t // tm, f // tf),
        in_specs=[
            pl.BlockSpec((tm, d), lambda i, j: (i, 0)),
            pl.BlockSpec((1, d), lambda i, j: (0, 0)),
            pl.BlockSpec((None, d, tf), lambda i, j: (layer, 0, j)),
            pl.BlockSpec((None, d, tf), lambda i, j: (layer, 0, j)),
            pl.BlockSpec((None, tf, d), lambda i, j: (layer, j, 0)),
            pl.BlockSpec((1, d), lambda i, j: (0, 0)),
        ],
        out_specs=out_specs,
        out_shape=out_shape,
        scratch_shapes=[] if tail == "normed" else [pltpu.VMEM((tm, d), BF16)],
        compiler_params=_params("parallel", "arbitrary"),
        name="ffn",
    )(x, g, wg, wu, wd, fg)


def _tail_kernel(w_ref, tail_ref, ng_ref, carry_ref, *, shift):
    s = pl.program_id(1)
    blk = w_ref[...]

    @pl.when(s == 0)
    def _():
        pad = jnp.zeros((ng_ref.shape[0] - shift, blk.shape[1]), F32)
        ng_ref[...] = jnp.concatenate([blk[:shift], pad], axis=0).astype(BF16)

    @pl.when(s > 0)
    def _():
        tail_ref[...] = jnp.concatenate([carry_ref[...], blk[:shift]], axis=0).astype(BF16)

    carry_ref[...] = blk[shift:]


def _tail_weights(w_in_t):
    depth, rows, d = w_in_t.shape
    tb = MEM_Q_COLS
    shift = C_MQ - C_NG
    n_out = -(-(rows - C_MQ) // tb)
    assert C_NG % tb == 0 and shift % 8 == 0 and shift <= LANES
    first, last = C_NG // tb, (rows - 1) // tb
    return pl.pallas_call(
        functools.partial(_tail_kernel, shift=shift),
        grid=(depth, n_out + 1),
        in_specs=[pl.BlockSpec((None, tb, d), lambda l, s: (l, jnp.minimum(first + s, last), 0))],
        out_specs=[
            pl.BlockSpec((None, tb, d), lambda l, s: (l, jnp.maximum(s - 1, 0), 0)),
            pl.BlockSpec((None, LANES, d), lambda l, s: (l, 0, 0)),
        ],
        out_shape=[
            jax.ShapeDtypeStruct((depth, n_out * tb, d), BF16),
            jax.ShapeDtypeStruct((depth, LANES, d), BF16),
        ],
        scratch_shapes=[pltpu.VMEM((tb - shift, d), F32)],
        compiler_params=_params("parallel", "arbitrary"),
        name="tail_weights",
    )(w_in_t)


def _inproj_kernel(hn_ref, w_ref, wmq_ref, wng_ref, cs_ref, proj_ref, ngt_ref, *, n_main):
    j = pl.program_id(1)

    @pl.when(j == 0)
    def _():
        ngt_ref[...] = jax.nn.sigmoid(_dot_nt(wng_ref[...], hn_ref[...]))

    @pl.when(j < n_main)
    def _():
        proj_ref[...] = (_dot_nt(hn_ref[...], w_ref[...].astype(BF16)) * cs_ref[...]).astype(BF16)

    @pl.when(j >= n_main)
    def _():
        proj_ref[...] = (_dot_nt(hn_ref[...], wmq_ref[...]) * cs_ref[...]).astype(BF16)


def _inproj(hn, w_in_t, layer, tail, wng, cs):
    t, d = hn.shape
    n = P_COLS
    tm = _pick(t, 2048)
    tn = MEM_Q_COLS
    assert P_MQ % tn == 0
    n_main = P_MQ // tn
    return pl.pallas_call(
        functools.partial(_inproj_kernel, n_main=n_main),
        grid=(t // tm, n // tn),
        in_specs=[
            pl.BlockSpec((tm, d), lambda i, j: (i, 0)),
            pl.BlockSpec((None, tn, d), lambda i, j: (layer, jnp.minimum(j, n_main - 1), 0)),
            pl.BlockSpec((None, tn, d), lambda i, j: (layer, 0, 0)),
            pl.BlockSpec((None, LANES, d), lambda i, j: (layer, 0, 0)),
            pl.BlockSpec((1, tn), lambda i, j: (0, j)),
        ],
        out_specs=[
            pl.BlockSpec((tm, tn), lambda i, j: (i, j)),
            pl.BlockSpec((LANES, tm), lambda i, j: (0, i)),
        ],
        out_shape=[
            jax.ShapeDtypeStruct((t, n), BF16),
            jax.ShapeDtypeStruct((LANES, t), F32),
        ],
        compiler_params=_params("parallel", "arbitrary"),
        name="inproj",
    )(hn, w_in_t, tail, wng, cs)


SUB = 128
ROWS_L = 16


def _assert_slopes_fit(slopes2):
    assert float(np.max(slopes2)) * SUB < 100.0, "ALiBi factor per bias block leaves the safe f32 range"


def _bias_factor_row(slope2, width):
    r = lax.broadcasted_iota(jnp.int32, (1, width), 1) & (SUB - 1)
    return jnp.exp2(slope2 * r.astype(F32))


def _scaled_vt(vt, w_row):
    ones = jnp.broadcast_to(w_row, (ROWS_L, w_row.shape[1]))
    return jnp.concatenate([vt * w_row, ones], axis=0).astype(BF16)


def _block_weights(z, off0, step, m_prev):
    nb = z.shape[0] // SUB
    blocks = [z[j * SUB:(j + 1) * SUB] for j in range(nb)]
    offs = [off0 + j * step for j in range(nb)]
    m_new = m_prev
    for zb, off in zip(blocks, offs):
        m_new = jnp.maximum(m_new, jnp.max(zb, axis=0, keepdims=True) + off)
    p = [jnp.exp2(zb - (m_new - off)).astype(BF16) for zb, off in zip(blocks, offs)]
    return m_new, jnp.concatenate(p, axis=0)


def _split_acc(acc, dv):
    return acc[0:dv] * (1.0 / jnp.maximum(acc[dv:dv + 1], 1e-30))


def _diff_kernel(slopes_ref, q_ref, k_ref, v_ref, lp_ref, sg_ref, o_ref,
                 vt_ref, pen_ref, z_ref, mrow_ref, alpha_ref, m_ref, acc_ref, *, tile, n_tiles, lam_init):
    h = pl.program_id(1)
    qi = pl.program_id(2)
    slope2 = slopes_ref[h]
    nb = tile // SUB

    @pl.when(qi == 0)
    def _():
        w_row = _bias_factor_row(slope2, tile)
        for t in range(n_tiles):
            vt_ref[t] = _scaled_vt(v_ref[t * tile:(t + 1) * tile, :].astype(F32).T, w_row)
        row = lax.broadcasted_iota(jnp.int32, (tile, tile), 0)
        col = lax.broadcasted_iota(jnp.int32, (tile, tile), 1)
        pen_ref[0] = jnp.zeros((tile, tile), F32)
        pen_ref[1] = jnp.where(row <= col, 0.0, NEG_INF)

    qt = _transpose_bf16(q_ref[...])
    feat = lax.broadcasted_iota(jnp.int32, qt.shape, 0)
    qts = [jnp.where((feat >= m * DIFF_QK_DIM) & (feat < (m + 1) * DIFF_QK_DIM), qt, 0.0).astype(BF16)
           for m in range(2)]

    m_ref[...] = jnp.full_like(m_ref, NEG_INF)
    acc_ref[...] = jnp.zeros_like(acc_ref)
    step = slope2 * float(SUB)

    def block_off(t, j):
        return slope2 * ((t - qi) * tile).astype(F32) + j * step

    def scores(t, m):
        ks = pl.multiple_of(t * tile, tile)
        return _dot(k_ref[pl.ds(ks, tile), :], qts[m]) + pen_ref[(t == qi).astype(jnp.int32)]

    def bookkeep(t, slot, m, z):
        z_ref[slot, m] = z
        m_old = m_ref[m]
        m_new = m_old
        for j in range(nb):
            m_new = jnp.maximum(m_new, jnp.max(z[j * SUB:(j + 1) * SUB], axis=0, keepdims=True)
                                + block_off(t, j))
        m_ref[m] = m_new
        mrow_ref[slot, m] = m_new
        alpha_ref[slot, m] = jnp.exp2(m_old - m_new)

    def weights(t, slot, m):
        m_new = mrow_ref[slot, m]
        return jnp.concatenate(
            [jnp.exp2(z_ref[slot, m, j * SUB:(j + 1) * SUB, :] - (m_new - block_off(t, j))).astype(BF16)
             for j in range(nb)], axis=0)

    def accumulate(t, slot, m, p):
        acc_ref[m] = alpha_ref[slot, m] * acc_ref[m] + _dot(vt_ref[t], p)

    def overlapped(t, slot):
        for m in range(2):
            z_next = scores(t + 1, m)
            p = weights(t, slot, m)
            bookkeep(t + 1, 1 - slot, m, z_next)
            accumulate(t, slot, m, p)

    def drain(t, slot):
        for m in range(2):
            accumulate(t, slot, m, weights(t, slot, m))

    for m in range(2):
        bookkeep(0, 0, m, scores(0, m))

    def body(u, carry):
        overlapped(2 * u, 0)
        overlapped(2 * u + 1, 1)
        return carry

    lax.fori_loop(0, qi // 2, body, 0)

    @pl.when(qi % 2 == 1)
    def _():
        overlapped(qi - 1, 0)
        drain(qi, 1)

    @pl.when(qi % 2 == 0)
    def _():
        drain(qi, 0)

    o1 = _split_acc(acc_ref[0], DIFF_V_DIM)
    o2 = _split_acc(acc_ref[1], DIFF_V_DIM)
    lp = lp_ref[...]
    lam = (jnp.exp(jnp.sum(lp[0:1] * lp[1:2], axis=-1, keepdims=True))
           - jnp.exp(jnp.sum(lp[2:3] * lp[3:4], axis=-1, keepdims=True)) + lam_init)
    ot = o1 - lam * o2
    ot = ot * lax.rsqrt(jnp.mean(ot * ot, axis=0, keepdims=True) + NORM_EPS)
    o_ref[...] = (ot.T * sg_ref[...] * (1.0 - lam_init)).astype(BF16)


def _diff_attention(proj, lp, sg, batch, seq, lam_init):
    t = proj.shape[0]
    tile = _pick(seq, 512)
    nq = seq // tile
    kb, vb = C_DK // LANES, C_DV // LANES
    slopes2 = _alibi_slopes(DIFF_HEADS) * np.float32(LOG2E)
    _assert_slopes_fit(slopes2)
    slopes = jnp.asarray(slopes2)
    assert tile % SUB == 0
    return pl.pallas_call(
        functools.partial(_diff_kernel, tile=tile, n_tiles=nq, lam_init=lam_init),
        grid=(batch, DIFF_HEADS, nq),
        in_specs=[
            pl.BlockSpec(memory_space=pltpu.SMEM),
            pl.BlockSpec((tile, LANES), lambda b, h, i: (b * nq + i, h)),
            pl.BlockSpec((seq, LANES), lambda b, h, i: (b, kb + h)),
            pl.BlockSpec((seq, LANES), lambda b, h, i: (b, vb + h)),
            pl.BlockSpec((4, DIFF_QK_DIM), lambda b, h, i: (0, 0)),
            pl.BlockSpec((1, DIFF_V_DIM), lambda b, h, i: (0, 0)),
        ],
        out_specs=pl.BlockSpec((tile, LANES), lambda b, h, i: (b * nq + i, h)),
        out_shape=jax.ShapeDtypeStruct((t, DIFF_HEADS * DIFF_V_DIM), BF16),
        scratch_shapes=[
            pltpu.VMEM((nq, DIFF_V_DIM + ROWS_L, tile), BF16),
            pltpu.VMEM((2, tile, tile), F32),
            pltpu.VMEM((2, 2, tile, tile), F32),
            pltpu.VMEM((2, 2, 1, tile), F32),
            pltpu.VMEM((2, 2, 1, tile), F32),
            pltpu.VMEM((2, 1, tile), F32),
            pltpu.VMEM((2, DIFF_V_DIM + ROWS_L, tile), F32),
        ],
        compiler_params=_params("parallel", "parallel", "arbitrary"),
        name="diff_attn",
    )(slopes, proj, proj, proj, lp, sg)


def _compress_kernel(kv_ref, pos_ref, w1_ref, w2_ref, o_ref, ot_ref):
    half = w1_ref.shape[0] // 2
    kv = kv_ref[...]
    top = _dot(kv, w1_ref[:half, :])
    bot = _dot(kv, w1_ref[half:, :])
    rows = kv.shape[0]
    bot_next = pltpu.roll(bot, rows - 1, 0)
    pos_term = _dot(pos_ref[...], w1_ref[...])[0:1]
    hid = jax.nn.gelu(top + bot_next + pos_term)
    out = _dot(hid.astype(BF16), w2_ref[...])
    o_ref[...] = out.astype(BF16)
    ot_ref[...] = out.T.astype(BF16)


def _compress(kvr, pos, w1, w2):
    _, groups, batch, rows, width = kvr.shape
    d = w2.shape[-1]
    return pl.pallas_call(
        _compress_kernel,
        grid=(2, groups, batch),
        in_specs=[
            pl.BlockSpec((None, None, None, rows, width), lambda a, g, b: (a, g, b, 0, 0)),
            pl.BlockSpec((None, 8, 2 * width), lambda a, g, b: (a, 0, 0)),
            pl.BlockSpec((None, 2 * width, d), lambda a, g, b: (a, 0, 0)),
            pl.BlockSpec((None, d, d), lambda a, g, b: (a, 0, 0)),
        ],
        out_specs=[
            pl.BlockSpec((None, None, None, rows, d), lambda a, g, b: (a, b, g, 0, 0)),
            pl.BlockSpec((None, None, None, d, rows), lambda a, g, b: (a, b, g, 0, 0)),
        ],
        out_shape=[
            jax.ShapeDtypeStruct((2, batch, groups, rows, d), BF16),
            jax.ShapeDtypeStruct((2, batch, groups, d, rows), BF16),
        ],
        compiler_params=_params("parallel", "parallel", "parallel"),
        name="compress",
    )(kvr, pos, w1, w2)


def _nsa_kernel(slopes_ref, q_ref, gate_ref, kc_ref, vct_ref, ks_ref, vs_ref, kw_ref, vw_ref,
                ovt_ref, o_ref, vst_ref, vwt_ref, base_ref, z_ref, mrow_ref, alpha_ref, m_ref, acc_ref,
                *, tq, tk, seq):
    g = pl.program_id(1)
    qi = pl.program_id(2)
    q0 = qi * tq
    hp = NSA_HPG
    n = hp * tq
    dv = NSA_DIM
    span = WIN_LEN + tq
    n_slc = seq // SLC_LEN
    slope_row = jnp.concatenate(
        [jnp.full((1, tq), slopes_ref[g * hp + h], F32) for h in range(hp)], axis=1)

    @pl.when(qi == 0)
    def _():
        ones_s = jnp.ones((1, tk), F32)
        ones_w = jnp.ones((1, tq), F32)
        for t in range(seq // tk):
            vst_ref[t] = _scaled_vt(vs_ref[t * tk:(t + 1) * tk, :].astype(F32).T, ones_s)
        for t in range(seq // tq):
            vwt_ref[t] = _scaled_vt(vw_ref[t * tq:(t + 1) * tq, :].astype(F32).T, ones_w)
        key_row = lax.broadcasted_iota(jnp.int32, (span, tq), 0).astype(F32)
        for h in range(hp):
            base_ref[h] = slopes_ref[g * hp + h] * key_row

    q = q_ref[...]
    q4t = jnp.concatenate(
        [q[:, h * NSA_DIM:(h + 1) * NSA_DIM].astype(F32).T for h in range(hp)], axis=1).astype(BF16)
    qpos_all = q0 + (lax.broadcasted_iota(jnp.int32, (1, n), 1) & (tq - 1))
    qpos = q0 + lax.broadcasted_iota(jnp.int32, (1, tq), 1)

    def tile_heads(a):
        return jnp.concatenate([a] * hp, axis=1)

    n_cmp = kc_ref.shape[0]
    cmp_end = lax.broadcasted_iota(jnp.int32, (n_cmp, 1), 0) * CMP_STRIDE + (CMP_LEN - 1)
    mask_c = cmp_end <= qpos_all
    z_c = jnp.where(mask_c, _dot(kc_ref[...], q4t), NEG_INF)
    m_c = jnp.max(z_c, axis=0, keepdims=True)
    e_c = jnp.where(mask_c, jnp.exp2(z_c - m_c), 0.0)
    p_c = e_c * (1.0 / jnp.maximum(jnp.sum(e_c, axis=0, keepdims=True), 1e-30))
    o_c = _dot(vct_ref[...], p_c.astype(BF16))

    wt = jnp.maximum(qi - WIN_LEN // tq, 0)
    w0 = pl.multiple_of(wt * tq, tq)
    dist = qpos - (w0 + lax.broadcasted_iota(jnp.int32, (span, 1), 0))
    in_win = (dist >= 0) & (dist < WIN_LEN)
    pen_w = jnp.concatenate([jnp.where(in_win, base_ref[h], NEG_INF) for h in range(hp)], axis=1)
    z_w = _dot(kw_ref[pl.ds(w0, span), :], q4t) + pen_w
    p_w = jnp.exp2(z_w - jnp.max(z_w, axis=0, keepdims=True)).astype(BF16)
    vw_cat = jnp.concatenate([vwt_ref[wt + j] for j in range(span // tq)], axis=1)
    o_w = _split_acc(_dot(vw_cat, p_w), dv)

    p_sum = p_c[:, 0:tq]
    for h in range(1, hp):
        p_sum = p_sum + p_c[:, h * tq:(h + 1) * tq]
    ovt = ovt_ref[...]
    p_hi = p_sum.astype(BF16)
    r1 = p_sum - p_hi.astype(F32)
    p_mid = r1.astype(BF16)
    p_lo = (r1 - p_mid.astype(F32)).astype(BF16)
    imp = _dot(ovt, p_hi) + _dot(ovt, p_mid) + _dot(ovt, p_lo)
    jblk = lax.broadcasted_iota(jnp.int32, (n_slc, tq), 0)
    cur = qpos // SLC_LEN
    forced = (jblk == 0) | (jblk == cur) | (jblk == cur - 1)
    imp = jnp.where(forced, SEL_BIG, jnp.where(jblk > cur, -SEL_BIG, imp))
    jblk_f = jblk.astype(F32)
    sel = jnp.zeros((n_slc, tq), F32)
    for _ in range(min(SLC_TOPK, n_slc)):
        mx = jnp.max(imp, axis=0, keepdims=True)
        first = jnp.min(jnp.where(imp == mx, jblk_f, float(n_slc)), axis=0, keepdims=True)
        hit = jblk_f == first
        sel = jnp.where(hit, 1.0, sel)
        imp = jnp.where(hit, -jnp.inf, imp)
    sel_bf = sel.astype(BF16)

    m_ref[...] = jnp.full_like(m_ref, NEG_INF)
    acc_ref[...] = jnp.zeros_like(acc_ref)
    exp_blk = lax.broadcasted_iota(jnp.int32, (tk, n_slc), 0) // SLC_LEN
    exp_col = lax.broadcasted_iota(jnp.int32, (tk, n_slc), 1)
    krow = lax.broadcasted_iota(jnp.int32, (tk, 1), 0)
    assert hp == 4
    half = 2 * tq

    def tile_off(t, i):
        return slope_row[:, i * half:(i + 1) * half] * (t * tk - q0).astype(F32)

    def penalty(t):
        expand = jnp.where(exp_col == exp_blk + t * (tk // SLC_LEN), 1.0, 0.0).astype(BF16)
        chosen = _dot(expand, sel_bf)
        keep = (chosen > 0.5) & ((krow + t * tk) <= qpos)
        return [jnp.concatenate([jnp.where(keep, base_ref[2 * i + j, 0:tk, :], NEG_INF)
                                 for j in range(2)], axis=1) for i in range(2)]

    def scores(t, i, pen):
        ks = pl.multiple_of(t * tk, tk)
        return _dot(ks_ref[pl.ds(ks, tk), :], q4t[:, i * half:(i + 1) * half]) + pen[i]

    def bookkeep(t, slot, i, z):
        z_ref[slot, i] = z
        m_old = m_ref[i]
        m_new = jnp.maximum(m_old, jnp.max(z, axis=0, keepdims=True) + tile_off(t, i))
        m_ref[i] = m_new
        mrow_ref[slot, i] = m_new
        alpha_ref[slot, i] = jnp.exp2(m_old - m_new)

    def weights(t, slot, i):
        return jnp.exp2(z_ref[slot, i] - (mrow_ref[slot, i] - tile_off(t, i))).astype(BF16)

    def accumulate(t, slot, i, p):
        acc_ref[i] = alpha_ref[slot, i] * acc_ref[i] + _dot(vst_ref[t], p)

    def overlapped(t, slot):
        pen = penalty(t + 1)
        for i in range(2):
            z_next = scores(t + 1, i, pen)
            p = weights(t, slot, i)
            bookkeep(t + 1, 1 - slot, i, z_next)
            accumulate(t, slot, i, p)

    def drain(t, slot):
        for i in range(2):
            accumulate(t, slot, i, weights(t, slot, i))

    pen = penalty(0)
    for i in range(2):
        bookkeep(0, 0, i, scores(0, i, pen))
    n_last = q0 // tk

    def body(u, carry):
        overlapped(2 * u, 0)
        overlapped(2 * u + 1, 1)
        return carry

    lax.fori_loop(0, n_last // 2, body, 0)

    @pl.when(n_last % 2 == 1)
    def _():
        overlapped(n_last - 1, 0)
        drain(n_last, 1)

    @pl.when(n_last % 2 == 0)
    def _():
        drain(n_last, 0)

    o_s = jnp.concatenate([_split_acc(acc_ref[i], dv) for i in range(2)], axis=1)

    def gate_row(branch):
        return jnp.concatenate(
            [gate_ref[pl.ds((g * hp + h) * 3 + branch, 1), :] for h in range(hp)], axis=1)

    ot = gate_row(0) * o_c + gate_row(1) * o_s + gate_row(2) * o_w
    o_ref[...] = jnp.concatenate(
        [ot[:, h * tq:(h + 1) * tq].T for h in range(hp)], axis=1).astype(BF16)


def _nsa_attention(proj, gates_t, k_cmp, vt_cmp, overlap_t, batch, seq):
    t = proj.shape[0]
    tq = _pick(seq, 256)
    tk = _pick(seq, 512)
    assert seq >= WIN_LEN + tq and WIN_LEN % tq == 0 and tk % tq == 0
    nq = seq // tq
    hp = NSA_HPG
    qb = C_NQ // (hp * NSA_DIM)
    kvb = C_NKV // LANES
    n_cmp = k_cmp.shape[3]
    slopes2 = _alibi_slopes(NSA_HEADS) * np.float32(LOG2E)
    _assert_slopes_fit(slopes2)
    slopes = jnp.asarray(slopes2)

    def kv_spec(branch, which):
        base = kvb + (branch * 2 + which) * NSA_GROUPS
        return pl.BlockSpec((seq, LANES), lambda b, g, i: (b, base + g))

    return pl.pallas_call(
        functools.partial(_nsa_kernel, tq=tq, tk=tk, seq=seq),
        grid=(batch, NSA_GROUPS, nq),
        in_specs=[
            pl.BlockSpec(memory_space=pltpu.SMEM),
            pl.BlockSpec((tq, hp * NSA_DIM), lambda b, g, i: (b * nq + i, qb + g)),
            pl.BlockSpec((LANES, tq), lambda b, g, i: (0, b * nq + i)),
            pl.BlockSpec((None, None, None, n_cmp, NSA_DIM), lambda b, g, i: (0, b, g, 0, 0)),
            pl.BlockSpec((None, None, None, NSA_DIM, n_cmp), lambda b, g, i: (1, b, g, 0, 0)),
            kv_spec(1, 0), kv_spec(1, 1),
            kv_spec(2, 0), kv_spec(2, 1),
            pl.BlockSpec(overlap_t.shape, lambda b, g, i: (0, 0)),
        ],
        out_specs=pl.BlockSpec((tq, hp * NSA_DIM), lambda b, g, i: (b * nq + i, g)),
        out_shape=jax.ShapeDtypeStruct((t, NSA_HEADS * NSA_DIM), BF16),
        scratch_shapes=[
            pltpu.VMEM((seq // tk, NSA_DIM + ROWS_L, tk), BF16),
            pltpu.VMEM((seq // tq, NSA_DIM + ROWS_L, tq), BF16),
            pltpu.VMEM((hp, WIN_LEN + tq, tq), F32),
            pltpu.VMEM((2, 2, tk, 2 * tq), F32),
            pltpu.VMEM((2, 2, 1, 2 * tq), F32),
            pltpu.VMEM((2, 2, 1, 2 * tq), F32),
            pltpu.VMEM((2, 1, 2 * tq), F32),
            pltpu.VMEM((2, NSA_DIM + ROWS_L, 2 * tq), F32),
        ],
        compiler_params=_params("parallel", "parallel", "arbitrary"),
        name="nsa_attn",
    )(slopes, proj, gates_t, k_cmp, vt_cmp, proj, proj, proj, proj, overlap_t)


def _memkv_kernel(mem_ref, g_ref, w_ref, o_ref):
    mn = _rmsnorm(mem_ref[...], g_ref[...]).astype(BF16)
    o_ref[...] = _dot(mn, w_ref[...]).astype(BF16)


def _memkv(mem, g, w):
    r, d = mem.shape
    n = w.shape[1]
    tm = _pick(r, 256)
    return pl.pallas_call(
        _memkv_kernel,
        grid=(r // tm,),
        in_specs=[
            pl.BlockSpec((tm, d), lambda i: (i, 0)),
            pl.BlockSpec((1, d), lambda i: (0, 0)),
            pl.BlockSpec((d, n), lambda i: (0, 0)),
        ],
        out_specs=pl.BlockSpec((tm, n), lambda i: (i, 0)),
        out_shape=jax.ShapeDtypeStruct((r, n), BF16),
        compiler_params=_params("parallel"),
        name="mem_kv",
    )(mem, g, w)


def _memattn_kernel(q_ref, k_ref, v_ref, o_ref):
    scale = MEM_DIM ** -0.5
    q = q_ref[...]
    k = k_ref[...]
    v = v_ref[...]
    outs = []
    for h in range(MEM_HEADS):
        sl = slice(h * MEM_DIM, (h + 1) * MEM_DIM)
        s = _dot_nt(q[:, sl], k[:, sl]) * scale
        m = jnp.max(s, axis=-1, keepdims=True)
        e = jnp.exp(s - m)
        p = e / jnp.sum(e, axis=-1, keepdims=True)
        outs.append(_dot(p.astype(BF16), v[:, sl]))
    o_ref[...] = jnp.concatenate(outs, axis=-1).astype(BF16)


def _mem_attention(proj, memkv, batch, seq, mem_len):
    t = proj.shape[0]
    tq = _pick(seq, 512)
    nq = seq // tq
    width = MEM_HEADS * MEM_DIM
    qb = P_MQ // width
    return pl.pallas_call(
        _memattn_kernel,
        grid=(batch, nq),
        in_specs=[
            pl.BlockSpec((tq, width), lambda b, i: (b * nq + i, qb)),
            pl.BlockSpec((mem_len, width), lambda b, i: (b, 0)),
            pl.BlockSpec((mem_len, width), lambda b, i: (b, 1)),
        ],
        out_specs=pl.BlockSpec((tq, width), lambda b, i: (b * nq + i, 0)),
        out_shape=jax.ShapeDtypeStruct((t, width), BF16),
        compiler_params=_params("parallel", "parallel"),
        name="mem_attn",
    )(proj, memkv, memkv)


def _merge_kernel(x_ref, hn_ref, od_ref, on_ref, om_ref, wgd_ref, wgn_ref, wgm_ref,
                  wud_ref, wun_ref, wum_ref, wo_ref, o_ref, acc_ref):
    j = pl.program_id(1)

    @pl.when(j == 0)
    def _():
        acc_ref[...] = jnp.zeros_like(acc_ref)

    hn = hn_ref[...]
    merged = jax.nn.sigmoid(_dot_nt(hn, wgd_ref[...])) * _dot(od_ref[...], wud_ref[...])
    merged += jax.nn.sigmoid(_dot_nt(hn, wgn_ref[...])) * _dot(on_ref[...], wun_ref[...])
    merged += jax.nn.sigmoid(_dot_nt(hn, wgm_ref[...])) * _dot(om_ref[...], wum_ref[...])
    acc_ref[...] += _dot(merged.astype(BF16), wo_ref[...])

    @pl.when(j == pl.num_programs(1) - 1)
    def _():
        o_ref[...] = x_ref[...] + acc_ref[...]


def _merge(x, hn, od, on, om, tail, layer, wud, wun, wum, wo):
    t, d = x.shape
    tm = _pick(t, 512)
    tj = _pick(d, 512)
    assert MEM_Q_COLS % tj == 0

    def gate_spec(branch):
        first = (MEM_Q_COLS + branch * d) // tj
        return pl.BlockSpec((None, tj, d), lambda i, j: (layer, first + j, 0))

    return pl.pallas_call(
        _merge_kernel,
        grid=(t // tm, d // tj),
        in_specs=[
            pl.BlockSpec((tm, d), lambda i, j: (i, 0)),
            pl.BlockSpec((tm, d), lambda i, j: (i, 0)),
            pl.BlockSpec((tm, od.shape[1]), lambda i, j: (i, 0)),
            pl.BlockSpec((tm, on.shape[1]), lambda i, j: (i, 0)),
            pl.BlockSpec((tm, om.shape[1]), lambda i, j: (i, 0)),
            gate_spec(0), gate_spec(1), gate_spec(2),
            pl.BlockSpec((wud.shape[0], tj), lambda i, j: (0, j)),
            pl.BlockSpec((wun.shape[0], tj), lambda i, j: (0, j)),
            pl.BlockSpec((wum.shape[0], tj), lambda i, j: (0, j)),
            pl.BlockSpec((tj, d), lambda i, j: (j, 0)),
        ],
        out_specs=pl.BlockSpec((tm, d), lambda i, j: (i, 0)),
        out_shape=jax.ShapeDtypeStruct((t, d), F32),
        scratch_shapes=[pltpu.VMEM((tm, d), F32)],
        compiler_params=_params("parallel", "arbitrary"),
        name="merge",
    )(x, hn, od, on, om, tail, tail, tail, wud, wun, wum, wo)


def _overlap_matrix_t(n_slc, n_cmp):
    cmp_start = np.arange(n_cmp)[None, :] * CMP_STRIDE
    slc_start = np.arange(n_slc)[:, None] * SLC_LEN
    return ((cmp_start < slc_start + SLC_LEN) & (cmp_start + CMP_LEN > slc_start)).astype(np.float32)


def kernel(x, mem, ffn1_norm, ffn1_w_gate, ffn1_w_up, ffn1_w_down, mix_norm, w_in, diff_lambda, diff_subln, nsa_cmp_pos, nsa_cmp_w1, nsa_cmp_w2, mem_norm, w_mem_kv, w_up_diff, w_up_nsa, w_up_mem, w_out, ffn2_norm, ffn2_w_gate, ffn2_w_up, ffn2_w_down, final_norm):
    batch, seq, d = x.shape
    mem_len = mem.shape[1]
    depth = w_in.shape[0]
    t = batch * seq
    assert seq % CMP_STRIDE == 0 and CMP_LEN == 2 * CMP_STRIDE
    n_rows = seq // CMP_STRIDE
    n_slc = seq // SLC_LEN
    overlap_t = jnp.asarray(_overlap_matrix_t(n_slc, n_rows), BF16)

    col_scale = np.ones((1, P_COLS), np.float32)
    col_scale[0, C_DQ:C_DK] = (DIFF_QK_DIM ** -0.5) * LOG2E
    col_scale[0, C_NQ:C_NKV] = (NSA_DIM ** -0.5) * LOG2E
    col_scale = jnp.asarray(col_scale)

    w_in_t = jnp.swapaxes(w_in, 1, 2)
    w_tail, w_ng = _tail_weights(w_in_t)

    xt = x.reshape(t, d)
    memt = mem.reshape(batch * mem_len, d)
    fg = final_norm.reshape(1, d)
    bf = lambda a: a.astype(BF16)

    for l in range(depth):
        lam_init = 0.8 - 0.6 * math.exp(-0.3 * l)
        xt, hn = _ffn(xt, ffn1_norm[l].reshape(1, d), ffn1_w_gate, ffn1_w_up, ffn1_w_down,
                      mix_norm[l].reshape(1, d), l, "normed")

        proj, ngt = _inproj(hn, w_in_t, l, w_tail, w_ng, col_scale)

        o_diff = _diff_attention(proj, diff_lambda[l], diff_subln[l].reshape(1, DIFF_V_DIM),
                                 batch, seq, lam_init)

        ckv = proj[:, C_NKV:C_NKV + 2 * NSA_GROUPS * NSA_DIM]
        ckv = ckv.reshape(batch, n_rows, CMP_STRIDE, 2, NSA_GROUPS, NSA_DIM)
        ckv = ckv.transpose(3, 4, 0, 1, 2, 5).reshape(2, NSA_GROUPS, batch, n_rows, CMP_STRIDE * NSA_DIM)
        pos = bf(jnp.broadcast_to(nsa_cmp_pos[l].reshape(2, 1, CMP_LEN * NSA_DIM), (2, 8, CMP_LEN * NSA_DIM)))
        kv_cmp, kvt_cmp = _compress(ckv, pos, bf(nsa_cmp_w1[l]), bf(nsa_cmp_w2[l]))
        o_nsa = _nsa_attention(proj, ngt, kv_cmp, kvt_cmp, overlap_t, batch, seq)

        memkv = _memkv(memt, mem_norm[l].reshape(1, d), bf(w_mem_kv[l]))
        o_mem = _mem_attention(proj, memkv, batch, seq, mem_len)

        xt = _merge(xt, hn, o_diff, o_nsa, o_mem, w_tail, l, bf(w_up_diff[l]), bf(w_up_nsa[l]),
                    bf(w_up_mem[l]), bf(w_out[l]))

        (xt,) = _ffn(xt, ffn2_norm[l].reshape(1, d), ffn2_w_gate, ffn2_w_up, ffn2_w_down, fg, l,
                     "final" if l == depth - 1 else "plain")
    return xt.reshape(batch, seq, d)
```

```python
import functools
import math

import jax
import jax.numpy as jnp
import numpy as np
from jax import lax
from jax.experimental import pallas as pl
from jax.experimental.pallas import tpu as pltpu

F32 = jnp.float32
BF16 = jnp.bfloat16

NORM_EPS = 1e-6
NEG_INF = -1e30
SEL_BIG = 1e9
LOG2E = math.log2(math.e)

DIFF_HEADS = 8
DIFF_QK_DIM = 64
DIFF_V_DIM = 2 * DIFF_QK_DIM

NSA_HEADS = 8
NSA_GROUPS = 2
NSA_HPG = NSA_HEADS // NSA_GROUPS
NSA_DIM = 128
CMP_LEN = 32
CMP_STRIDE = 16
SLC_LEN = 64
SLC_TOPK = 8
WIN_LEN = 512

MEM_HEADS = 4
MEM_DIM = 128
N_BRANCH = 3

LANES = 128
VMEM_LIMIT = 56 * 1024 * 1024
DIFF_HEADS_PER_STEP = 2

DIFF_QK_COLS = DIFF_HEADS * 2 * DIFF_QK_DIM
DIFF_V_COLS = DIFF_HEADS * DIFF_V_DIM
NSA_Q_COLS = NSA_HEADS * NSA_DIM
NSA_KV_COLS = 3 * 2 * NSA_GROUPS * NSA_DIM
NSA_GATE_COLS = NSA_HEADS * 3
MEM_Q_COLS = MEM_HEADS * MEM_DIM
C_DQ = 0
C_DK = C_DQ + DIFF_QK_COLS
C_DV = C_DK + DIFF_QK_COLS
C_NQ = C_DV + DIFF_V_COLS
C_NKV = C_NQ + NSA_Q_COLS
C_NG = C_NKV + NSA_KV_COLS
C_MQ = C_NG + NSA_GATE_COLS
C_MG = C_MQ + MEM_Q_COLS
P_MQ = C_NG
P_COLS = P_MQ + MEM_Q_COLS


def _alibi_slopes(n_heads):
    return np.array([2.0 ** (-8.0 * (h + 1) / n_heads) for h in range(n_heads)], dtype=np.float32)


def _rmsnorm(x, g):
    y = x * lax.rsqrt(jnp.mean(x * x, axis=-1, keepdims=True) + NORM_EPS)
    return y * g


def _dot(a, b):
    return jnp.dot(a, b, preferred_element_type=F32)


def _dot_nt(a, b):
    return lax.dot_general(a, b, (((1,), (1,)), ((), ())), preferred_element_type=F32)


def _transpose_bf16(x):
    return x.astype(F32).T.astype(BF16)


def _params(*sem):
    return pltpu.CompilerParams(dimension_semantics=sem, vmem_limit_bytes=VMEM_LIMIT)


def _pick(n, pref):
    t = min(n, pref)
    assert n % t == 0, (n, t)
    return t


def _ffn_kernel(x_ref, g_ref, wg_ref, wu_ref, wd_ref, fg_ref, o_ref, *rest, tail):
    xn_ref = rest[-1]
    j = pl.program_id(1)

    @pl.when(j == 0)
    def _():
        xn_ref[...] = _rmsnorm(x_ref[...], g_ref[...]).astype(BF16)
        o_ref[...] = jnp.zeros_like(o_ref)

    xn = xn_ref[...]
    a = _dot(xn, wg_ref[...].astype(BF16))
    u = _dot(xn, wu_ref[...].astype(BF16))
    h = (a * jax.nn.sigmoid(a)) * u
    o_ref[...] += _dot(h.astype(BF16), wd_ref[...].astype(BF16))

    @pl.when(j == pl.num_programs(1) - 1)
    def _():
        rows = 128

        def chunk(c, carry):
            r = pl.ds(pl.multiple_of(c * rows, rows), rows)
            y = x_ref[r, :] + 0.5 * o_ref[r, :]
            if tail == "final":
                y = _rmsnorm(y, fg_ref[...])
            if tail == "normed":
                rest[0][r, :] = _rmsnorm(y, fg_ref[...]).astype(BF16)
            o_ref[r, :] = y
            return carry

        lax.fori_loop(0, o_ref.shape[0] // rows, chunk, 0)


def _ffn(x, g, wg, wu, wd, fg, layer, tail):
    t, d = x.shape
    f = wg.shape[2]
    tm = _pick(t, 1024)
    tf = _pick(f, 256)
    out_specs = [pl.BlockSpec((tm, d), lambda i, j: (i, 0))]
    out_shape = [jax.ShapeDtypeStruct((t, d), F32)]
    if tail == "normed":
        out_specs.append(pl.BlockSpec((tm, d), lambda i, j: (i, 0)))
        out_shape.append(jax.ShapeDtypeStruct((t, d), BF16))
    return pl.pallas_call(
        functools.partial(_ffn_kernel, tail=tail),
        grid=(t // tm, f // tf),
        in_specs=[
            pl.BlockSpec((tm, d), lambda i, j: (i, 0)),
            pl.BlockSpec((1, d), lambda i, j: (0, 0)),
            pl.BlockSpec((None, d, tf), lambda i, j: (layer, 0, j)),
            pl.BlockSpec((None, d, tf), lambda i, j: (layer, 0, j)),
            pl.BlockSpec((None, tf, d), lambda i, j: (layer, j, 0)),
            pl.BlockSpec((1, d), lambda i, j: (0, 0)),
        ],
        out_specs=out_specs,
        out_shape=out_shape,
        scratch_shapes=[] if tail == "normed" else [pltpu.VMEM((tm, d), BF16)],
        compiler_params=_params("parallel", "arbitrary"),
        name="ffn",
    )(x, g, wg, wu, wd, fg)


def _tail_kernel(w_ref, tail_ref, ng_ref, carry_ref, *, shift):
    s = pl.program_id(1)
    blk = w_ref[...]

    @pl.when(s == 0)
    def _():
        pad = jnp.zeros((ng_ref.shape[0] - shift, blk.shape[1]), F32)
        ng_ref[...] = jnp.concatenate([blk[:shift], pad], axis=0).astype(BF16)

    @pl.when(s > 0)
    def _():
        tail_ref[...] = jnp.concatenate([carry_ref[...], blk[:shift]], axis=0).astype(BF16)

    carry_ref[...] = blk[shift:]


def _tail_weights(w_in_t):
    depth, rows, d = w_in_t.shape
    tb = MEM_Q_COLS
    shift = C_MQ - C_NG
    n_out = -(-(rows - C_MQ) // tb)
    assert C_NG % tb == 0 and shift % 8 == 0 and shift <= LANES
    first, last = C_NG // tb, (rows - 1) // tb
    return pl.pallas_call(
        functools.partial(_tail_kernel, shift=shift),
        grid=(depth, n_out + 1),
        in_specs=[pl.BlockSpec((None, tb, d), lambda l, s: (l, jnp.minimum(first + s, last), 0))],
        out_specs=[
            pl.BlockSpec((None, tb, d), lambda l, s: (l, jnp.maximum(s - 1, 0), 0)),
            pl.BlockSpec((None, LANES, d), lambda l, s: (l, 0, 0)),
        ],
        out_shape=[
            jax.ShapeDtypeStruct((depth, n_out * tb, d), BF16),
            jax.ShapeDtypeStruct((depth, LANES, d), BF16),
        ],
        scratch_shapes=[pltpu.VMEM((tb - shift, d), F32)],
        compiler_params=_params("parallel", "arbitrary"),
        name="tail_weights",
    )(w_in_t)


def _inproj_kernel(hn_ref, w_ref, wmq_ref, wng_ref, cs_ref, proj_ref, ngt_ref, *, n_main):
    j = pl.program_id(1)

    @pl.when(j == 0)
    def _():
        ngt_ref[...] = jax.nn.sigmoid(_dot_nt(wng_ref[...], hn_ref[...]))

    @pl.when(j < n_main)
    def _():
        proj_ref[...] = (_dot_nt(hn_ref[...], w_ref[...].astype(BF16)) * cs_ref[...]).astype(BF16)

    @pl.when(j >= n_main)
    def _():
        proj_ref[...] = (_dot_nt(hn_ref[...], wmq_ref[...]) * cs_ref[...]).astype(BF16)


def _inproj(hn, w_in_t, layer, tail, wng, cs):
    t, d = hn.shape
    n = P_COLS
    tm = _pick(t, 2048)
    tn = MEM_Q_COLS
    assert P_MQ % tn == 0
    n_main = P_MQ // tn
    return pl.pallas_call(
        functools.partial(_inproj_kernel, n_main=n_main),
        grid=(t // tm, n // tn),
        in_specs=[
            pl.BlockSpec((tm, d), lambda i, j: (i, 0)),
            pl.BlockSpec((None, tn, d), lambda i, j: (layer, jnp.minimum(j, n_main - 1), 0)),
            pl.BlockSpec((None, tn, d), lambda i, j: (layer, 0, 0)),
            pl.BlockSpec((None, LANES, d), lambda i, j: (layer, 0, 0)),
            pl.BlockSpec((1, tn), lambda i, j: (0, j)),
        ],
        out_specs=[
            pl.BlockSpec((tm, tn), lambda i, j: (i, j)),
            pl.BlockSpec((LANES, tm), lambda i, j: (0, i)),
        ],
        out_shape=[
            jax.ShapeDtypeStruct((t, n), BF16),
            jax.ShapeDtypeStruct((LANES, t), F32),
        ],
        compiler_params=_params("parallel", "arbitrary"),
        name="inproj",
    )(hn, w_in_t, tail, wng, cs)


SUB = 128
ROWS_L = 16


def _assert_slopes_fit(slopes2):
    assert float(np.max(slopes2)) * SUB < 100.0, "ALiBi factor per bias block leaves the safe f32 range"


def _bias_factor_row(slope2, width):
    r = lax.broadcasted_iota(jnp.int32, (1, width), 1) & (SUB - 1)
    return jnp.exp2(slope2 * r.astype(F32))


def _scaled_vt(vt, w_row):
    ones = jnp.broadcast_to(w_row, (ROWS_L, w_row.shape[1]))
    return jnp.concatenate([vt * w_row, ones], axis=0).astype(BF16)


def _split_acc(acc, dv):
    return acc[0:dv] * (1.0 / jnp.maximum(acc[dv:dv + 1], 1e-30))


def _diff_kernel(slopes_ref, q_ref, k_ref, v_ref, lp_ref, sg_ref, o_ref,
                 vt_ref, pen_ref, z_ref, mrow_ref, alpha_ref, m_ref, acc_ref, *, tile, n_tiles, lam_init):
    hb = pl.program_id(1)
    qi = pl.program_id(2)
    hps = DIFF_HEADS_PER_STEP
    chains = range(2 * hps)
    slope2 = [slopes_ref[hb * hps + hh] for hh in range(hps)]
    nb = tile // SUB
    dh = 2 * DIFF_QK_DIM

    @pl.when(qi == 0)
    def _():
        for hh in range(hps):
            w_row = _bias_factor_row(slope2[hh], tile)
            for t in range(n_tiles):
                v = v_ref[t * tile:(t + 1) * tile, hh * DIFF_V_DIM:(hh + 1) * DIFF_V_DIM]
                vt_ref[hh, t] = _scaled_vt(v.astype(F32).T, w_row)
        row = lax.broadcasted_iota(jnp.int32, (tile, tile), 0)
        col = lax.broadcasted_iota(jnp.int32, (tile, tile), 1)
        pen_ref[0] = jnp.zeros((tile, tile), F32)
        pen_ref[1] = jnp.where(row <= col, 0.0, NEG_INF)

    qts = []
    for hh in range(hps):
        qt = _transpose_bf16(q_ref[:, hh * dh:(hh + 1) * dh])
        feat = lax.broadcasted_iota(jnp.int32, qt.shape, 0)
        qts += [jnp.where((feat >= m * DIFF_QK_DIM) & (feat < (m + 1) * DIFF_QK_DIM), qt, 0.0).astype(BF16)
                for m in range(2)]

    m_ref[...] = jnp.full_like(m_ref, NEG_INF)
    acc_ref[...] = jnp.zeros_like(acc_ref)

    def block_off(t, c, j):
        s2 = slope2[c // 2]
        return s2 * ((t - qi) * tile).astype(F32) + j * (s2 * float(SUB))

    def scores(t, c):
        ks = pl.multiple_of(t * tile, tile)
        hh = c // 2
        k = k_ref[pl.ds(ks, tile), hh * dh:(hh + 1) * dh]
        return _dot(k, qts[c]) + pen_ref[(t == qi).astype(jnp.int32)]

    def bookkeep(t, slot, c, z):
        z_ref[slot, c] = z
        m_old = m_ref[c]
        m_new = m_old
        for j in range(nb):
            m_new = jnp.maximum(m_new, jnp.max(z[j * SUB:(j + 1) * SUB], axis=0, keepdims=True)
                                + block_off(t, c, j))
        m_ref[c] = m_new
        mrow_ref[slot, c] = m_new
        alpha_ref[slot, c] = jnp.exp2(m_old - m_new)

    def weights(t, slot, c):
        m_new = mrow_ref[slot, c]
        return jnp.concatenate(
            [jnp.exp2(z_ref[slot, c, j * SUB:(j + 1) * SUB, :] - (m_new - block_off(t, c, j))).astype(BF16)
             for j in range(nb)], axis=0)

    def accumulate(t, slot, c, p):
        acc_ref[c] = alpha_ref[slot, c] * acc_ref[c] + _dot(vt_ref[c // 2, t], p)

    def overlapped(t, slot):
        for c in chains:
            z_next = scores(t + 1, c)
            p = weights(t, slot, c)
            bookkeep(t + 1, 1 - slot, c, z_next)
            accumulate(t, slot, c, p)

    def drain(t, slot):
        for c in chains:
            accumulate(t, slot, c, weights(t, slot, c))

    for c in chains:
        bookkeep(0, 0, c, scores(0, c))

    def body(u, carry):
        overlapped(2 * u, 0)
        overlapped(2 * u + 1, 1)
        return carry

    lax.fori_loop(0, qi // 2, body, 0)

    @pl.when(qi % 2 == 1)
    def _():
        overlapped(qi - 1, 0)
        drain(qi, 1)

    @pl.when(qi % 2 == 0)
    def _():
        drain(qi, 0)

    lp = lp_ref[...]
    lam = (jnp.exp(jnp.sum(lp[0:1] * lp[1:2], axis=-1, keepdims=True))
           - jnp.exp(jnp.sum(lp[2:3] * lp[3:4], axis=-1, keepdims=True)) + lam_init)
    outs = []
    for hh in range(hps):
        o1 = _split_acc(acc_ref[2 * hh], DIFF_V_DIM)
        o2 = _split_acc(acc_ref[2 * hh + 1], DIFF_V_DIM)
        ot = o1 - lam * o2
        ot = ot * lax.rsqrt(jnp.mean(ot * ot, axis=0, keepdims=True) + NORM_EPS)
        outs.append(ot.T * sg_ref[...] * (1.0 - lam_init))
    o_ref[...] = jnp.concatenate(outs, axis=1).astype(BF16)


def _diff_attention(proj, lp, sg, batch, seq, lam_init):
    t = proj.shape[0]
    tile = _pick(seq, 512)
    nq = seq // tile
    hps = DIFF_HEADS_PER_STEP
    width = hps * 2 * DIFF_QK_DIM
    assert DIFF_HEADS % hps == 0 and DIFF_V_DIM == 2 * DIFF_QK_DIM and tile % SUB == 0
    kb, vb = C_DK // width, C_DV // width
    slopes2 = _alibi_slopes(DIFF_HEADS) * np.float32(LOG2E)
    _assert_slopes_fit(slopes2)
    slopes = jnp.asarray(slopes2)
    return pl.pallas_call(
        functools.partial(_diff_kernel, tile=tile, n_tiles=nq, lam_init=lam_init),
        grid=(batch, DIFF_HEADS // hps, nq),
        in_specs=[
            pl.BlockSpec(memory_space=pltpu.SMEM),
            pl.BlockSpec((tile, width), lambda b, h, i: (b * nq + i, h)),
            pl.BlockSpec((seq, width), lambda b, h, i: (b, kb + h)),
            pl.BlockSpec((seq, width), lambda b, h, i: (b, vb + h)),
            pl.BlockSpec((4, DIFF_QK_DIM), lambda b, h, i: (0, 0)),
            pl.BlockSpec((1, DIFF_V_DIM), lambda b, h, i: (0, 0)),
        ],
        out_specs=pl.BlockSpec((tile, width), lambda b, h, i: (b * nq + i, h)),
        out_shape=jax.ShapeDtypeStruct((t, DIFF_HEADS * DIFF_V_DIM), BF16),
        scratch_shapes=[
            pltpu.VMEM((hps, nq, DIFF_V_DIM + ROWS_L, tile), BF16),
            pltpu.VMEM((2, tile, tile), F32),
            pltpu.VMEM((2, 2 * hps, tile, tile), F32),
            pltpu.VMEM((2, 2 * hps, 1, tile), F32),
            pltpu.VMEM((2, 2 * hps, 1, tile), F32),
            pltpu.VMEM((2 * hps, 1, tile), F32),
            pltpu.VMEM((2 * hps, DIFF_V_DIM + ROWS_L, tile), F32),
        ],
        compiler_params=_params("parallel", "parallel", "arbitrary"),
        name="diff_attn",
    )(slopes, proj, proj, proj, lp, sg)


def _compress_kernel(kv_ref, pos_ref, w1_ref, w2_ref, o_ref, ot_ref):
    half = w1_ref.shape[0] // 2
    kv = kv_ref[...]
    top = _dot(kv, w1_ref[:half, :])
    bot = _dot(kv, w1_ref[half:, :])
    rows = kv.shape[0]
    bot_next = pltpu.roll(bot, rows - 1, 0)
    pos_term = _dot(pos_ref[...], w1_ref[...])[0:1]
    hid = jax.nn.gelu(top + bot_next + pos_term)
    out = _dot(hid.astype(BF16), w2_ref[...])
    o_ref[...] = out.astype(BF16)
    ot_ref[...] = out.T.astype(BF16)


def _compress(kvr, pos, w1, w2):
    _, groups, batch, rows, width = kvr.shape
    d = w2.shape[-1]
    return pl.pallas_call(
        _compress_kernel,
        grid=(2, groups, batch),
        in_specs=[
            pl.BlockSpec((None, None, None, rows, width), lambda a, g, b: (a, g, b, 0, 0)),
            pl.BlockSpec((None, 8, 2 * width), lambda a, g, b: (a, 0, 0)),
            pl.BlockSpec((None, 2 * width, d), lambda a, g, b: (a, 0, 0)),
            pl.BlockSpec((None, d, d), lambda a, g, b: (a, 0, 0)),
        ],
        out_specs=[
            pl.BlockSpec((None, None, None, rows, d), lambda a, g, b: (a, b, g, 0, 0)),
            pl.BlockSpec((None, None, None, d, rows), lambda a, g, b: (a, b, g, 0, 0)),
        ],
        out_shape=[
            jax.ShapeDtypeStruct((2, batch, groups, rows, d), BF16),
            jax.ShapeDtypeStruct((2, batch, groups, d, rows), BF16),
        ],
        compiler_params=_params("parallel", "parallel", "parallel"),
        name="compress",
    )(kvr, pos, w1, w2)


def _nsa_kernel(slopes_ref, q_ref, gate_ref, kc_ref, vct_ref, ks_ref, vs_ref, kw_ref, vw_ref,
                ovt_ref, o_ref, vst_ref, vwt_ref, base_ref, z_ref, mrow_ref, alpha_ref, m_ref, acc_ref,
                *, tq, tk, seq):
    g = pl.program_id(1)
    qi = pl.program_id(2)
    q0 = qi * tq
    hp = NSA_HPG
    n = hp * tq
    dv = NSA_DIM
    span = WIN_LEN + tq
    n_slc = seq // SLC_LEN
    slope_row = jnp.concatenate(
        [jnp.full((1, tq), slopes_ref[g * hp + h], F32) for h in range(hp)], axis=1)

    @pl.when(qi == 0)
    def _():
        ones_s = jnp.ones((1, tk), F32)
        ones_w = jnp.ones((1, tq), F32)
        for t in range(seq // tk):
            vst_ref[t] = _scaled_vt(vs_ref[t * tk:(t + 1) * tk, :].astype(F32).T, ones_s)
        for t in range(seq // tq):
            vwt_ref[t] = _scaled_vt(vw_ref[t * tq:(t + 1) * tq, :].astype(F32).T, ones_w)
        key_row = lax.broadcasted_iota(jnp.int32, (span, tq), 0).astype(F32)
        for h in range(hp):
            base_ref[h] = slopes_ref[g * hp + h] * key_row

    q = q_ref[...]
    q4t = jnp.concatenate(
        [q[:, h * NSA_DIM:(h + 1) * NSA_DIM].astype(F32).T for h in range(hp)], axis=1).astype(BF16)
    qpos_all = q0 + (lax.broadcasted_iota(jnp.int32, (1, n), 1) & (tq - 1))
    qpos = q0 + lax.broadcasted_iota(jnp.int32, (1, tq), 1)

    n_cmp = kc_ref.shape[0]
    cmp_end = lax.broadcasted_iota(jnp.int32, (n_cmp, 1), 0) * CMP_STRIDE + (CMP_LEN - 1)
    mask_c = cmp_end <= qpos_all
    z_c = jnp.where(mask_c, _dot(kc_ref[...], q4t), NEG_INF)
    m_c = jnp.max(z_c, axis=0, keepdims=True)
    e_c = jnp.where(mask_c, jnp.exp2(z_c - m_c), 0.0)
    p_c = e_c * (1.0 / jnp.maximum(jnp.sum(e_c, axis=0, keepdims=True), 1e-30))
    o_c = _dot(vct_ref[...], p_c.astype(BF16))

    wt = jnp.maximum(qi - WIN_LEN // tq, 0)
    w0 = pl.multiple_of(wt * tq, tq)
    dist = qpos - (w0 + lax.broadcasted_iota(jnp.int32, (span, 1), 0))
    in_win = (dist >= 0) & (dist < WIN_LEN)
    pen_w = jnp.concatenate([jnp.where(in_win, base_ref[h], NEG_INF) for h in range(hp)], axis=1)
    z_w = _dot(kw_ref[pl.ds(w0, span), :], q4t) + pen_w
    p_w = jnp.exp2(z_w - jnp.max(z_w, axis=0, keepdims=True)).astype(BF16)
    vw_cat = jnp.concatenate([vwt_ref[wt + j] for j in range(span // tq)], axis=1)
    o_w = _split_acc(_dot(vw_cat, p_w), dv)

    p_sum = p_c[:, 0:tq]
    for h in range(1, hp):
        p_sum = p_sum + p_c[:, h * tq:(h + 1) * tq]
    ovt = ovt_ref[...]
    p_hi = p_sum.astype(BF16)
    r1 = p_sum - p_hi.astype(F32)
    p_mid = r1.astype(BF16)
    p_lo = (r1 - p_mid.astype(F32)).astype(BF16)
    imp = _dot(ovt, p_hi) + _dot(ovt, p_mid) + _dot(ovt, p_lo)
    jblk = lax.broadcasted_iota(jnp.int32, (n_slc, tq), 0)
    cur = qpos // SLC_LEN
    forced = (jblk == 0) | (jblk == cur) | (jblk == cur - 1)
    imp = jnp.where(forced, SEL_BIG, jnp.where(jblk > cur, -SEL_BIG, imp))
    jblk_f = jblk.astype(F32)
    sel = jnp.zeros((n_slc, tq), F32)
    for _ in range(min(SLC_TOPK, n_slc)):
        mx = jnp.max(imp, axis=0, keepdims=True)
        first = jnp.min(jnp.where(imp == mx, jblk_f, float(n_slc)), axis=0, keepdims=True)
        hit = jblk_f == first
        sel = jnp.where(hit, 1.0, sel)
        imp = jnp.where(hit, -jnp.inf, imp)
    sel_bf = sel.astype(BF16)

    m_ref[...] = jnp.full_like(m_ref, NEG_INF)
    acc_ref[...] = jnp.zeros_like(acc_ref)
    exp_blk = lax.broadcasted_iota(jnp.int32, (tk, n_slc), 0) // SLC_LEN
    exp_col = lax.broadcasted_iota(jnp.int32, (tk, n_slc), 1)
    krow = lax.broadcasted_iota(jnp.int32, (tk, 1), 0)
    assert hp == 4
    half = 2 * tq

    def tile_off(t, i):
        return slope_row[:, i * half:(i + 1) * half] * (t * tk - q0).astype(F32)

    def penalty(t):
        expand = jnp.where(exp_col == exp_blk + t * (tk // SLC_LEN), 1.0, 0.0).astype(BF16)
        chosen = _dot(expand, sel_bf)
        keep = (chosen > 0.5) & ((krow + t * tk) <= qpos)
        return [jnp.concatenate([jnp.where(keep, base_ref[2 * i + j, 0:tk, :], NEG_INF)
                                 for j in range(2)], axis=1) for i in range(2)]

    def scores(t, i, pen):
        ks = pl.multiple_of(t * tk, tk)
        return _dot(ks_ref[pl.ds(ks, tk), :], q4t[:, i * half:(i + 1) * half]) + pen[i]

    def bookkeep(t, slot, i, z):
        z_ref[slot, i] = z
        m_old = m_ref[i]
        m_new = jnp.maximum(m_old, jnp.max(z, axis=0, keepdims=True) + tile_off(t, i))
        m_ref[i] = m_new
        mrow_ref[slot, i] = m_new
        alpha_ref[slot, i] = jnp.exp2(m_old - m_new)

    def weights(t, slot, i):
        return jnp.exp2(z_ref[slot, i] - (mrow_ref[slot, i] - tile_off(t, i))).astype(BF16)

    def accumulate(t, slot, i, p):
        acc_ref[i] = alpha_ref[slot, i] * acc_ref[i] + _dot(vst_ref[t], p)

    def overlapped(t, slot):
        pen = penalty(t + 1)
        for i in range(2):
            z_next = scores(t + 1, i, pen)
            p = weights(t, slot, i)
            bookkeep(t + 1, 1 - slot, i, z_next)
            accumulate(t, slot, i, p)

    def drain(t, slot):
        for i in range(2):
            accumulate(t, slot, i, weights(t, slot, i))

    pen = penalty(0)
    for i in range(2):
        bookkeep(0, 0, i, scores(0, i, pen))
    n_last = q0 // tk

    def body(u, carry):
        overlapped(2 * u, 0)
        overlapped(2 * u + 1, 1)
        return carry

    lax.fori_loop(0, n_last // 2, body, 0)

    @pl.when(n_last % 2 == 1)
    def _():
        overlapped(n_last - 1, 0)
        drain(n_last, 1)

    @pl.when(n_last % 2 == 0)
    def _():
        drain(n_last, 0)

    o_s = jnp.concatenate([_split_acc(acc_ref[i], dv) for i in range(2)], axis=1)

    def gate_row(branch):
        return jnp.concatenate(
            [gate_ref[pl.ds((g * hp + h) * 3 + branch, 1), :] for h in range(hp)], axis=1)

    ot = gate_row(0) * o_c + gate_row(1) * o_s + gate_row(2) * o_w
    o_ref[...] = jnp.concatenate(
        [ot[:, h * tq:(h + 1) * tq].T for h in range(hp)], axis=1).astype(BF16)


def _nsa_attention(proj, gates_t, k_cmp, vt_cmp, overlap_t, batch, seq):
    t = proj.shape[0]
    tq = _pick(seq, 256)
    tk = _pick(seq, 512)
    assert seq >= WIN_LEN + tq and WIN_LEN % tq == 0 and tk % tq == 0
    nq = seq // tq
    hp = NSA_HPG
    qb = C_NQ // (hp * NSA_DIM)
    kvb = C_NKV // LANES
    n_cmp = k_cmp.shape[3]
    slopes2 = _alibi_slopes(NSA_HEADS) * np.float32(LOG2E)
    _assert_slopes_fit(slopes2)
    slopes = jnp.asarray(slopes2)

    def kv_spec(branch, which):
        base = kvb + (branch * 2 + which) * NSA_GROUPS
        return pl.BlockSpec((seq, LANES), lambda b, g, i: (b, base + g))

    return pl.pallas_call(
        functools.partial(_nsa_kernel, tq=tq, tk=tk, seq=seq),
        grid=(batch, NSA_GROUPS, nq),
        in_specs=[
            pl.BlockSpec(memory_space=pltpu.SMEM),
            pl.BlockSpec((tq, hp * NSA_DIM), lambda b, g, i: (b * nq + i, qb + g)),
            pl.BlockSpec((LANES, tq), lambda b, g, i: (0, b * nq + i)),
            pl.BlockSpec((None, None, None, n_cmp, NSA_DIM), lambda b, g, i: (0, b, g, 0, 0)),
            pl.BlockSpec((None, None, None, NSA_DIM, n_cmp), lambda b, g, i: (1, b, g, 0, 0)),
            kv_spec(1, 0), kv_spec(1, 1),
            kv_spec(2, 0), kv_spec(2, 1),
            pl.BlockSpec(overlap_t.shape, lambda b, g, i: (0, 0)),
        ],
        out_specs=pl.BlockSpec((tq, hp * NSA_DIM), lambda b, g, i: (b * nq + i, g)),
        out_shape=jax.ShapeDtypeStruct((t, NSA_HEADS * NSA_DIM), BF16),
        scratch_shapes=[
            pltpu.VMEM((seq // tk, NSA_DIM + ROWS_L, tk), BF16),
            pltpu.VMEM((seq // tq, NSA_DIM + ROWS_L, tq), BF16),
            pltpu.VMEM((hp, WIN_LEN + tq, tq), F32),
            pltpu.VMEM((2, 2, tk, 2 * tq), F32),
            pltpu.VMEM((2, 2, 1, 2 * tq), F32),
            pltpu.VMEM((2, 2, 1, 2 * tq), F32),
            pltpu.VMEM((2, 1, 2 * tq), F32),
            pltpu.VMEM((2, NSA_DIM + ROWS_L, 2 * tq), F32),
        ],
        compiler_params=_params("parallel", "parallel", "arbitrary"),
        name="nsa_attn",
    )(slopes, proj, gates_t, k_cmp, vt_cmp, proj, proj, proj, proj, overlap_t)


def _memkv_kernel(mem_ref, g_ref, w_ref, o_ref):
    mn = _rmsnorm(mem_ref[...], g_ref[...]).astype(BF16)
    o_ref[...] = _dot(mn, w_ref[...]).astype(BF16)


def _memkv(mem, g, w):
    r, d = mem.shape
    n = w.shape[1]
    tm = _pick(r, 256)
    return pl.pallas_call(
        _memkv_kernel,
        grid=(r // tm,),
        in_specs=[
            pl.BlockSpec((tm, d), lambda i: (i, 0)),
            pl.BlockSpec((1, d), lambda i: (0, 0)),
            pl.BlockSpec((d, n), lambda i: (0, 0)),
        ],
        out_specs=pl.BlockSpec((tm, n), lambda i: (i, 0)),
        out_shape=jax.ShapeDtypeStruct((r, n), BF16),
        compiler_params=_params("parallel"),
        name="mem_kv",
    )(mem, g, w)


def _memattn_kernel(q_ref, k_ref, v_ref, o_ref):
    scale = MEM_DIM ** -0.5
    q = q_ref[...]
    k = k_ref[...]
    v = v_ref[...]
    outs = []
    for h in range(MEM_HEADS):
        sl = slice(h * MEM_DIM, (h + 1) * MEM_DIM)
        s = _dot_nt(q[:, sl], k[:, sl]) * scale
        m = jnp.max(s, axis=-1, keepdims=True)
        e = jnp.exp(s - m)
        p = e / jnp.sum(e, axis=-1, keepdims=True)
        outs.append(_dot(p.astype(BF16), v[:, sl]))
    o_ref[...] = jnp.concatenate(outs, axis=-1).astype(BF16)


def _mem_attention(proj, memkv, batch, seq, mem_len):
    t = proj.shape[0]
    tq = _pick(seq, 512)
    nq = seq // tq
    width = MEM_HEADS * MEM_DIM
    qb = P_MQ // width
    return pl.pallas_call(
        _memattn_kernel,
        grid=(batch, nq),
        in_specs=[
            pl.BlockSpec((tq, width), lambda b, i: (b * nq + i, qb)),
            pl.BlockSpec((mem_len, width), lambda b, i: (b, 0)),
            pl.BlockSpec((mem_len, width), lambda b, i: (b, 1)),
        ],
        out_specs=pl.BlockSpec((tq, width), lambda b, i: (b * nq + i, 0)),
        out_shape=jax.ShapeDtypeStruct((t, width), BF16),
        compiler_params=_params("parallel", "parallel"),
        name="mem_attn",
    )(proj, memkv, memkv)


def _merge_kernel(x_ref, hn_ref, od_ref, on_ref, om_ref, wgd_ref, wgn_ref, wgm_ref,
                  wud_ref, wun_ref, wum_ref, wo_ref, o_ref, acc_ref):
    j = pl.program_id(1)

    @pl.when(j == 0)
    def _():
        acc_ref[...] = jnp.zeros_like(acc_ref)

    hn = hn_ref[...]
    merged = jax.nn.sigmoid(_dot_nt(hn, wgd_ref[...])) * _dot(od_ref[...], wud_ref[...])
    merged += jax.nn.sigmoid(_dot_nt(hn, wgn_ref[...])) * _dot(on_ref[...], wun_ref[...])
    merged += jax.nn.sigmoid(_dot_nt(hn, wgm_ref[...])) * _dot(om_ref[...], wum_ref[...])
    acc_ref[...] += _dot(merged.astype(BF16), wo_ref[...])

    @pl.when(j == pl.num_programs(1) - 1)
    def _():
        o_ref[...] = x_ref[...] + acc_ref[...]


def _merge(x, hn, od, on, om, tail, layer, wud, wun, wum, wo):
    t, d = x.shape
    tm = _pick(t, 512)
    tj = _pick(d, 512)
    assert MEM_Q_COLS % tj == 0

    def gate_spec(branch):
        first = (MEM_Q_COLS + branch * d) // tj
        return pl.BlockSpec((None, tj, d), lambda i, j: (layer, first + j, 0))

    return pl.pallas_call(
        _merge_kernel,
        grid=(t // tm, d // tj),
        in_specs=[
            pl.BlockSpec((tm, d), lambda i, j: (i, 0)),
            pl.BlockSpec((tm, d), lambda i, j: (i, 0)),
            pl.BlockSpec((tm, od.shape[1]), lambda i, j: (i, 0)),
            pl.BlockSpec((tm, on.shape[1]), lambda i, j: (i, 0)),
            pl.BlockSpec((tm, om.shape[1]), lambda i, j: (i, 0)),
            gate_spec(0), gate_spec(1), gate_spec(2),
            pl.BlockSpec((wud.shape[0], tj), lambda i, j: (0, j)),
            pl.BlockSpec((wun.shape[0], tj), lambda i, j: (0, j)),
            pl.BlockSpec((wum.shape[0], tj), lambda i, j: (0, j)),
            pl.BlockSpec((tj, d), lambda i, j: (j, 0)),
        ],
        out_specs=pl.BlockSpec((tm, d), lambda i, j: (i, 0)),
        out_shape=jax.ShapeDtypeStruct((t, d), F32),
        scratch_shapes=[pltpu.VMEM((tm, d), F32)],
        compiler_params=_params("parallel", "arbitrary"),
        name="merge",
    )(x, hn, od, on, om, tail, tail, tail, wud, wun, wum, wo)


def _overlap_matrix_t(n_slc, n_cmp):
    cmp_start = np.arange(n_cmp)[None, :] * CMP_STRIDE
    slc_start = np.arange(n_slc)[:, None] * SLC_LEN
    return ((cmp_start < slc_start + SLC_LEN) & (cmp_start + CMP_LEN > slc_start)).astype(np.float32)


def kernel(x, mem, ffn1_norm, ffn1_w_gate, ffn1_w_up, ffn1_w_down, mix_norm, w_in, diff_lambda, diff_subln, nsa_cmp_pos, nsa_cmp_w1, nsa_cmp_w2, mem_norm, w_mem_kv, w_up_diff, w_up_nsa, w_up_mem, w_out, ffn2_norm, ffn2_w_gate, ffn2_w_up, ffn2_w_down, final_norm):
    batch, seq, d = x.shape
    mem_len = mem.shape[1]
    depth = w_in.shape[0]
    t = batch * seq
    assert seq % CMP_STRIDE == 0 and CMP_LEN == 2 * CMP_STRIDE
    n_rows = seq // CMP_STRIDE
    n_slc = seq // SLC_LEN
    overlap_t = jnp.asarray(_overlap_matrix_t(n_slc, n_rows), BF16)

    col_scale = np.ones((1, P_COLS), np.float32)
    col_scale[0, C_DQ:C_DK] = (DIFF_QK_DIM ** -0.5) * LOG2E
    col_scale[0, C_NQ:C_NKV] = (NSA_DIM ** -0.5) * LOG2E
    col_scale = jnp.asarray(col_scale)

    w_in_t = jnp.swapaxes(w_in, 1, 2)
    w_tail, w_ng = _tail_weights(w_in_t)

    xt = x.reshape(t, d)
    memt = mem.reshape(batch * mem_len, d)
    fg = final_norm.reshape(1, d)
    bf = lambda a: a.astype(BF16)

    for l in range(depth):
        lam_init = 0.8 - 0.6 * math.exp(-0.3 * l)
        xt, hn = _ffn(xt, ffn1_norm[l].reshape(1, d), ffn1_w_gate, ffn1_w_up, ffn1_w_down,
                      mix_norm[l].reshape(1, d), l, "normed")

        proj, ngt = _inproj(hn, w_in_t, l, w_tail, w_ng, col_scale)

        o_diff = _diff_attention(proj, diff_lambda[l], diff_subln[l].reshape(1, DIFF_V_DIM),
                                 batch, seq, lam_init)

        ckv = proj[:, C_NKV:C_NKV + 2 * NSA_GROUPS * NSA_DIM]
        ckv = ckv.reshape(batch, n_rows, CMP_STRIDE, 2, NSA_GROUPS, NSA_DIM)
        ckv = ckv.transpose(3, 4, 0, 1, 2, 5).reshape(2, NSA_GROUPS, batch, n_rows, CMP_STRIDE * NSA_DIM)
        pos = bf(jnp.broadcast_to(nsa_cmp_pos[l].reshape(2, 1, CMP_LEN * NSA_DIM), (2, 8, CMP_LEN * NSA_DIM)))
        kv_cmp, kvt_cmp = _compress(ckv, pos, bf(nsa_cmp_w1[l]), bf(nsa_cmp_w2[l]))
        o_nsa = _nsa_attention(proj, ngt, kv_cmp, kvt_cmp, overlap_t, batch, seq)

        memkv = _memkv(memt, mem_norm[l].reshape(1, d), bf(w_mem_kv[l]))
        o_mem = _mem_attention(proj, memkv, batch, seq, mem_len)

        xt = _merge(xt, hn, o_diff, o_nsa, o_mem, w_tail, l, bf(w_up_diff[l]), bf(w_up_nsa[l]),
                    bf(w_up_mem[l]), bf(w_out[l]))

        (xt,) = _ffn(xt, ffn2_norm[l].reshape(1, d), ffn2_w_gate, ffn2_w_up, ffn2_w_down, fg, l,
                     "final" if l == depth - 1 else "plain")
    return xt.reshape(batch, seq, d)
```

```python
import functools
import math

import jax
import jax.numpy as jnp
import numpy as np
from jax import lax
from jax.experimental import pallas as pl
from jax.experimental.pallas import tpu as pltpu

F32 = jnp.float32
BF16 = jnp.bfloat16

NORM_EPS = 1e-6
NEG_INF = -1e30
SEL_BIG = 1e9
LOG2E = math.log2(math.e)

DIFF_HEADS = 8
DIFF_QK_DIM = 64
DIFF_V_DIM = 2 * DIFF_QK_DIM

NSA_HEADS = 8
NSA_GROUPS = 2
NSA_HPG = NSA_HEADS // NSA_GROUPS
NSA_DIM = 128
CMP_LEN = 32
CMP_STRIDE = 16
SLC_LEN = 64
SLC_TOPK = 8
WIN_LEN = 512

MEM_HEADS = 4
MEM_DIM = 128
N_BRANCH = 3

LANES = 128
VMEM_LIMIT = 56 * 1024 * 1024
DIFF_HEADS_PER_STEP = 4

DIFF_QK_COLS = DIFF_HEADS * 2 * DIFF_QK_DIM
DIFF_V_COLS = DIFF_HEADS * DIFF_V_DIM
NSA_Q_COLS = NSA_HEADS * NSA_DIM
NSA_KV_COLS = 3 * 2 * NSA_GROUPS * NSA_DIM
NSA_GATE_COLS = NSA_HEADS * 3
MEM_Q_COLS = MEM_HEADS * MEM_DIM
C_DQ = 0
C_DK = C_DQ + DIFF_QK_COLS
C_DV = C_DK + DIFF_QK_COLS
C_NQ = C_DV + DIFF_V_COLS
C_NKV = C_NQ + NSA_Q_COLS
C_NG = C_NKV + NSA_KV_COLS
C_MQ = C_NG + NSA_GATE_COLS
C_MG = C_MQ + MEM_Q_COLS
P_MQ = C_NG
P_COLS = P_MQ + MEM_Q_COLS


def _alibi_slopes(n_heads):
    return np.array([2.0 ** (-8.0 * (h + 1) / n_heads) for h in range(n_heads)], dtype=np.float32)


def _rmsnorm(x, g):
    y = x * lax.rsqrt(jnp.mean(x * x, axis=-1, keepdims=True) + NORM_EPS)
    return y * g


def _dot(a, b):
    return jnp.dot(a, b, preferred_element_type=F32)


def _dot_nt(a, b):
    return lax.dot_general(a, b, (((1,), (1,)), ((), ())), preferred_element_type=F32)


def _transpose_bf16(x):
    return x.astype(F32).T.astype(BF16)


def _params(*sem):
    return pltpu.CompilerParams(dimension_semantics=sem, vmem_limit_bytes=VMEM_LIMIT)


def _pick(n, pref):
    t = min(n, pref)
    assert n % t == 0, (n, t)
    return t


def _ffn_kernel(x_ref, g_ref, wg_ref, wu_ref, wd_ref, fg_ref, o_ref, *rest, tail):
    xn_ref = rest[-1]
    j = pl.program_id(1)

    @pl.when(j == 0)
    def _():
        xn_ref[...] = _rmsnorm(x_ref[...], g_ref[...]).astype(BF16)
        o_ref[...] = jnp.zeros_like(o_ref)

    xn = xn_ref[...]
    a = _dot(xn, wg_ref[...].astype(BF16))
    u = _dot(xn, wu_ref[...].astype(BF16))
    h = (a * jax.nn.sigmoid(a)) * u
    o_ref[...] += _dot(h.astype(BF16), wd_ref[...].astype(BF16))

    @pl.when(j == pl.num_programs(1) - 1)
    def _():
        rows = 128

        def chunk(c, carry):
            r = pl.ds(pl.multiple_of(c * rows, rows), rows)
            y = x_ref[r, :] + 0.5 * o_ref[r, :]
            if tail == "final":
                y = _rmsnorm(y, fg_ref[...])
            if tail == "normed":
                rest[0][r, :] = _rmsnorm(y, fg_ref[...]).astype(BF16)
            o_ref[r, :] = y
            return carry

        lax.fori_loop(0, o_ref.shape[0] // rows, chunk, 0)


def _ffn(x, g, wg, wu, wd, fg, layer, tail):
    t, d = x.shape
    f = wg.shape[2]
    tm = _pick(t, 1024)
    tf = _pick(f, 256)
    out_specs = [pl.BlockSpec((tm, d), lambda i, j: (i, 0))]
    out_shape = [jax.ShapeDtypeStruct((t, d), F32)]
    if tail == "normed":
        out_specs.append(pl.BlockSpec((tm, d), lambda i, j: (i, 0)))
        out_shape.append(jax.ShapeDtypeStruct((t, d), BF16))
    return pl.pallas_call(
        functools.partial(_ffn_kernel, tail=tail),
        grid=(t // tm, f // tf),
        in_specs=[
            pl.BlockSpec((tm, d), lambda i, j: (i, 0)),
            pl.BlockSpec((1, d), lambda i, j: (0, 0)),
            pl.BlockSpec((None, d, tf), lambda i, j: (layer, 0, j)),
            pl.BlockSpec((None, d, tf), lambda i, j: (layer, 0, j)),
            pl.BlockSpec((None, tf, d), lambda i, j: (layer, j, 0)),
            pl.BlockSpec((1, d), lambda i, j: (0, 0)),
        ],
        out_specs=out_specs,
        out_shape=out_shape,
        scratch_shapes=[] if tail == "normed" else [pltpu.VMEM((tm, d), BF16)],
        compiler_params=_params("parallel", "arbitrary"),
        name="ffn",
    )(x, g, wg, wu, wd, fg)


def _tail_kernel(w_ref, tail_ref, ng_ref, carry_ref, *, shift):
    s = pl.program_id(1)
    blk = w_ref[...]

    @pl.when(s == 0)
    def _():
        pad = jnp.zeros((ng_ref.shape[0] - shift, blk.shape[1]), F32)
        ng_ref[...] = jnp.concatenate([blk[:shift], pad], axis=0).astype(BF16)

    @pl.when(s > 0)
    def _():
        tail_ref[...] = jnp.concatenate([carry_ref[...], blk[:shift]], axis=0).astype(BF16)

    carry_ref[...] = blk[shift:]


def _tail_weights(w_in_t):
    depth, rows, d = w_in_t.shape
    tb = MEM_Q_COLS
    shift = C_MQ - C_NG
    n_out = -(-(rows - C_MQ) // tb)
    assert C_NG % tb == 0 and shift % 8 == 0 and shift <= LANES
    first, last = C_NG // tb, (rows - 1) // tb
    return pl.pallas_call(
        functools.partial(_tail_kernel, shift=shift),
        grid=(depth, n_out + 1),
        in_specs=[pl.BlockSpec((None, tb, d), lambda l, s: (l, jnp.minimum(first + s, last), 0))],
        out_specs=[
            pl.BlockSpec((None, tb, d), lambda l, s: (l, jnp.maximum(s - 1, 0), 0)),
            pl.BlockSpec((None, LANES, d), lambda l, s: (l, 0, 0)),
        ],
        out_shape=[
            jax.ShapeDtypeStruct((depth, n_out * tb, d), BF16),
            jax.ShapeDtypeStruct((depth, LANES, d), BF16),
        ],
        scratch_shapes=[pltpu.VMEM((tb - shift, d), F32)],
        compiler_params=_params("parallel", "arbitrary"),
        name="tail_weights",
    )(w_in_t)


def _inproj_kernel(hn_ref, w_ref, wmq_ref, wng_ref, cs_ref, proj_ref, ngt_ref, *, n_main):
    j = pl.program_id(1)

    @pl.when(j == 0)
    def _():
        ngt_ref[...] = jax.nn.sigmoid(_dot_nt(wng_ref[...], hn_ref[...]))

    @pl.when(j < n_main)
    def _():
        proj_ref[...] = (_dot_nt(hn_ref[...], w_ref[...].astype(BF16)) * cs_ref[...]).astype(BF16)

    @pl.when(j >= n_main)
    def _():
        proj_ref[...] = (_dot_nt(hn_ref[...], wmq_ref[...]) * cs_ref[...]).astype(BF16)


def _inproj(hn, w_in_t, layer, tail, wng, cs):
    t, d = hn.shape
    n = P_COLS
    tm = _pick(t, 2048)
    tn = MEM_Q_COLS
    assert P_MQ % tn == 0
    n_main = P_MQ // tn
    return pl.pallas_call(
        functools.partial(_inproj_kernel, n_main=n_main),
        grid=(t // tm, n // tn),
        in_specs=[
            pl.BlockSpec((tm, d), lambda i, j: (i, 0)),
            pl.BlockSpec((None, tn, d), lambda i, j: (layer, jnp.minimum(j, n_main - 1), 0)),
            pl.BlockSpec((None, tn, d), lambda i, j: (layer, 0, 0)),
            pl.BlockSpec((None, LANES, d), lambda i, j: (layer, 0, 0)),
            pl.BlockSpec((1, tn), lambda i, j: (0, j)),
        ],
        out_specs=[
            pl.BlockSpec((tm, tn), lambda i, j: (i, j)),
            pl.BlockSpec((LANES, tm), lambda i, j: (0, i)),
        ],
        out_shape=[
            jax.ShapeDtypeStruct((t, n), BF16),
            jax.ShapeDtypeStruct((LANES, t), F32),
        ],
        compiler_params=_params("parallel", "arbitrary"),
        name="inproj",
    )(hn, w_in_t, tail, wng, cs)


SUB = 128
ROWS_L = 16


def _assert_slopes_fit(slopes2):
    assert float(np.max(slopes2)) * SUB < 100.0, "ALiBi factor per bias block leaves the safe f32 range"


def _bias_factor_row(slope2, width):
    r = lax.broadcasted_iota(jnp.int32, (1, width), 1) & (SUB - 1)
    return jnp.exp2(slope2 * r.astype(F32))


def _scaled_vt(vt, w_row):
    ones = jnp.broadcast_to(w_row, (ROWS_L, w_row.shape[1]))
    return jnp.concatenate([vt * w_row, ones], axis=0).astype(BF16)


def _split_acc(acc, dv):
    return acc[0:dv] * (1.0 / jnp.maximum(acc[dv:dv + 1], 1e-30))


def _diff_kernel(slopes_ref, q_ref, k_ref, v_ref, lp_ref, sg_ref, o_ref,
                 vt_ref, pen_ref, z_ref, mrow_ref, alpha_ref, m_ref, acc_ref, *, tile, n_tiles, lam_init):
    hb = pl.program_id(1)
    qi = pl.program_id(2)
    hps = DIFF_HEADS_PER_STEP
    chains = range(2 * hps)
    slope2 = [slopes_ref[hb * hps + hh] for hh in range(hps)]
    nb = tile // SUB
    dh = 2 * DIFF_QK_DIM

    @pl.when(qi == 0)
    def _():
        for hh in range(hps):
            w_row = _bias_factor_row(slope2[hh], tile)
            for t in range(n_tiles):
                v = v_ref[t * tile:(t + 1) * tile, hh * DIFF_V_DIM:(hh + 1) * DIFF_V_DIM]
                vt_ref[hh, t] = _scaled_vt(v.astype(F32).T, w_row)
        row = lax.broadcasted_iota(jnp.int32, (tile, tile), 0)
        col = lax.broadcasted_iota(jnp.int32, (tile, tile), 1)
        pen_ref[0] = jnp.zeros((tile, tile), F32)
        pen_ref[1] = jnp.where(row <= col, 0.0, NEG_INF)

    qts = []
    for hh in range(hps):
        qt = _transpose_bf16(q_ref[:, hh * dh:(hh + 1) * dh])
        feat = lax.broadcasted_iota(jnp.int32, qt.shape, 0)
        qts += [jnp.where((feat >= m * DIFF_QK_DIM) & (feat < (m + 1) * DIFF_QK_DIM), qt, 0.0).astype(BF16)
                for m in range(2)]

    m_ref[...] = jnp.full_like(m_ref, NEG_INF)
    acc_ref[...] = jnp.zeros_like(acc_ref)

    def block_off(t, c, j):
        s2 = slope2[c // 2]
        return s2 * ((t - qi) * tile).astype(F32) + j * (s2 * float(SUB))

    def scores(t, c):
        ks = pl.multiple_of(t * tile, tile)
        hh = c // 2
        k = k_ref[pl.ds(ks, tile), hh * dh:(hh + 1) * dh]
        return _dot(k, qts[c]) + pen_ref[(t == qi).astype(jnp.int32)]

    def bookkeep(t, slot, c, z):
        z_ref[slot, c] = z
        m_old = m_ref[c]
        m_new = m_old
        for j in range(nb):
            m_new = jnp.maximum(m_new, jnp.max(z[j * SUB:(j + 1) * SUB], axis=0, keepdims=True)
                                + block_off(t, c, j))
        m_ref[c] = m_new
        mrow_ref[slot, c] = m_new
        alpha_ref[slot, c] = jnp.exp2(m_old - m_new)

    def weights(t, slot, c):
        m_new = mrow_ref[slot, c]
        return jnp.concatenate(
            [jnp.exp2(z_ref[slot, c, j * SUB:(j + 1) * SUB, :] - (m_new - block_off(t, c, j))).astype(BF16)
             for j in range(nb)], axis=0)

    def accumulate(t, slot, c, p):
        acc_ref[c] = alpha_ref[slot, c] * acc_ref[c] + _dot(vt_ref[c // 2, t], p)

    def overlapped(t, slot):
        for c in chains:
            z_next = scores(t + 1, c)
            p = weights(t, slot, c)
            bookkeep(t + 1, 1 - slot, c, z_next)
            accumulate(t, slot, c, p)

    def drain(t, slot):
        for c in chains:
            accumulate(t, slot, c, weights(t, slot, c))

    for c in chains:
        bookkeep(0, 0, c, scores(0, c))

    def body(u, carry):
        overlapped(2 * u, 0)
        overlapped(2 * u + 1, 1)
        return carry

    lax.fori_loop(0, qi // 2, body, 0)

    @pl.when(qi % 2 == 1)
    def _():
        overlapped(qi - 1, 0)
        drain(qi, 1)

    @pl.when(qi % 2 == 0)
    def _():
        drain(qi, 0)

    lp = lp_ref[...]
    lam = (jnp.exp(jnp.sum(lp[0:1] * lp[1:2], axis=-1, keepdims=True))
           - jnp.exp(jnp.sum(lp[2:3] * lp[3:4], axis=-1, keepdims=True)) + lam_init)
    outs = []
    for hh in range(hps):
        o1 = _split_acc(acc_ref[2 * hh], DIFF_V_DIM)
        o2 = _split_acc(acc_ref[2 * hh + 1], DIFF_V_DIM)
        ot = o1 - lam * o2
        ot = ot * lax.rsqrt(jnp.mean(ot * ot, axis=0, keepdims=True) + NORM_EPS)
        outs.append(ot.T * sg_ref[...] * (1.0 - lam_init))
    o_ref[...] = jnp.concatenate(outs, axis=1).astype(BF16)


def _diff_attention(proj, lp, sg, batch, seq, lam_init):
    t = proj.shape[0]
    tile = _pick(seq, 512)
    nq = seq // tile
    hps = DIFF_HEADS_PER_STEP
    width = hps * 2 * DIFF_QK_DIM
    assert DIFF_HEADS % hps == 0 and DIFF_V_DIM == 2 * DIFF_QK_DIM and tile % SUB == 0
    kb, vb = C_DK // width, C_DV // width
    slopes2 = _alibi_slopes(DIFF_HEADS) * np.float32(LOG2E)
    _assert_slopes_fit(slopes2)
    slopes = jnp.asarray(slopes2)
    return pl.pallas_call(
        functools.partial(_diff_kernel, tile=tile, n_tiles=nq, lam_init=lam_init),
        grid=(batch, DIFF_HEADS // hps, nq),
        in_specs=[
            pl.BlockSpec(memory_space=pltpu.SMEM),
            pl.BlockSpec((tile, width), lambda b, h, i: (b * nq + i, h)),
            pl.BlockSpec((seq, width), lambda b, h, i: (b, kb + h)),
            pl.BlockSpec((seq, width), lambda b, h, i: (b, vb + h)),
            pl.BlockSpec((4, DIFF_QK_DIM), lambda b, h, i: (0, 0)),
            pl.BlockSpec((1, DIFF_V_DIM), lambda b, h, i: (0, 0)),
        ],
        out_specs=pl.BlockSpec((tile, width), lambda b, h, i: (b * nq + i, h)),
        out_shape=jax.ShapeDtypeStruct((t, DIFF_HEADS * DIFF_V_DIM), BF16),
        scratch_shapes=[
            pltpu.VMEM((hps, nq, DIFF_V_DIM + ROWS_L, tile), BF16),
            pltpu.VMEM((2, tile, tile), F32),
            pltpu.VMEM((2, 2 * hps, tile, tile), F32),
            pltpu.VMEM((2, 2 * hps, 1, tile), F32),
            pltpu.VMEM((2, 2 * hps, 1, tile), F32),
            pltpu.VMEM((2 * hps, 1, tile), F32),
            pltpu.VMEM((2 * hps, DIFF_V_DIM + ROWS_L, tile), F32),
        ],
        compiler_params=_params("parallel", "parallel", "arbitrary"),
        name="diff_attn",
    )(slopes, proj, proj, proj, lp, sg)


def _compress_kernel(kv_ref, pos_ref, w1_ref, w2_ref, o_ref, ot_ref):
    half = w1_ref.shape[0] // 2
    kv = kv_ref[...]
    top = _dot(kv, w1_ref[:half, :])
    bot = _dot(kv, w1_ref[half:, :])
    rows = kv.shape[0]
    bot_next = pltpu.roll(bot, rows - 1, 0)
    pos_term = _dot(pos_ref[...], w1_ref[...])[0:1]
    hid = jax.nn.gelu(top + bot_next + pos_term)
    out = _dot(hid.astype(BF16), w2_ref[...])
    o_ref[...] = out.astype(BF16)
    ot_ref[...] = out.T.astype(BF16)


def _compress(kvr, pos, w1, w2):
    _, groups, batch, rows, width = kvr.shape
    d = w2.shape[-1]
    return pl.pallas_call(
        _compress_kernel,
        grid=(2, groups, batch),
        in_specs=[
            pl.BlockSpec((None, None, None, rows, width), lambda a, g, b: (a, g, b, 0, 0)),
            pl.BlockSpec((None, 8, 2 * width), lambda a, g, b: (a, 0, 0)),
            pl.BlockSpec((None, 2 * width, d), lambda a, g, b: (a, 0, 0)),
            pl.BlockSpec((None, d, d), lambda a, g, b: (a, 0, 0)),
        ],
        out_specs=[
            pl.BlockSpec((None, None, None, rows, d), lambda a, g, b: (a, b, g, 0, 0)),
            pl.BlockSpec((None, None, None, d, rows), lambda a, g, b: (a, b, g, 0, 0)),
        ],
        out_shape=[
            jax.ShapeDtypeStruct((2, batch, groups, rows, d), BF16),
            jax.ShapeDtypeStruct((2, batch, groups, d, rows), BF16),
        ],
        compiler_params=_params("parallel", "parallel", "parallel"),
        name="compress",
    )(kvr, pos, w1, w2)


def _nsa_kernel(slopes_ref, q_ref, gate_ref, kc_ref, vct_ref, ks_ref, vs_ref, kw_ref, vw_ref,
                ovt_ref, o_ref, vst_ref, vwt_ref, base_ref, z_ref, mrow_ref, alpha_ref, m_ref, acc_ref,
                *, tq, tk, seq):
    g = pl.program_id(1)
    qi = pl.program_id(2)
    q0 = qi * tq
    hp = NSA_HPG
    n = hp * tq
    dv = NSA_DIM
    span = WIN_LEN + tq
    n_slc = seq // SLC_LEN
    slope_row = jnp.concatenate(
        [jnp.full((1, tq), slopes_ref[g * hp + h], F32) for h in range(hp)], axis=1)

    @pl.when(qi == 0)
    def _():
        ones_s = jnp.ones((1, tk), F32)
        ones_w = jnp.ones((1, tq), F32)
        for t in range(seq // tk):
            vst_ref[t] = _scaled_vt(vs_ref[t * tk:(t + 1) * tk, :].astype(F32).T, ones_s)
        for t in range(seq // tq):
            vwt_ref[t] = _scaled_vt(vw_ref[t * tq:(t + 1) * tq, :].astype(F32).T, ones_w)
        key_row = lax.broadcasted_iota(jnp.int32, (span, tq), 0).astype(F32)
        for h in range(hp):
            base_ref[h] = slopes_ref[g * hp + h] * key_row

    q = q_ref[...]
    q4t = jnp.concatenate(
        [q[:, h * NSA_DIM:(h + 1) * NSA_DIM].astype(F32).T for h in range(hp)], axis=1).astype(BF16)
    qpos_all = q0 + (lax.broadcasted_iota(jnp.int32, (1, n), 1) & (tq - 1))
    qpos = q0 + lax.broadcasted_iota(jnp.int32, (1, tq), 1)

    n_cmp = kc_ref.shape[0]
    cmp_end = lax.broadcasted_iota(jnp.int32, (n_cmp, 1), 0) * CMP_STRIDE + (CMP_LEN - 1)
    mask_c = cmp_end <= qpos_all
    z_c = jnp.where(mask_c, _dot(kc_ref[...], q4t), NEG_INF)
    m_c = jnp.max(z_c, axis=0, keepdims=True)
    e_c = jnp.where(mask_c, jnp.exp2(z_c - m_c), 0.0)
    p_c = e_c * (1.0 / jnp.maximum(jnp.sum(e_c, axis=0, keepdims=True), 1e-30))
    o_c = _dot(vct_ref[...], p_c.astype(BF16))

    wt = jnp.maximum(qi - WIN_LEN // tq, 0)
    w0 = pl.multiple_of(wt * tq, tq)
    dist = qpos - (w0 + lax.broadcasted_iota(jnp.int32, (span, 1), 0))
    in_win = (dist >= 0) & (dist < WIN_LEN)
    pen_w = jnp.concatenate([jnp.where(in_win, base_ref[h], NEG_INF) for h in range(hp)], axis=1)
    z_w = _dot(kw_ref[pl.ds(w0, span), :], q4t) + pen_w
    p_w = jnp.exp2(z_w - jnp.max(z_w, axis=0, keepdims=True)).astype(BF16)
    vw_cat = jnp.concatenate([vwt_ref[wt + j] for j in range(span // tq)], axis=1)
    o_w = _split_acc(_dot(vw_cat, p_w), dv)

    p_sum = p_c[:, 0:tq]
    for h in range(1, hp):
        p_sum = p_sum + p_c[:, h * tq:(h + 1) * tq]
    ovt = ovt_ref[...]
    p_hi = p_sum.astype(BF16)
    r1 = p_sum - p_hi.astype(F32)
    p_mid = r1.astype(BF16)
    p_lo = (r1 - p_mid.astype(F32)).astype(BF16)
    imp = _dot(ovt, p_hi) + _dot(ovt, p_mid) + _dot(ovt, p_lo)
    jblk = lax.broadcasted_iota(jnp.int32, (n_slc, tq), 0)
    cur = qpos // SLC_LEN
    forced = (jblk == 0) | (jblk == cur) | (jblk == cur - 1)
    imp = jnp.where(forced, SEL_BIG, jnp.where(jblk > cur, -SEL_BIG, imp))
    jblk_f = jblk.astype(F32)
    sel = jnp.zeros((n_slc, tq), F32)
    for _ in range(min(SLC_TOPK, n_slc)):
        mx = jnp.max(imp, axis=0, keepdims=True)
        first = jnp.min(jnp.where(imp == mx, jblk_f, float(n_slc)), axis=0, keepdims=True)
        hit = jblk_f == first
        sel = jnp.where(hit, 1.0, sel)
        imp = jnp.where(hit, -jnp.inf, imp)
    sel_bf = sel.astype(BF16)

    m_ref[...] = jnp.full_like(m_ref, NEG_INF)
    acc_ref[...] = jnp.zeros_like(acc_ref)
    exp_blk = lax.broadcasted_iota(jnp.int32, (tk, n_slc), 0) // SLC_LEN
    exp_col = lax.broadcasted_iota(jnp.int32, (tk, n_slc), 1)
    krow = lax.broadcasted_iota(jnp.int32, (tk, 1), 0)
    assert hp == 4
    half = 2 * tq

    def tile_off(t, i):
        return slope_row[:, i * half:(i + 1) * half] * (t * tk - q0).astype(F32)

    def penalty(t):
        expand = jnp.where(exp_col == exp_blk + t * (tk // SLC_LEN), 1.0, 0.0).astype(BF16)
        chosen = _dot(expand, sel_bf)
        keep = (chosen > 0.5) & ((krow + t * tk) <= qpos)
        return [jnp.concatenate([jnp.where(keep, base_ref[2 * i + j, 0:tk, :], NEG_INF)
                                 for j in range(2)], axis=1) for i in range(2)]

    def scores(t, i, pen):
        ks = pl.multiple_of(t * tk, tk)
        return _dot(ks_ref[pl.ds(ks, tk), :], q4t[:, i * half:(i + 1) * half]) + pen[i]

    def bookkeep(t, slot, i, z):
        z_ref[slot, i] = z
        m_old = m_ref[i]
        m_new = jnp.maximum(m_old, jnp.max(z, axis=0, keepdims=True) + tile_off(t, i))
        m_ref[i] = m_new
        mrow_ref[slot, i] = m_new
        alpha_ref[slot, i] = jnp.exp2(m_old - m_new)

    def weights(t, slot, i):
        return jnp.exp2(z_ref[slot, i] - (mrow_ref[slot, i] - tile_off(t, i))).astype(BF16)

    def accumulate(t, slot, i, p):
        acc_ref[i] = alpha_ref[slot, i] * acc_ref[i] + _dot(vst_ref[t], p)

    def overlapped(t, slot):
        pen = penalty(t + 1)
        for i in range(2):
            z_next = scores(t + 1, i, pen)
            p = weights(t, slot, i)
            bookkeep(t + 1, 1 - slot, i, z_next)
            accumulate(t, slot, i, p)

    def drain(t, slot):
        for i in range(2):
            accumulate(t, slot, i, weights(t, slot, i))

    pen = penalty(0)
    for i in range(2):
        bookkeep(0, 0, i, scores(0, i, pen))
    n_last = q0 // tk

    def body(u, carry):
        overlapped(2 * u, 0)
        overlapped(2 * u + 1, 1)
        return carry

    lax.fori_loop(0, n_last // 2, body, 0)

    @pl.when(n_last % 2 == 1)
    def _():
        overlapped(n_last - 1, 0)
        drain(n_last, 1)

    @pl.when(n_last % 2 == 0)
    def _():
        drain(n_last, 0)

    o_s = jnp.concatenate([_split_acc(acc_ref[i], dv) for i in range(2)], axis=1)

    def gate_row(branch):
        return jnp.concatenate(
            [gate_ref[pl.ds((g * hp + h) * 3 + branch, 1), :] for h in range(hp)], axis=1)

    ot = gate_row(0) * o_c + gate_row(1) * o_s + gate_row(2) * o_w
    o_ref[...] = jnp.concatenate(
        [ot[:, h * tq:(h + 1) * tq].T for h in range(hp)], axis=1).astype(BF16)


def _nsa_attention(proj, gates_t, k_cmp, vt_cmp, overlap_t, batch, seq):
    t = proj.shape[0]
    tq = _pick(seq, 256)
    tk = _pick(seq, 512)
    assert seq >= WIN_LEN + tq and WIN_LEN % tq == 0 and tk % tq == 0
    nq = seq // tq
    hp = NSA_HPG
    qb = C_NQ // (hp * NSA_DIM)
    kvb = C_NKV // LANES
    n_cmp = k_cmp.shape[3]
    slopes2 = _alibi_slopes(NSA_HEADS) * np.float32(LOG2E)
    _assert_slopes_fit(slopes2)
    slopes = jnp.asarray(slopes2)

    def kv_spec(branch, which):
        base = kvb + (branch * 2 + which) * NSA_GROUPS
        return pl.BlockSpec((seq, LANES), lambda b, g, i: (b, base + g))

    return pl.pallas_call(
        functools.partial(_nsa_kernel, tq=tq, tk=tk, seq=seq),
        grid=(batch, NSA_GROUPS, nq),
        in_specs=[
            pl.BlockSpec(memory_space=pltpu.SMEM),
            pl.BlockSpec((tq, hp * NSA_DIM), lambda b, g, i: (b * nq + i, qb + g)),
            pl.BlockSpec((LANES, tq), lambda b, g, i: (0, b * nq + i)),
            pl.BlockSpec((None, None, None, n_cmp, NSA_DIM), lambda b, g, i: (0, b, g, 0, 0)),
            pl.BlockSpec((None, None, None, NSA_DIM, n_cmp), lambda b, g, i: (1, b, g, 0, 0)),
            kv_spec(1, 0), kv_spec(1, 1),
            kv_spec(2, 0), kv_spec(2, 1),
            pl.BlockSpec(overlap_t.shape, lambda b, g, i: (0, 0)),
        ],
        out_specs=pl.BlockSpec((tq, hp * NSA_DIM), lambda b, g, i: (b * nq + i, g)),
        out_shape=jax.ShapeDtypeStruct((t, NSA_HEADS * NSA_DIM), BF16),
        scratch_shapes=[
            pltpu.VMEM((seq // tk, NSA_DIM + ROWS_L, tk), BF16),
            pltpu.VMEM((seq // tq, NSA_DIM + ROWS_L, tq), BF16),
            pltpu.VMEM((hp, WIN_LEN + tq, tq), F32),
            pltpu.VMEM((2, 2, tk, 2 * tq), F32),
            pltpu.VMEM((2, 2, 1, 2 * tq), F32),
            pltpu.VMEM((2, 2, 1, 2 * tq), F32),
            pltpu.VMEM((2, 1, 2 * tq), F32),
            pltpu.VMEM((2, NSA_DIM + ROWS_L, 2 * tq), F32),
        ],
        compiler_params=_params("parallel", "parallel", "arbitrary"),
        name="nsa_attn",
    )(slopes, proj, gates_t, k_cmp, vt_cmp, proj, proj, proj, proj, overlap_t)


def _memkv_kernel(mem_ref, g_ref, w_ref, o_ref):
    mn = _rmsnorm(mem_ref[...], g_ref[...]).astype(BF16)
    o_ref[...] = _dot(mn, w_ref[...]).astype(BF16)


def _memkv(mem, g, w):
    r, d = mem.shape
    n = w.shape[1]
    tm = _pick(r, 256)
    return pl.pallas_call(
        _memkv_kernel,
        grid=(r // tm,),
        in_specs=[
            pl.BlockSpec((tm, d), lambda i: (i, 0)),
            pl.BlockSpec((1, d), lambda i: (0, 0)),
            pl.BlockSpec((d, n), lambda i: (0, 0)),
        ],
        out_specs=pl.BlockSpec((tm, n), lambda i: (i, 0)),
        out_shape=jax.ShapeDtypeStruct((r, n), BF16),
        compiler_params=_params("parallel"),
        name="mem_kv",
    )(mem, g, w)


def _memattn_kernel(q_ref, k_ref, v_ref, o_ref):
    scale = MEM_DIM ** -0.5
    q = q_ref[...]
    k = k_ref[...]
    v = v_ref[...]
    outs = []
    for h in range(MEM_HEADS):
        sl = slice(h * MEM_DIM, (h + 1) * MEM_DIM)
        s = _dot_nt(q[:, sl], k[:, sl]) * scale
        m = jnp.max(s, axis=-1, keepdims=True)
        e = jnp.exp(s - m)
        p = e / jnp.sum(e, axis=-1, keepdims=True)
        outs.append(_dot(p.astype(BF16), v[:, sl]))
    o_ref[...] = jnp.concatenate(outs, axis=-1).astype(BF16)


def _mem_attention(proj, memkv, batch, seq, mem_len):
    t = proj.shape[0]
    tq = _pick(seq, 512)
    nq = seq // tq
    width = MEM_HEADS * MEM_DIM
    qb = P_MQ // width
    return pl.pallas_call(
        _memattn_kernel,
        grid=(batch, nq),
        in_specs=[
            pl.BlockSpec((tq, width), lambda b, i: (b * nq + i, qb)),
            pl.BlockSpec((mem_len, width), lambda b, i: (b, 0)),
            pl.BlockSpec((mem_len, width), lambda b, i: (b, 1)),
        ],
        out_specs=pl.BlockSpec((tq, width), lambda b, i: (b * nq + i, 0)),
        out_shape=jax.ShapeDtypeStruct((t, width), BF16),
        compiler_params=_params("parallel", "parallel"),
        name="mem_attn",
    )(proj, memkv, memkv)


def _merge_kernel(x_ref, hn_ref, od_ref, on_ref, om_ref, wgd_ref, wgn_ref, wgm_ref,
                  wud_ref, wun_ref, wum_ref, wo_ref, o_ref, acc_ref):
    j = pl.program_id(1)

    @pl.when(j == 0)
    def _():
        acc_ref[...] = jnp.zeros_like(acc_ref)

    hn = hn_ref[...]
    merged = jax.nn.sigmoid(_dot_nt(hn, wgd_ref[...])) * _dot(od_ref[...], wud_ref[...])
    merged += jax.nn.sigmoid(_dot_nt(hn, wgn_ref[...])) * _dot(on_ref[...], wun_ref[...])
    merged += jax.nn.sigmoid(_dot_nt(hn, wgm_ref[...])) * _dot(om_ref[...], wum_ref[...])
    acc_ref[...] += _dot(merged.astype(BF16), wo_ref[...])

    @pl.when(j == pl.num_programs(1) - 1)
    def _():
        o_ref[...] = x_ref[...] + acc_ref[...]


def _merge(x, hn, od, on, om, tail, layer, wud, wun, wum, wo):
    t, d = x.shape
    tm = _pick(t, 512)
    tj = _pick(d, 512)
    assert MEM_Q_COLS % tj == 0

    def gate_spec(branch):
        first = (MEM_Q_COLS + branch * d) // tj
        return pl.BlockSpec((None, tj, d), lambda i, j: (layer, first + j, 0))

    return pl.pallas_call(
        _merge_kernel,
        grid=(t // tm, d // tj),
        in_specs=[
            pl.BlockSpec((tm, d), lambda i, j: (i, 0)),
            pl.BlockSpec((tm, d), lambda i, j: (i, 0)),
            pl.BlockSpec((tm, od.shape[1]), lambda i, j: (i, 0)),
            pl.BlockSpec((tm, on.shape[1]), lambda i, j: (i, 0)),
            pl.BlockSpec((tm, om.shape[1]), lambda i, j: (i, 0)),
            gate_spec(0), gate_spec(1), gate_spec(2),
            pl.BlockSpec((wud.shape[0], tj), lambda i, j: (0, j)),
            pl.BlockSpec((wun.shape[0], tj), lambda i, j: (0, j)),
            pl.BlockSpec((wum.shape[0], tj), lambda i, j: (0, j)),
            pl.BlockSpec((tj, d), lambda i, j: (j, 0)),
        ],
        out_specs=pl.BlockSpec((tm, d), lambda i, j: (i, 0)),
        out_shape=jax.ShapeDtypeStruct((t, d), F32),
        scratch_shapes=[pltpu.VMEM((tm, d), F32)],
        compiler_params=_params("parallel", "arbitrary"),
        name="merge",
    )(x, hn, od, on, om, tail, tail, tail, wud, wun, wum, wo)


def _overlap_matrix_t(n_slc, n_cmp):
    cmp_start = np.arange(n_cmp)[None, :] * CMP_STRIDE
    slc_start = np.arange(n_slc)[:, None] * SLC_LEN
    return ((cmp_start < slc_start + SLC_LEN) & (cmp_start + CMP_LEN > slc_start)).astype(np.float32)


def kernel(x, mem, ffn1_norm, ffn1_w_gate, ffn1_w_up, ffn1_w_down, mix_norm, w_in, diff_lambda, diff_subln, nsa_cmp_pos, nsa_cmp_w1, nsa_cmp_w2, mem_norm, w_mem_kv, w_up_diff, w_up_nsa, w_up_mem, w_out, ffn2_norm, ffn2_w_gate, ffn2_w_up, ffn2_w_down, final_norm):
    batch, seq, d = x.shape
    mem_len = mem.shape[1]
    depth = w_in.shape[0]
    t = batch * seq
    assert seq % CMP_STRIDE == 0 and CMP_LEN == 2 * CMP_STRIDE
    n_rows = seq // CMP_STRIDE
    n_slc = seq // SLC_LEN
    overlap_t = jnp.asarray(_overlap_matrix_t(n_slc, n_rows), BF16)

    col_scale = np.ones((1, P_COLS), np.float32)
    col_scale[0, C_DQ:C_DK] = (DIFF_QK_DIM ** -0.5) * LOG2E
    col_scale[0, C_NQ:C_NKV] = (NSA_DIM ** -0.5) * LOG2E
    col_scale = jnp.asarray(col_scale)

    w_in_t = jnp.swapaxes(w_in, 1, 2)
    w_tail, w_ng = _tail_weights(w_in_t)

    xt = x.reshape(t, d)
    memt = mem.reshape(batch * mem_len, d)
    fg = final_norm.reshape(1, d)
    bf = lambda a: a.astype(BF16)

    for l in range(depth):
        lam_init = 0.8 - 0.6 * math.exp(-0.3 * l)
        xt, hn = _ffn(xt, ffn1_norm[l].reshape(1, d), ffn1_w_gate, ffn1_w_up, ffn1_w_down,
                      mix_norm[l].reshape(1, d), l, "normed")

        proj, ngt = _inproj(hn, w_in_t, l, w_tail, w_ng, col_scale)

        o_diff = _diff_attention(proj, diff_lambda[l], diff_subln[l].reshape(1, DIFF_V_DIM),
                                 batch, seq, lam_init)

        ckv = proj[:, C_NKV:C_NKV + 2 * NSA_GROUPS * NSA_DIM]
        ckv = ckv.reshape(batch, n_rows, CMP_STRIDE, 2, NSA_GROUPS, NSA_DIM)
        ckv = ckv.transpose(3, 4, 0, 1, 2, 5).reshape(2, NSA_GROUPS, batch, n_rows, CMP_STRIDE * NSA_DIM)
        pos = bf(jnp.broadcast_to(nsa_cmp_pos[l].reshape(2, 1, CMP_LEN * NSA_DIM), (2, 8, CMP_LEN * NSA_DIM)))
        kv_cmp, kvt_cmp = _compress(ckv, pos, bf(nsa_cmp_w1[l]), bf(nsa_cmp_w2[l]))
        o_nsa = _nsa_attention(proj, ngt, kv_cmp, kvt_cmp, overlap_t, batch, seq)

        memkv = _memkv(memt, mem_norm[l].reshape(1, d), bf(w_mem_kv[l]))
        o_mem = _mem_attention(proj, memkv, batch, seq, mem_len)

        xt = _merge(xt, hn, o_diff, o_nsa, o_mem, w_tail, l, bf(w_up_diff[l]), bf(w_up_nsa[l]),
                    bf(w_up_mem[l]), bf(w_out[l]))

        (xt,) = _ffn(xt, ffn2_norm[l].reshape(1, d), ffn2_w_gate, ffn2_w_up, ffn2_w_down, fg, l,
                     "final" if l == depth - 1 else "plain")
    return xt.reshape(batch, seq, d)
```

```python
import functools
import math

import jax
import jax.numpy as jnp
import numpy as np
from jax import lax
from jax.experimental import pallas as pl
from jax.experimental.pallas import tpu as pltpu

F32 = jnp.float32
BF16 = jnp.bfloat16

NORM_EPS = 1e-6
NEG_INF = -1e30
SEL_BIG = 1e9
LOG2E = math.log2(math.e)

DIFF_HEADS = 8
DIFF_QK_DIM = 64
DIFF_V_DIM = 2 * DIFF_QK_DIM

NSA_HEADS = 8
NSA_GROUPS = 2
NSA_HPG = NSA_HEADS // NSA_GROUPS
NSA_DIM = 128
CMP_LEN = 32
CMP_STRIDE = 16
SLC_LEN = 64
SLC_TOPK = 8
WIN_LEN = 512

MEM_HEADS = 4
MEM_DIM = 128
N_BRANCH = 3

LANES = 128
VMEM_LIMIT = 56 * 1024 * 1024
DIFF_HEADS_PER_STEP = 4
NSA_GROUPS_PER_STEP = 2

DIFF_QK_COLS = DIFF_HEADS * 2 * DIFF_QK_DIM
DIFF_V_COLS = DIFF_HEADS * DIFF_V_DIM
NSA_Q_COLS = NSA_HEADS * NSA_DIM
NSA_KV_COLS = 3 * 2 * NSA_GROUPS * NSA_DIM
NSA_GATE_COLS = NSA_HEADS * 3
MEM_Q_COLS = MEM_HEADS * MEM_DIM
C_DQ = 0
C_DK = C_DQ + DIFF_QK_COLS
C_DV = C_DK + DIFF_QK_COLS
C_NQ = C_DV + DIFF_V_COLS
C_NKV = C_NQ + NSA_Q_COLS
C_NG = C_NKV + NSA_KV_COLS
C_MQ = C_NG + NSA_GATE_COLS
C_MG = C_MQ + MEM_Q_COLS
P_MQ = C_NG
P_COLS = P_MQ + MEM_Q_COLS


def _alibi_slopes(n_heads):
    return np.array([2.0 ** (-8.0 * (h + 1) / n_heads) for h in range(n_heads)], dtype=np.float32)


def _rmsnorm(x, g):
    y = x * lax.rsqrt(jnp.mean(x * x, axis=-1, keepdims=True) + NORM_EPS)
    return y * g


def _dot(a, b):
    return jnp.dot(a, b, preferred_element_type=F32)


def _dot_nt(a, b):
    return lax.dot_general(a, b, (((1,), (1,)), ((), ())), preferred_element_type=F32)


def _transpose_bf16(x):
    return x.astype(F32).T.astype(BF16)


def _params(*sem):
    return pltpu.CompilerParams(dimension_semantics=sem, vmem_limit_bytes=VMEM_LIMIT)


def _pick(n, pref):
    t = min(n, pref)
    assert n % t == 0, (n, t)
    return t


def _ffn_kernel(x_ref, g_ref, wg_ref, wu_ref, wd_ref, fg_ref, o_ref, *rest, tail):
    xn_ref = rest[-1]
    j = pl.program_id(1)

    @pl.when(j == 0)
    def _():
        xn_ref[...] = _rmsnorm(x_ref[...], g_ref[...]).astype(BF16)
        o_ref[...] = jnp.zeros_like(o_ref)

    xn = xn_ref[...]
    a = _dot(xn, wg_ref[...].astype(BF16))
    u = _dot(xn, wu_ref[...].astype(BF16))
    h = (a * jax.nn.sigmoid(a)) * u
    o_ref[...] += _dot(h.astype(BF16), wd_ref[...].astype(BF16))

    @pl.when(j == pl.num_programs(1) - 1)
    def _():
        rows = 128

        def chunk(c, carry):
            r = pl.ds(pl.multiple_of(c * rows, rows), rows)
            y = x_ref[r, :] + 0.5 * o_ref[r, :]
            if tail == "final":
                y = _rmsnorm(y, fg_ref[...])
            if tail == "normed":
                rest[0][r, :] = _rmsnorm(y, fg_ref[...]).astype(BF16)
            o_ref[r, :] = y
            return carry

        lax.fori_loop(0, o_ref.shape[0] // rows, chunk, 0)


def _ffn(x, g, wg, wu, wd, fg, layer, tail):
    t, d = x.shape
    f = wg.shape[2]
    tm = _pick(t, 1024)
    tf = _pick(f, 256)
    out_specs = [pl.BlockSpec((tm, d), lambda i, j: (i, 0))]
    out_shape = [jax.ShapeDtypeStruct((t, d), F32)]
    if tail == "normed":
        out_specs.append(pl.BlockSpec((tm, d), lambda i, j: (i, 0)))
        out_shape.append(jax.ShapeDtypeStruct((t, d), BF16))
    return pl.pallas_call(
        functools.partial(_ffn_kernel, tail=tail),
        grid=(t // tm, f // tf),
        in_specs=[
            pl.BlockSpec((tm, d), lambda i, j: (i, 0)),
            pl.BlockSpec((1, d), lambda i, j: (0, 0)),
            pl.BlockSpec((None, d, tf), lambda i, j: (layer, 0, j)),
            pl.BlockSpec((None, d, tf), lambda i, j: (layer, 0, j)),
            pl.BlockSpec((None, tf, d), lambda i, j: (layer, j, 0)),
            pl.BlockSpec((1, d), lambda i, j: (0, 0)),
        ],
        out_specs=out_specs,
        out_shape=out_shape,
        scratch_shapes=[] if tail == "normed" else [pltpu.VMEM((tm, d), BF16)],
        compiler_params=_params("parallel", "arbitrary"),
        name="ffn",
    )(x, g, wg, wu, wd, fg)


def _tail_kernel(w_ref, tail_ref, ng_ref, carry_ref, *, shift):
    s = pl.program_id(1)
    blk = w_ref[...]

    @pl.when(s == 0)
    def _():
        pad = jnp.zeros((ng_ref.shape[0] - shift, blk.shape[1]), F32)
        ng_ref[...] = jnp.concatenate([blk[:shift], pad], axis=0).astype(BF16)

    @pl.when(s > 0)
    def _():
        tail_ref[...] = jnp.concatenate([carry_ref[...], blk[:shift]], axis=0).astype(BF16)

    carry_ref[...] = blk[shift:]


def _tail_weights(w_in_t):
    depth, rows, d = w_in_t.shape
    tb = MEM_Q_COLS
    shift = C_MQ - C_NG
    n_out = -(-(rows - C_MQ) // tb)
    assert C_NG % tb == 0 and shift % 8 == 0 and shift <= LANES
    first, last = C_NG // tb, (rows - 1) // tb
    return pl.pallas_call(
        functools.partial(_tail_kernel, shift=shift),
        grid=(depth, n_out + 1),
        in_specs=[pl.BlockSpec((None, tb, d), lambda l, s: (l, jnp.minimum(first + s, last), 0))],
        out_specs=[
            pl.BlockSpec((None, tb, d), lambda l, s: (l, jnp.maximum(s - 1, 0), 0)),
            pl.BlockSpec((None, LANES, d), lambda l, s: (l, 0, 0)),
        ],
        out_shape=[
            jax.ShapeDtypeStruct((depth, n_out * tb, d), BF16),
            jax.ShapeDtypeStruct((depth, LANES, d), BF16),
        ],
        scratch_shapes=[pltpu.VMEM((tb - shift, d), F32)],
        compiler_params=_params("parallel", "arbitrary"),
        name="tail_weights",
    )(w_in_t)


def _inproj_kernel(hn_ref, w_ref, wmq_ref, wng_ref, cs_ref, proj_ref, ngt_ref, *, n_main):
    j = pl.program_id(1)

    @pl.when(j == 0)
    def _():
        ngt_ref[...] = jax.nn.sigmoid(_dot_nt(wng_ref[...], hn_ref[...]))

    @pl.when(j < n_main)
    def _():
        proj_ref[...] = (_dot_nt(hn_ref[...], w_ref[...].astype(BF16)) * cs_ref[...]).astype(BF16)

    @pl.when(j >= n_main)
    def _():
        proj_ref[...] = (_dot_nt(hn_ref[...], wmq_ref[...]) * cs_ref[...]).astype(BF16)


def _inproj(hn, w_in_t, layer, tail, wng, cs):
    t, d = hn.shape
    n = P_COLS
    tm = _pick(t, 2048)
    tn = MEM_Q_COLS
    assert P_MQ % tn == 0
    n_main = P_MQ // tn
    return pl.pallas_call(
        functools.partial(_inproj_kernel, n_main=n_main),
        grid=(t // tm, n // tn),
        in_specs=[
            pl.BlockSpec((tm, d), lambda i, j: (i, 0)),
            pl.BlockSpec((None, tn, d), lambda i, j: (layer, jnp.minimum(j, n_main - 1), 0)),
            pl.BlockSpec((None, tn, d), lambda i, j: (layer, 0, 0)),
            pl.BlockSpec((None, LANES, d), lambda i, j: (layer, 0, 0)),
            pl.BlockSpec((1, tn), lambda i, j: (0, j)),
        ],
        out_specs=[
            pl.BlockSpec((tm, tn), lambda i, j: (i, j)),
            pl.BlockSpec((LANES, tm), lambda i, j: (0, i)),
        ],
        out_shape=[
            jax.ShapeDtypeStruct((t, n), BF16),
            jax.ShapeDtypeStruct((LANES, t), F32),
        ],
        compiler_params=_params("parallel", "arbitrary"),
        name="inproj",
    )(hn, w_in_t, tail, wng, cs)


SUB = 128
ROWS_L = 16


def _assert_slopes_fit(slopes2):
    assert float(np.max(slopes2)) * SUB < 100.0, "ALiBi factor per bias block leaves the safe f32 range"


def _bias_factor_row(slope2, width):
    r = lax.broadcasted_iota(jnp.int32, (1, width), 1) & (SUB - 1)
    return jnp.exp2(slope2 * r.astype(F32))


def _scaled_vt(vt, w_row):
    ones = jnp.broadcast_to(w_row, (ROWS_L, w_row.shape[1]))
    return jnp.concatenate([vt * w_row, ones], axis=0).astype(BF16)


def _split_acc(acc, dv):
    return acc[0:dv] * (1.0 / jnp.maximum(acc[dv:dv + 1], 1e-30))


def _diff_kernel(slopes_ref, q_ref, k_ref, v_ref, lp_ref, sg_ref, o_ref,
                 vt_ref, pen_ref, z_ref, mrow_ref, alpha_ref, m_ref, acc_ref, *, tile, n_tiles, lam_init):
    hb = pl.program_id(1)
    qi = pl.program_id(2)
    hps = DIFF_HEADS_PER_STEP
    chains = range(2 * hps)
    slope2 = [slopes_ref[hb * hps + hh] for hh in range(hps)]
    nb = tile // SUB
    dh = 2 * DIFF_QK_DIM

    @pl.when(qi == 0)
    def _():
        for hh in range(hps):
            w_row = _bias_factor_row(slope2[hh], tile)
            for t in range(n_tiles):
                v = v_ref[t * tile:(t + 1) * tile, hh * DIFF_V_DIM:(hh + 1) * DIFF_V_DIM]
                vt_ref[hh, t] = _scaled_vt(v.astype(F32).T, w_row)
        row = lax.broadcasted_iota(jnp.int32, (tile, tile), 0)
        col = lax.broadcasted_iota(jnp.int32, (tile, tile), 1)
        pen_ref[0] = jnp.zeros((tile, tile), F32)
        pen_ref[1] = jnp.where(row <= col, 0.0, NEG_INF)

    qts = []
    for hh in range(hps):
        qt = _transpose_bf16(q_ref[:, hh * dh:(hh + 1) * dh])
        feat = lax.broadcasted_iota(jnp.int32, qt.shape, 0)
        qts += [jnp.where((feat >= m * DIFF_QK_DIM) & (feat < (m + 1) * DIFF_QK_DIM), qt, 0.0).astype(BF16)
                for m in range(2)]

    m_ref[...] = jnp.full_like(m_ref, NEG_INF)
    acc_ref[...] = jnp.zeros_like(acc_ref)

    def block_off(t, c, j):
        s2 = slope2[c // 2]
        return s2 * ((t - qi) * tile).astype(F32) + j * (s2 * float(SUB))

    def scores(t, c):
        ks = pl.multiple_of(t * tile, tile)
        hh = c // 2
        k = k_ref[pl.ds(ks, tile), hh * dh:(hh + 1) * dh]
        return _dot(k, qts[c]) + pen_ref[(t == qi).astype(jnp.int32)]

    def bookkeep(t, slot, c, z):
        z_ref[slot, c] = z
        m_old = m_ref[c]
        m_new = m_old
        for j in range(nb):
            m_new = jnp.maximum(m_new, jnp.max(z[j * SUB:(j + 1) * SUB], axis=0, keepdims=True)
                                + block_off(t, c, j))
        m_ref[c] = m_new
        mrow_ref[slot, c] = m_new
        alpha_ref[slot, c] = jnp.exp2(m_old - m_new)

    def weights(t, slot, c):
        m_new = mrow_ref[slot, c]
        return jnp.concatenate(
            [jnp.exp2(z_ref[slot, c, j * SUB:(j + 1) * SUB, :] - (m_new - block_off(t, c, j))).astype(BF16)
             for j in range(nb)], axis=0)

    def accumulate(t, slot, c, p):
        acc_ref[c] = alpha_ref[slot, c] * acc_ref[c] + _dot(vt_ref[c // 2, t], p)

    def overlapped(t, slot):
        for c in chains:
            z_next = scores(t + 1, c)
            p = weights(t, slot, c)
            bookkeep(t + 1, 1 - slot, c, z_next)
            accumulate(t, slot, c, p)

    def drain(t, slot):
        for c in chains:
            accumulate(t, slot, c, weights(t, slot, c))

    for c in chains:
        bookkeep(0, 0, c, scores(0, c))

    def body(u, carry):
        overlapped(2 * u, 0)
        overlapped(2 * u + 1, 1)
        return carry

    lax.fori_loop(0, qi // 2, body, 0)

    @pl.when(qi % 2 == 1)
    def _():
        overlapped(qi - 1, 0)
        drain(qi, 1)

    @pl.when(qi % 2 == 0)
    def _():
        drain(qi, 0)

    lp = lp_ref[...]
    lam = (jnp.exp(jnp.sum(lp[0:1] * lp[1:2], axis=-1, keepdims=True))
           - jnp.exp(jnp.sum(lp[2:3] * lp[3:4], axis=-1, keepdims=True)) + lam_init)
    outs = []
    for hh in range(hps):
        o1 = _split_acc(acc_ref[2 * hh], DIFF_V_DIM)
        o2 = _split_acc(acc_ref[2 * hh + 1], DIFF_V_DIM)
        ot = o1 - lam * o2
        ot = ot * lax.rsqrt(jnp.mean(ot * ot, axis=0, keepdims=True) + NORM_EPS)
        outs.append(ot.T * sg_ref[...] * (1.0 - lam_init))
    o_ref[...] = jnp.concatenate(outs, axis=1).astype(BF16)


def _diff_attention(proj, lp, sg, batch, seq, lam_init):
    t = proj.shape[0]
    tile = _pick(seq, 512)
    nq = seq // tile
    hps = DIFF_HEADS_PER_STEP
    width = hps * 2 * DIFF_QK_DIM
    assert DIFF_HEADS % hps == 0 and DIFF_V_DIM == 2 * DIFF_QK_DIM and tile % SUB == 0
    kb, vb = C_DK // width, C_DV // width
    slopes2 = _alibi_slopes(DIFF_HEADS) * np.float32(LOG2E)
    _assert_slopes_fit(slopes2)
    slopes = jnp.asarray(slopes2)
    return pl.pallas_call(
        functools.partial(_diff_kernel, tile=tile, n_tiles=nq, lam_init=lam_init),
        grid=(batch, DIFF_HEADS // hps, nq),
        in_specs=[
            pl.BlockSpec(memory_space=pltpu.SMEM),
            pl.BlockSpec((tile, width), lambda b, h, i: (b * nq + i, h)),
            pl.BlockSpec((seq, width), lambda b, h, i: (b, kb + h)),
            pl.BlockSpec((seq, width), lambda b, h, i: (b, vb + h)),
            pl.BlockSpec((4, DIFF_QK_DIM), lambda b, h, i: (0, 0)),
            pl.BlockSpec((1, DIFF_V_DIM), lambda b, h, i: (0, 0)),
        ],
        out_specs=pl.BlockSpec((tile, width), lambda b, h, i: (b * nq + i, h)),
        out_shape=jax.ShapeDtypeStruct((t, DIFF_HEADS * DIFF_V_DIM), BF16),
        scratch_shapes=[
            pltpu.VMEM((hps, nq, DIFF_V_DIM + ROWS_L, tile), BF16),
            pltpu.VMEM((2, tile, tile), F32),
            pltpu.VMEM((2, 2 * hps, tile, tile), F32),
            pltpu.VMEM((2, 2 * hps, 1, tile), F32),
            pltpu.VMEM((2, 2 * hps, 1, tile), F32),
            pltpu.VMEM((2 * hps, 1, tile), F32),
            pltpu.VMEM((2 * hps, DIFF_V_DIM + ROWS_L, tile), F32),
        ],
        compiler_params=_params("parallel", "parallel", "arbitrary"),
        name="diff_attn",
    )(slopes, proj, proj, proj, lp, sg)


def _compress_kernel(kv_ref, pos_ref, w1_ref, w2_ref, o_ref, ot_ref):
    half = w1_ref.shape[0] // 2
    kv = kv_ref[...]
    top = _dot(kv, w1_ref[:half, :])
    bot = _dot(kv, w1_ref[half:, :])
    rows = kv.shape[0]
    bot_next = pltpu.roll(bot, rows - 1, 0)
    pos_term = _dot(pos_ref[...], w1_ref[...])[0:1]
    hid = jax.nn.gelu(top + bot_next + pos_term)
    out = _dot(hid.astype(BF16), w2_ref[...])
    o_ref[...] = out.astype(BF16)
    ot_ref[...] = out.T.astype(BF16)


def _compress(kvr, pos, w1, w2):
    _, groups, batch, rows, width = kvr.shape
    d = w2.shape[-1]
    return pl.pallas_call(
        _compress_kernel,
        grid=(2, groups, batch),
        in_specs=[
            pl.BlockSpec((None, None, None, rows, width), lambda a, g, b: (a, g, b, 0, 0)),
            pl.BlockSpec((None, 8, 2 * width), lambda a, g, b: (a, 0, 0)),
            pl.BlockSpec((None, 2 * width, d), lambda a, g, b: (a, 0, 0)),
            pl.BlockSpec((None, d, d), lambda a, g, b: (a, 0, 0)),
        ],
        out_specs=[
            pl.BlockSpec((None, None, None, rows, d), lambda a, g, b: (a, b, g, 0, 0)),
            pl.BlockSpec((None, None, None, d, rows), lambda a, g, b: (a, b, g, 0, 0)),
        ],
        out_shape=[
            jax.ShapeDtypeStruct((2, batch, groups, rows, d), BF16),
            jax.ShapeDtypeStruct((2, batch, groups, d, rows), BF16),
        ],
        compiler_params=_params("parallel", "parallel", "parallel"),
        name="compress",
    )(kvr, pos, w1, w2)


def _nsa_kernel(slopes_ref, q_ref, gate_ref, kc_ref, vct_ref, ks_ref, vs_ref, kw_ref, vw_ref,
                ovt_ref, o_ref, vst_ref, vwt_ref, base_ref, z_ref, mrow_ref, alpha_ref, m_ref, acc_ref,
                *, tq, tk, seq):
    gb = pl.program_id(1)
    qi = pl.program_id(2)
    q0 = qi * tq
    hp = NSA_HPG
    gps = NSA_GROUPS_PER_STEP
    n = hp * tq
    dv = NSA_DIM
    span = WIN_LEN + tq
    n_slc = seq // SLC_LEN

    def head(gg, h):
        return (gb * gps + gg) * hp + h

    def cols(gg):
        return slice(gg * NSA_DIM, (gg + 1) * NSA_DIM)

    slope_row = [jnp.concatenate([jnp.full((1, tq), slopes_ref[head(gg, h)], F32) for h in range(hp)],
                                 axis=1) for gg in range(gps)]

    @pl.when(qi == 0)
    def _():
        ones_s = jnp.ones((1, tk), F32)
        ones_w = jnp.ones((1, tq), F32)
        key_row = lax.broadcasted_iota(jnp.int32, (span, tq), 0).astype(F32)
        for gg in range(gps):
            for t in range(seq // tk):
                vst_ref[gg, t] = _scaled_vt(vs_ref[t * tk:(t + 1) * tk, cols(gg)].astype(F32).T, ones_s)
            for t in range(seq // tq):
                vwt_ref[gg, t] = _scaled_vt(vw_ref[t * tq:(t + 1) * tq, cols(gg)].astype(F32).T, ones_w)
            for h in range(hp):
                base_ref[gg * hp + h] = slopes_ref[head(gg, h)] * key_row

    qpos_all = q0 + (lax.broadcasted_iota(jnp.int32, (1, n), 1) & (tq - 1))
    qpos = q0 + lax.broadcasted_iota(jnp.int32, (1, tq), 1)
    wt = jnp.maximum(qi - WIN_LEN // tq, 0)
    w0 = pl.multiple_of(wt * tq, tq)

    def front(gg):
        q = q_ref[:, gg * hp * NSA_DIM:(gg + 1) * hp * NSA_DIM]
        q4t = jnp.concatenate(
            [q[:, h * NSA_DIM:(h + 1) * NSA_DIM].astype(F32).T for h in range(hp)], axis=1).astype(BF16)

        n_cmp = kc_ref.shape[1]
        cmp_end = lax.broadcasted_iota(jnp.int32, (n_cmp, 1), 0) * CMP_STRIDE + (CMP_LEN - 1)
        mask_c = cmp_end <= qpos_all
        z_c = jnp.where(mask_c, _dot(kc_ref[gg], q4t), NEG_INF)
        m_c = jnp.max(z_c, axis=0, keepdims=True)
        e_c = jnp.where(mask_c, jnp.exp2(z_c - m_c), 0.0)
        p_c = e_c * (1.0 / jnp.maximum(jnp.sum(e_c, axis=0, keepdims=True), 1e-30))
        o_c = _dot(vct_ref[gg], p_c.astype(BF16))

        dist = qpos - (w0 + lax.broadcasted_iota(jnp.int32, (span, 1), 0))
        in_win = (dist >= 0) & (dist < WIN_LEN)
        pen_w = jnp.concatenate(
            [jnp.where(in_win, base_ref[gg * hp + h], NEG_INF) for h in range(hp)], axis=1)
        z_w = _dot(kw_ref[pl.ds(w0, span), cols(gg)], q4t) + pen_w
        p_w = jnp.exp2(z_w - jnp.max(z_w, axis=0, keepdims=True)).astype(BF16)
        vw_cat = jnp.concatenate([vwt_ref[gg, wt + j] for j in range(span // tq)], axis=1)
        o_w = _split_acc(_dot(vw_cat, p_w), dv)

        p_sum = p_c[:, 0:tq]
        for h in range(1, hp):
            p_sum = p_sum + p_c[:, h * tq:(h + 1) * tq]
        ovt = ovt_ref[...]
        p_hi = p_sum.astype(BF16)
        r1 = p_sum - p_hi.astype(F32)
        p_mid = r1.astype(BF16)
        p_lo = (r1 - p_mid.astype(F32)).astype(BF16)
        imp = _dot(ovt, p_hi) + _dot(ovt, p_mid) + _dot(ovt, p_lo)
        jblk = lax.broadcasted_iota(jnp.int32, (n_slc, tq), 0)
        cur = qpos // SLC_LEN
        forced = (jblk == 0) | (jblk == cur) | (jblk == cur - 1)
        imp = jnp.where(forced, SEL_BIG, jnp.where(jblk > cur, -SEL_BIG, imp))
        jblk_f = jblk.astype(F32)
        sel = jnp.zeros((n_slc, tq), F32)
        for _ in range(min(SLC_TOPK, n_slc)):
            mx = jnp.max(imp, axis=0, keepdims=True)
            first = jnp.min(jnp.where(imp == mx, jblk_f, float(n_slc)), axis=0, keepdims=True)
            hit = jblk_f == first
            sel = jnp.where(hit, 1.0, sel)
            imp = jnp.where(hit, -jnp.inf, imp)
        return q4t, sel.astype(BF16), o_c, o_w

    fronts = [front(gg) for gg in range(gps)]

    m_ref[...] = jnp.full_like(m_ref, NEG_INF)
    acc_ref[...] = jnp.zeros_like(acc_ref)
    exp_blk = lax.broadcasted_iota(jnp.int32, (tk, n_slc), 0) // SLC_LEN
    exp_col = lax.broadcasted_iota(jnp.int32, (tk, n_slc), 1)
    krow = lax.broadcasted_iota(jnp.int32, (tk, 1), 0)
    assert hp == 4
    half = 2 * tq

    chains = range(2 * gps)

    def tile_off(t, c):
        gg, i = divmod(c, 2)
        return slope_row[gg][:, i * half:(i + 1) * half] * (t * tk - q0).astype(F32)

    def penalty(t):
        expand = jnp.where(exp_col == exp_blk + t * (tk // SLC_LEN), 1.0, 0.0).astype(BF16)
        causal = (krow + t * tk) <= qpos
        pens = []
        for gg in range(gps):
            keep = (_dot(expand, fronts[gg][1]) > 0.5) & causal
            pens += [jnp.concatenate([jnp.where(keep, base_ref[gg * hp + 2 * i + j, 0:tk, :], NEG_INF)
                                      for j in range(2)], axis=1) for i in range(2)]
        return pens

    def scores(t, c, pen):
        gg, i = divmod(c, 2)
        ks = pl.multiple_of(t * tk, tk)
        return _dot(ks_ref[pl.ds(ks, tk), cols(gg)], fronts[gg][0][:, i * half:(i + 1) * half]) + pen[c]

    def bookkeep(t, slot, c, z):
        z_ref[slot, c] = z
        m_old = m_ref[c]
        m_new = jnp.maximum(m_old, jnp.max(z, axis=0, keepdims=True) + tile_off(t, c))
        m_ref[c] = m_new
        mrow_ref[slot, c] = m_new
        alpha_ref[slot, c] = jnp.exp2(m_old - m_new)

    def weights(t, slot, c):
        return jnp.exp2(z_ref[slot, c] - (mrow_ref[slot, c] - tile_off(t, c))).astype(BF16)

    def accumulate(t, slot, c, p):
        acc_ref[c] = alpha_ref[slot, c] * acc_ref[c] + _dot(vst_ref[c // 2, t], p)

    def overlapped(t, slot):
        pen = penalty(t + 1)
        for c in chains:
            z_next = scores(t + 1, c, pen)
            p = weights(t, slot, c)
            bookkeep(t + 1, 1 - slot, c, z_next)
            accumulate(t, slot, c, p)

    def drain(t, slot):
        for c in chains:
            accumulate(t, slot, c, weights(t, slot, c))

    pen = penalty(0)
    for c in chains:
        bookkeep(0, 0, c, scores(0, c, pen))
    n_last = q0 // tk

    def body(u, carry):
        overlapped(2 * u, 0)
        overlapped(2 * u + 1, 1)
        return carry

    lax.fori_loop(0, n_last // 2, body, 0)

    @pl.when(n_last % 2 == 1)
    def _():
        overlapped(n_last - 1, 0)
        drain(n_last, 1)

    @pl.when(n_last % 2 == 0)
    def _():
        drain(n_last, 0)

    outs = []
    for gg in range(gps):
        _, _, o_c, o_w = fronts[gg]
        o_s = jnp.concatenate([_split_acc(acc_ref[2 * gg + i], dv) for i in range(2)], axis=1)

        def gate_row(branch):
            return jnp.concatenate(
                [gate_ref[pl.ds(head(gg, h) * 3 + branch, 1), :] for h in range(hp)], axis=1)

        ot = gate_row(0) * o_c + gate_row(1) * o_s + gate_row(2) * o_w
        outs += [ot[:, h * tq:(h + 1) * tq].T for h in range(hp)]
    o_ref[...] = jnp.concatenate(outs, axis=1).astype(BF16)


def _nsa_attention(proj, gates_t, k_cmp, vt_cmp, overlap_t, batch, seq):
    t = proj.shape[0]
    tq = _pick(seq, 256)
    tk = _pick(seq, 512)
    assert seq >= WIN_LEN + tq and WIN_LEN % tq == 0 and tk % tq == 0
    nq = seq // tq
    hp = NSA_HPG
    qb = C_NQ // (hp * NSA_DIM)
    kvb = C_NKV // LANES
    n_cmp = k_cmp.shape[3]
    slopes2 = _alibi_slopes(NSA_HEADS) * np.float32(LOG2E)
    _assert_slopes_fit(slopes2)
    slopes = jnp.asarray(slopes2)

    gps = NSA_GROUPS_PER_STEP
    assert NSA_GROUPS % gps == 0 and qb % gps == 0 and kvb % gps == 0

    def kv_spec(branch, which):
        base = (kvb + (branch * 2 + which) * NSA_GROUPS) // gps
        return pl.BlockSpec((seq, gps * LANES), lambda b, g, i: (b, base + g))

    return pl.pallas_call(
        functools.partial(_nsa_kernel, tq=tq, tk=tk, seq=seq),
        grid=(batch, NSA_GROUPS // gps, nq),
        in_specs=[
            pl.BlockSpec(memory_space=pltpu.SMEM),
            pl.BlockSpec((tq, gps * hp * NSA_DIM), lambda b, g, i: (b * nq + i, qb // gps + g)),
            pl.BlockSpec((LANES, tq), lambda b, g, i: (0, b * nq + i)),
            pl.BlockSpec((None, None, gps, n_cmp, NSA_DIM), lambda b, g, i: (0, b, g, 0, 0)),
            pl.BlockSpec((None, None, gps, NSA_DIM, n_cmp), lambda b, g, i: (1, b, g, 0, 0)),
            kv_spec(1, 0), kv_spec(1, 1),
            kv_spec(2, 0), kv_spec(2, 1),
            pl.BlockSpec(overlap_t.shape, lambda b, g, i: (0, 0)),
        ],
        out_specs=pl.BlockSpec((tq, gps * hp * NSA_DIM), lambda b, g, i: (b * nq + i, g)),
        out_shape=jax.ShapeDtypeStruct((t, NSA_HEADS * NSA_DIM), BF16),
        scratch_shapes=[
            pltpu.VMEM((gps, seq // tk, NSA_DIM + ROWS_L, tk), BF16),
            pltpu.VMEM((gps, seq // tq, NSA_DIM + ROWS_L, tq), BF16),
            pltpu.VMEM((gps * hp, WIN_LEN + tq, tq), F32),
            pltpu.VMEM((2, 2 * gps, tk, 2 * tq), F32),
            pltpu.VMEM((2, 2 * gps, 1, 2 * tq), F32),
            pltpu.VMEM((2, 2 * gps, 1, 2 * tq), F32),
            pltpu.VMEM((2 * gps, 1, 2 * tq), F32),
            pltpu.VMEM((2 * gps, NSA_DIM + ROWS_L, 2 * tq), F32),
        ],
        compiler_params=_params("parallel", "parallel", "arbitrary"),
        name="nsa_attn",
    )(slopes, proj, gates_t, k_cmp, vt_cmp, proj, proj, proj, proj, overlap_t)


def _memkv_kernel(mem_ref, g_ref, w_ref, o_ref):
    mn = _rmsnorm(mem_ref[...], g_ref[...]).astype(BF16)
    o_ref[...] = _dot(mn, w_ref[...]).astype(BF16)


def _memkv(mem, g, w):
    r, d = mem.shape
    n = w.shape[1]
    tm = _pick(r, 256)
    return pl.pallas_call(
        _memkv_kernel,
        grid=(r // tm,),
        in_specs=[
            pl.BlockSpec((tm, d), lambda i: (i, 0)),
            pl.BlockSpec((1, d), lambda i: (0, 0)),
            pl.BlockSpec((d, n), lambda i: (0, 0)),
        ],
        out_specs=pl.BlockSpec((tm, n), lambda i: (i, 0)),
        out_shape=jax.ShapeDtypeStruct((r, n), BF16),
        compiler_params=_params("parallel"),
        name="mem_kv",
    )(mem, g, w)


def _memattn_kernel(q_ref, k_ref, v_ref, o_ref):
    scale = MEM_DIM ** -0.5
    q = q_ref[...]
    k = k_ref[...]
    v = v_ref[...]
    outs = []
    for h in range(MEM_HEADS):
        sl = slice(h * MEM_DIM, (h + 1) * MEM_DIM)
        s = _dot_nt(q[:, sl], k[:, sl]) * scale
        m = jnp.max(s, axis=-1, keepdims=True)
        e = jnp.exp(s - m)
        p = e / jnp.sum(e, axis=-1, keepdims=True)
        outs.append(_dot(p.astype(BF16), v[:, sl]))
    o_ref[...] = jnp.concatenate(outs, axis=-1).astype(BF16)


def _mem_attention(proj, memkv, batch, seq, mem_len):
    t = proj.shape[0]
    tq = _pick(seq, 2048)
    nq = seq // tq
    width = MEM_HEADS * MEM_DIM
    qb = P_MQ // width
    return pl.pallas_call(
        _memattn_kernel,
        grid=(batch, nq),
        in_specs=[
            pl.BlockSpec((tq, width), lambda b, i: (b * nq + i, qb)),
            pl.BlockSpec((mem_len, width), lambda b, i: (b, 0)),
            pl.BlockSpec((mem_len, width), lambda b, i: (b, 1)),
        ],
        out_specs=pl.BlockSpec((tq, width), lambda b, i: (b * nq + i, 0)),
        out_shape=jax.ShapeDtypeStruct((t, width), BF16),
        compiler_params=_params("parallel", "parallel"),
        name="mem_attn",
    )(proj, memkv, memkv)


def _merge_kernel(x_ref, hn_ref, od_ref, on_ref, om_ref, wgd_ref, wgn_ref, wgm_ref,
                  wud_ref, wun_ref, wum_ref, wo_ref, o_ref, acc_ref):
    j = pl.program_id(1)

    @pl.when(j == 0)
    def _():
        acc_ref[...] = jnp.zeros_like(acc_ref)

    hn = hn_ref[...]
    merged = jax.nn.sigmoid(_dot_nt(hn, wgd_ref[...])) * _dot(od_ref[...], wud_ref[...])
    merged += jax.nn.sigmoid(_dot_nt(hn, wgn_ref[...])) * _dot(on_ref[...], wun_ref[...])
    merged += jax.nn.sigmoid(_dot_nt(hn, wgm_ref[...])) * _dot(om_ref[...], wum_ref[...])
    acc_ref[...] += _dot(merged.astype(BF16), wo_ref[...])

    @pl.when(j == pl.num_programs(1) - 1)
    def _():
        o_ref[...] = x_ref[...] + acc_ref[...]


def _merge(x, hn, od, on, om, tail, layer, wud, wun, wum, wo):
    t, d = x.shape
    tm = _pick(t, 512)
    tj = _pick(d, 512)
    assert MEM_Q_COLS % tj == 0

    def gate_spec(branch):
        first = (MEM_Q_COLS + branch * d) // tj
        return pl.BlockSpec((None, tj, d), lambda i, j: (layer, first + j, 0))

    return pl.pallas_call(
        _merge_kernel,
        grid=(t // tm, d // tj),
        in_specs=[
            pl.BlockSpec((tm, d), lambda i, j: (i, 0)),
            pl.BlockSpec((tm, d), lambda i, j: (i, 0)),
            pl.BlockSpec((tm, od.shape[1]), lambda i, j: (i, 0)),
            pl.BlockSpec((tm, on.shape[1]), lambda i, j: (i, 0)),
            pl.BlockSpec((tm, om.shape[1]), lambda i, j: (i, 0)),
            gate_spec(0), gate_spec(1), gate_spec(2),
            pl.BlockSpec((wud.shape[0], tj), lambda i, j: (0, j)),
            pl.BlockSpec((wun.shape[0], tj), lambda i, j: (0, j)),
            pl.BlockSpec((wum.shape[0], tj), lambda i, j: (0, j)),
            pl.BlockSpec((tj, d), lambda i, j: (j, 0)),
        ],
        out_specs=pl.BlockSpec((tm, d), lambda i, j: (i, 0)),
        out_shape=jax.ShapeDtypeStruct((t, d), F32),
        scratch_shapes=[pltpu.VMEM((tm, d), F32)],
        compiler_params=_params("parallel", "arbitrary"),
        name="merge",
    )(x, hn, od, on, om, tail, tail, tail, wud, wun, wum, wo)


def _overlap_matrix_t(n_slc, n_cmp):
    cmp_start = np.arange(n_cmp)[None, :] * CMP_STRIDE
    slc_start = np.arange(n_slc)[:, None] * SLC_LEN
    return ((cmp_start < slc_start + SLC_LEN) & (cmp_start + CMP_LEN > slc_start)).astype(np.float32)


def kernel(x, mem, ffn1_norm, ffn1_w_gate, ffn1_w_up, ffn1_w_down, mix_norm, w_in, diff_lambda, diff_subln, nsa_cmp_pos, nsa_cmp_w1, nsa_cmp_w2, mem_norm, w_mem_kv, w_up_diff, w_up_nsa, w_up_mem, w_out, ffn2_norm, ffn2_w_gate, ffn2_w_up, ffn2_w_down, final_norm):
    batch, seq, d = x.shape
    mem_len = mem.shape[1]
    depth = w_in.shape[0]
    t = batch * seq
    assert seq % CMP_STRIDE == 0 and CMP_LEN == 2 * CMP_STRIDE
    n_rows = seq // CMP_STRIDE
    n_slc = seq // SLC_LEN
    overlap_t = jnp.asarray(_overlap_matrix_t(n_slc, n_rows), BF16)

    col_scale = np.ones((1, P_COLS), np.float32)
    col_scale[0, C_DQ:C_DK] = (DIFF_QK_DIM ** -0.5) * LOG2E
    col_scale[0, C_NQ:C_NKV] = (NSA_DIM ** -0.5) * LOG2E
    col_scale = jnp.asarray(col_scale)

    w_in_t = jnp.swapaxes(w_in, 1, 2)
    w_tail, w_ng = _tail_weights(w_in_t)

    xt = x.reshape(t, d)
    memt = mem.reshape(batch * mem_len, d)
    fg = final_norm.reshape(1, d)
    bf = lambda a: a.astype(BF16)

    for l in range(depth):
        lam_init = 0.8 - 0.6 * math.exp(-0.3 * l)
        xt, hn = _ffn(xt, ffn1_norm[l].reshape(1, d), ffn1_w_gate, ffn1_w_up, ffn1_w_down,
                      mix_norm[l].reshape(1, d), l, "normed")

        proj, ngt = _inproj(hn, w_in_t, l, w_tail, w_ng, col_scale)

        o_diff = _diff_attention(proj, diff_lambda[l], diff_subln[l].reshape(1, DIFF_V_DIM),
                                 batch, seq, lam_init)

        ckv = proj[:, C_NKV:C_NKV + 2 * NSA_GROUPS * NSA_DIM]
        ckv = ckv.reshape(batch, n_rows, CMP_STRIDE, 2, NSA_GROUPS, NSA_DIM)
        ckv = ckv.transpose(3, 4, 0, 1, 2, 5).reshape(2, NSA_GROUPS, batch, n_rows, CMP_STRIDE * NSA_DIM)
        pos = bf(jnp.broadcast_to(nsa_cmp_pos[l].reshape(2, 1, CMP_LEN * NSA_DIM), (2, 8, CMP_LEN * NSA_DIM)))
        kv_cmp, kvt_cmp = _compress(ckv, pos, bf(nsa_cmp_w1[l]), bf(nsa_cmp_w2[l]))
        o_nsa = _nsa_attention(proj, ngt, kv_cmp, kvt_cmp, overlap_t, batch, seq)

        memkv = _memkv(memt, mem_norm[l].reshape(1, d), bf(w_mem_kv[l]))
        o_mem = _mem_attention(proj, memkv, batch, seq, mem_len)

        xt = _merge(xt, hn, o_diff, o_nsa, o_mem, w_tail, l, bf(w_up_diff[l]), bf(w_up_nsa[l]),
                    bf(w_up_mem[l]), bf(w_out[l]))

        (xt,) = _ffn(xt, ffn2_norm[l].reshape(1, d), ffn2_w_gate, ffn2_w_up, ffn2_w_down, fg, l,
                     "final" if l == depth - 1 else "plain")
    return xt.reshape(batch, seq, d)
```

```python
import functools
import math

import jax
import jax.numpy as jnp
import numpy as np
from jax import lax
from jax.experimental import pallas as pl
from jax.experimental.pallas import tpu as pltpu

F32 = jnp.float32
BF16 = jnp.bfloat16

NORM_EPS = 1e-6
NEG_INF = -1e30
SEL_BIG = 1e9
LOG2E = math.log2(math.e)

DIFF_HEADS = 8
DIFF_QK_DIM = 64
DIFF_V_DIM = 2 * DIFF_QK_DIM

NSA_HEADS = 8
NSA_GROUPS = 2
NSA_HPG = NSA_HEADS // NSA_GROUPS
NSA_DIM = 128
CMP_LEN = 32
CMP_STRIDE = 16
SLC_LEN = 64
SLC_TOPK = 8
WIN_LEN = 512

MEM_HEADS = 4
MEM_DIM = 128
N_BRANCH = 3

LANES = 128
VMEM_LIMIT = 56 * 1024 * 1024
DIFF_HEADS_PER_STEP = 4
NSA_GROUPS_PER_STEP = 2

DIFF_QK_COLS = DIFF_HEADS * 2 * DIFF_QK_DIM
DIFF_V_COLS = DIFF_HEADS * DIFF_V_DIM
NSA_Q_COLS = NSA_HEADS * NSA_DIM
NSA_KV_COLS = 3 * 2 * NSA_GROUPS * NSA_DIM
NSA_GATE_COLS = NSA_HEADS * 3
MEM_Q_COLS = MEM_HEADS * MEM_DIM
C_DQ = 0
C_DK = C_DQ + DIFF_QK_COLS
C_DV = C_DK + DIFF_QK_COLS
C_NQ = C_DV + DIFF_V_COLS
C_NKV = C_NQ + NSA_Q_COLS
C_NG = C_NKV + NSA_KV_COLS
C_MQ = C_NG + NSA_GATE_COLS
C_MG = C_MQ + MEM_Q_COLS
P_MQ = C_NG
P_COLS = P_MQ + MEM_Q_COLS


def _alibi_slopes(n_heads):
    return np.array([2.0 ** (-8.0 * (h + 1) / n_heads) for h in range(n_heads)], dtype=np.float32)


def _rmsnorm(x, g):
    y = x * lax.rsqrt(jnp.mean(x * x, axis=-1, keepdims=True) + NORM_EPS)
    return y * g


def _dot(a, b):
    return jnp.dot(a, b, preferred_element_type=F32)


def _dot_nt(a, b):
    return lax.dot_general(a, b, (((1,), (1,)), ((), ())), preferred_element_type=F32)


def _transpose_bf16(x):
    return x.astype(F32).T.astype(BF16)


def _params(*sem):
    return pltpu.CompilerParams(dimension_semantics=sem, vmem_limit_bytes=VMEM_LIMIT)


def _pick(n, pref):
    t = min(n, pref)
    assert n % t == 0, (n, t)
    return t


def _ffn_kernel(x_ref, g_ref, wg_ref, wu_ref, wd_ref, fg_ref, o_ref, *rest, tail):
    xn_ref = rest[-1]
    j = pl.program_id(1)

    @pl.when(j == 0)
    def _():
        xn_ref[...] = _rmsnorm(x_ref[...], g_ref[...]).astype(BF16)
        o_ref[...] = jnp.zeros_like(o_ref)

    xn = xn_ref[...]
    a = _dot(xn, wg_ref[...].astype(BF16))
    u = _dot(xn, wu_ref[...].astype(BF16))
    h = (a * jax.nn.sigmoid(a)) * u
    o_ref[...] += _dot(h.astype(BF16), wd_ref[...].astype(BF16))

    @pl.when(j == pl.num_programs(1) - 1)
    def _():
        rows = 128

        def chunk(c, carry):
            r = pl.ds(pl.multiple_of(c * rows, rows), rows)
            y = x_ref[r, :] + 0.5 * o_ref[r, :]
            if tail == "final":
                y = _rmsnorm(y, fg_ref[...])
            if tail == "normed":
                rest[0][r, :] = _rmsnorm(y, fg_ref[...]).astype(BF16)
            o_ref[r, :] = y
            return carry

        lax.fori_loop(0, o_ref.shape[0] // rows, chunk, 0)


def _ffn(x, g, wg, wu, wd, fg, layer, tail):
    t, d = x.shape
    f = wg.shape[2]
    tm = _pick(t, 1024)
    tf = _pick(f, 256)
    out_specs = [pl.BlockSpec((tm, d), lambda i, j: (i, 0))]
    out_shape = [jax.ShapeDtypeStruct((t, d), F32)]
    if tail == "normed":
        out_specs.append(pl.BlockSpec((tm, d), lambda i, j: (i, 0)))
        out_shape.append(jax.ShapeDtypeStruct((t, d), BF16))
    return pl.pallas_call(
        functools.partial(_ffn_kernel, tail=tail),
        grid=(t // tm, f // tf),
        in_specs=[
            pl.BlockSpec((tm, d), lambda i, j: (i, 0)),
            pl.BlockSpec((1, d), lambda i, j: (0, 0)),
            pl.BlockSpec((None, d, tf), lambda i, j: (layer, 0, j)),
            pl.BlockSpec((None, d, tf), lambda i, j: (layer, 0, j)),
            pl.BlockSpec((None, tf, d), lambda i, j: (layer, j, 0)),
            pl.BlockSpec((1, d), lambda i, j: (0, 0)),
        ],
        out_specs=out_specs,
        out_shape=out_shape,
        scratch_shapes=[] if tail == "normed" else [pltpu.VMEM((tm, d), BF16)],
        compiler_params=_params("parallel", "arbitrary"),
        name="ffn",
    )(x, g, wg, wu, wd, fg)


def _tail_kernel(w_ref, tail_ref, ng_ref, carry_ref, *, shift):
    s = pl.program_id(1)
    blk = w_ref[...]

    @pl.when(s == 0)
    def _():
        pad = jnp.zeros((ng_ref.shape[0] - shift, blk.shape[1]), F32)
        ng_ref[...] = jnp.concatenate([blk[:shift], pad], axis=0).astype(BF16)

    @pl.when(s > 0)
    def _():
        tail_ref[...] = jnp.concatenate([carry_ref[...], blk[:shift]], axis=0).astype(BF16)

    carry_ref[...] = blk[shift:]


def _tail_weights(w_in_t):
    depth, rows, d = w_in_t.shape
    tb = MEM_Q_COLS
    shift = C_MQ - C_NG
    n_out = -(-(rows - C_MQ) // tb)
    assert C_NG % tb == 0 and shift % 8 == 0 and shift <= LANES
    first, last = C_NG // tb, (rows - 1) // tb
    return pl.pallas_call(
        functools.partial(_tail_kernel, shift=shift),
        grid=(depth, n_out + 1),
        in_specs=[pl.BlockSpec((None, tb, d), lambda l, s: (l, jnp.minimum(first + s, last), 0))],
        out_specs=[
            pl.BlockSpec((None, tb, d), lambda l, s: (l, jnp.maximum(s - 1, 0), 0)),
            pl.BlockSpec((None, LANES, d), lambda l, s: (l, 0, 0)),
        ],
        out_shape=[
            jax.ShapeDtypeStruct((depth, n_out * tb, d), BF16),
            jax.ShapeDtypeStruct((depth, LANES, d), BF16),
        ],
        scratch_shapes=[pltpu.VMEM((tb - shift, d), F32)],
        compiler_params=_params("parallel", "arbitrary"),
        name="tail_weights",
    )(w_in_t)


def _inproj_kernel(hn_ref, w_ref, wmq_ref, wng_ref, cs_ref, proj_ref, ngt_ref, *, n_main):
    j = pl.program_id(1)

    @pl.when(j == 0)
    def _():
        ngt_ref[...] = jax.nn.sigmoid(_dot_nt(wng_ref[...], hn_ref[...]))

    @pl.when(j < n_main)
    def _():
        proj_ref[...] = (_dot_nt(hn_ref[...], w_ref[...].astype(BF16)) * cs_ref[...]).astype(BF16)

    @pl.when(j >= n_main)
    def _():
        proj_ref[...] = (_dot_nt(hn_ref[...], wmq_ref[...]) * cs_ref[...]).astype(BF16)


def _inproj(hn, w_in_t, layer, tail, wng, cs):
    t, d = hn.shape
    n = P_COLS
    tm = _pick(t, 2048)
    tn = MEM_Q_COLS
    assert P_MQ % tn == 0
    n_main = P_MQ // tn
    return pl.pallas_call(
        functools.partial(_inproj_kernel, n_main=n_main),
        grid=(t // tm, n // tn),
        in_specs=[
            pl.BlockSpec((tm, d), lambda i, j: (i, 0)),
            pl.BlockSpec((None, tn, d), lambda i, j: (layer, jnp.minimum(j, n_main - 1), 0)),
            pl.BlockSpec((None, tn, d), lambda i, j: (layer, 0, 0)),
            pl.BlockSpec((None, LANES, d), lambda i, j: (layer, 0, 0)),
            pl.BlockSpec((1, tn), lambda i, j: (0, j)),
        ],
        out_specs=[
            pl.BlockSpec((tm, tn), lambda i, j: (i, j)),
            pl.BlockSpec((LANES, tm), lambda i, j: (0, i)),
        ],
        out_shape=[
            jax.ShapeDtypeStruct((t, n), BF16),
            jax.ShapeDtypeStruct((LANES, t), F32),
        ],
        compiler_params=_params("parallel", "arbitrary"),
        name="inproj",
    )(hn, w_in_t, tail, wng, cs)


SUB = 128
ROWS_L = 16


def _assert_slopes_fit(slopes2):
    assert float(np.max(slopes2)) * SUB < 100.0, "ALiBi factor per bias block leaves the safe f32 range"


def _bias_factor_row(slope2, width):
    r = lax.broadcasted_iota(jnp.int32, (1, width), 1) & (SUB - 1)
    return jnp.exp2(slope2 * r.astype(F32))


def _scaled_vt(vt, w_row):
    ones = jnp.broadcast_to(w_row, (ROWS_L, w_row.shape[1]))
    return jnp.concatenate([vt * w_row, ones], axis=0).astype(BF16)


def _split_acc(acc, dv):
    return acc[0:dv] * (1.0 / jnp.maximum(acc[dv:dv + 1], 1e-30))


def _diff_kernel(slopes_ref, q_ref, k_ref, v_ref, lp_ref, sg_ref, o_ref,
                 vt_ref, pen_ref, z_ref, mrow_ref, alpha_ref, m_ref, acc_ref, *, tile, n_tiles, lam_init):
    hb = pl.program_id(1)
    qi = pl.program_id(2)
    hps = DIFF_HEADS_PER_STEP
    chains = range(2 * hps)
    slope2 = [slopes_ref[hb * hps + hh] for hh in range(hps)]
    nb = tile // SUB
    dh = 2 * DIFF_QK_DIM

    @pl.when(qi == 0)
    def _():
        for hh in range(hps):
            w_row = _bias_factor_row(slope2[hh], tile)
            for t in range(n_tiles):
                v = v_ref[t * tile:(t + 1) * tile, hh * DIFF_V_DIM:(hh + 1) * DIFF_V_DIM]
                vt_ref[hh, t] = _scaled_vt(v.astype(F32).T, w_row)
        row = lax.broadcasted_iota(jnp.int32, (tile, tile), 0)
        col = lax.broadcasted_iota(jnp.int32, (tile, tile), 1)
        pen_ref[0] = jnp.zeros((tile, tile), F32)
        pen_ref[1] = jnp.where(row <= col, 0.0, NEG_INF)

    qts = []
    for hh in range(hps):
        qt = _transpose_bf16(q_ref[:, hh * dh:(hh + 1) * dh])
        feat = lax.broadcasted_iota(jnp.int32, qt.shape, 0)
        qts += [jnp.where((feat >= m * DIFF_QK_DIM) & (feat < (m + 1) * DIFF_QK_DIM), qt, 0.0).astype(BF16)
                for m in range(2)]

    m_ref[...] = jnp.full_like(m_ref, NEG_INF)
    acc_ref[...] = jnp.zeros_like(acc_ref)

    def block_off(t, c, j):
        s2 = slope2[c // 2]
        return s2 * ((t - qi) * tile).astype(F32) + j * (s2 * float(SUB))

    def scores(t, c):
        ks = pl.multiple_of(t * tile, tile)
        hh = c // 2
        k = k_ref[pl.ds(ks, tile), hh * dh:(hh + 1) * dh]
        return _dot(k, qts[c]) + pen_ref[(t == qi).astype(jnp.int32)]

    def bookkeep(t, slot, c, z):
        z_ref[slot, c] = z
        m_old = m_ref[c]
        m_new = m_old
        for j in range(nb):
            m_new = jnp.maximum(m_new, jnp.max(z[j * SUB:(j + 1) * SUB], axis=0, keepdims=True)
                                + block_off(t, c, j))
        m_ref[c] = m_new
        mrow_ref[slot, c] = m_new
        alpha_ref[slot, c] = jnp.exp2(m_old - m_new)

    def weights(t, slot, c):
        m_new = mrow_ref[slot, c]
        return jnp.concatenate(
            [jnp.exp2(z_ref[slot, c, j * SUB:(j + 1) * SUB, :] - (m_new - block_off(t, c, j))).astype(BF16)
             for j in range(nb)], axis=0)

    def accumulate(t, slot, c, p):
        acc_ref[c] = alpha_ref[slot, c] * acc_ref[c] + _dot(vt_ref[c // 2, t], p)

    def overlapped(t, slot):
        for c in chains:
            z_next = scores(t + 1, c)
            p = weights(t, slot, c)
            bookkeep(t + 1, 1 - slot, c, z_next)
            accumulate(t, slot, c, p)

    def drain(t, slot):
        for c in chains:
            accumulate(t, slot, c, weights(t, slot, c))

    for c in chains:
        bookkeep(0, 0, c, scores(0, c))

    def body(u, carry):
        overlapped(2 * u, 0)
        overlapped(2 * u + 1, 1)
        return carry

    lax.fori_loop(0, qi // 2, body, 0)

    @pl.when(qi % 2 == 1)
    def _():
        overlapped(qi - 1, 0)
        drain(qi, 1)

    @pl.when(qi % 2 == 0)
    def _():
        drain(qi, 0)

    lp = lp_ref[...]
    lam = (jnp.exp(jnp.sum(lp[0:1] * lp[1:2], axis=-1, keepdims=True))
           - jnp.exp(jnp.sum(lp[2:3] * lp[3:4], axis=-1, keepdims=True)) + lam_init)
    outs = []
    for hh in range(hps):
        o1 = _split_acc(acc_ref[2 * hh], DIFF_V_DIM)
        o2 = _split_acc(acc_ref[2 * hh + 1], DIFF_V_DIM)
        ot = o1 - lam * o2
        ot = ot * lax.rsqrt(jnp.mean(ot * ot, axis=0, keepdims=True) + NORM_EPS)
        outs.append(ot.T * sg_ref[...] * (1.0 - lam_init))
    o_ref[...] = jnp.concatenate(outs, axis=1).astype(BF16)


def _diff_attention(proj, lp, sg, batch, seq, lam_init):
    t = proj.shape[0]
    tile = _pick(seq, 512)
    nq = seq // tile
    hps = DIFF_HEADS_PER_STEP
    width = hps * 2 * DIFF_QK_DIM
    assert DIFF_HEADS % hps == 0 and DIFF_V_DIM == 2 * DIFF_QK_DIM and tile % SUB == 0
    kb, vb = C_DK // width, C_DV // width
    slopes2 = _alibi_slopes(DIFF_HEADS) * np.float32(LOG2E)
    _assert_slopes_fit(slopes2)
    slopes = jnp.asarray(slopes2)
    return pl.pallas_call(
        functools.partial(_diff_kernel, tile=tile, n_tiles=nq, lam_init=lam_init),
        grid=(batch, DIFF_HEADS // hps, nq),
        in_specs=[
            pl.BlockSpec(memory_space=pltpu.SMEM),
            pl.BlockSpec((tile, width), lambda b, h, i: (b * nq + i, h)),
            pl.BlockSpec((seq, width), lambda b, h, i: (b, kb + h)),
            pl.BlockSpec((seq, width), lambda b, h, i: (b, vb + h)),
            pl.BlockSpec((4, DIFF_QK_DIM), lambda b, h, i: (0, 0)),
            pl.BlockSpec((1, DIFF_V_DIM), lambda b, h, i: (0, 0)),
        ],
        out_specs=pl.BlockSpec((tile, width), lambda b, h, i: (b * nq + i, h)),
        out_shape=jax.ShapeDtypeStruct((t, DIFF_HEADS * DIFF_V_DIM), BF16),
        scratch_shapes=[
            pltpu.VMEM((hps, nq, DIFF_V_DIM + ROWS_L, tile), BF16),
            pltpu.VMEM((2, tile, tile), F32),
            pltpu.VMEM((2, 2 * hps, tile, tile), F32),
            pltpu.VMEM((2, 2 * hps, 1, tile), F32),
            pltpu.VMEM((2, 2 * hps, 1, tile), F32),
            pltpu.VMEM((2 * hps, 1, tile), F32),
            pltpu.VMEM((2 * hps, DIFF_V_DIM + ROWS_L, tile), F32),
        ],
        compiler_params=_params("parallel", "parallel", "arbitrary"),
        name="diff_attn",
    )(slopes, proj, proj, proj, lp, sg)


def _compress_kernel(kv_ref, pos_ref, w1_ref, w2_ref, o_ref, ot_ref, tok_ref):
    half = w1_ref.shape[0] // 2
    tok_ref[...] = kv_ref[...].astype(F32)
    n_rows = tok_ref.shape[0] // CMP_STRIDE
    kv = jnp.concatenate([tok_ref[pl.ds(l, n_rows, stride=CMP_STRIDE), :] for l in range(CMP_STRIDE)],
                         axis=1).astype(BF16)
    top = _dot(kv, w1_ref[:half, :])
    bot = _dot(kv, w1_ref[half:, :])
    rows = kv.shape[0]
    bot_next = pltpu.roll(bot, rows - 1, 0)
    pos_term = _dot(pos_ref[...], w1_ref[...])[0:1]
    hid = jax.nn.gelu(top + bot_next + pos_term)
    out = _dot(hid.astype(BF16), w2_ref[...])
    o_ref[...] = out.astype(BF16)
    ot_ref[...] = out.T.astype(BF16)


def _compress(proj, pos, w1, w2, batch, seq):
    groups = NSA_GROUPS
    d = w2.shape[-1]
    rows = seq // CMP_STRIDE
    width = CMP_STRIDE * d
    kvb = C_NKV // LANES
    return pl.pallas_call(
        _compress_kernel,
        grid=(2, groups, batch),
        in_specs=[
            pl.BlockSpec((seq, d), lambda a, g, b: (b, kvb + a * groups + g)),
            pl.BlockSpec((None, 8, 2 * width), lambda a, g, b: (a, 0, 0)),
            pl.BlockSpec((None, 2 * width, d), lambda a, g, b: (a, 0, 0)),
            pl.BlockSpec((None, d, d), lambda a, g, b: (a, 0, 0)),
        ],
        out_specs=[
            pl.BlockSpec((None, None, None, rows, d), lambda a, g, b: (a, b, g, 0, 0)),
            pl.BlockSpec((None, None, None, d, rows), lambda a, g, b: (a, b, g, 0, 0)),
        ],
        out_shape=[
            jax.ShapeDtypeStruct((2, batch, groups, rows, d), BF16),
            jax.ShapeDtypeStruct((2, batch, groups, d, rows), BF16),
        ],
        scratch_shapes=[pltpu.VMEM((seq, d), F32)],
        compiler_params=_params("parallel", "parallel", "parallel"),
        name="compress",
    )(proj, pos, w1, w2)


def _nsa_kernel(slopes_ref, q_ref, gate_ref, kc_ref, vct_ref, ks_ref, vs_ref, kw_ref, vw_ref,
                ovt_ref, o_ref, vst_ref, vwt_ref, base_ref, z_ref, mrow_ref, alpha_ref, m_ref, acc_ref,
                *, tq, tk, seq):
    gb = pl.program_id(1)
    qi = pl.program_id(2)
    q0 = qi * tq
    hp = NSA_HPG
    gps = NSA_GROUPS_PER_STEP
    n = hp * tq
    dv = NSA_DIM
    span = WIN_LEN + tq
    n_slc = seq // SLC_LEN

    def head(gg, h):
        return (gb * gps + gg) * hp + h

    def cols(gg):
        return slice(gg * NSA_DIM, (gg + 1) * NSA_DIM)

    slope_row = [jnp.concatenate([jnp.full((1, tq), slopes_ref[head(gg, h)], F32) for h in range(hp)],
                                 axis=1) for gg in range(gps)]

    @pl.when(qi == 0)
    def _():
        ones_s = jnp.ones((1, tk), F32)
        ones_w = jnp.ones((1, tq), F32)
        key_row = lax.broadcasted_iota(jnp.int32, (span, tq), 0).astype(F32)
        for gg in range(gps):
            for t in range(seq // tk):
                vst_ref[gg, t] = _scaled_vt(vs_ref[t * tk:(t + 1) * tk, cols(gg)].astype(F32).T, ones_s)
            for t in range(seq // tq):
                vwt_ref[gg, t] = _scaled_vt(vw_ref[t * tq:(t + 1) * tq, cols(gg)].astype(F32).T, ones_w)
            for h in range(hp):
                base_ref[gg * hp + h] = slopes_ref[head(gg, h)] * key_row

    qpos_all = q0 + (lax.broadcasted_iota(jnp.int32, (1, n), 1) & (tq - 1))
    qpos = q0 + lax.broadcasted_iota(jnp.int32, (1, tq), 1)
    wt = jnp.maximum(qi - WIN_LEN // tq, 0)
    w0 = pl.multiple_of(wt * tq, tq)

    def front(gg):
        q = q_ref[:, gg * hp * NSA_DIM:(gg + 1) * hp * NSA_DIM]
        q4t = jnp.concatenate(
            [q[:, h * NSA_DIM:(h + 1) * NSA_DIM].astype(F32).T for h in range(hp)], axis=1).astype(BF16)

        n_cmp = kc_ref.shape[1]
        cmp_end = lax.broadcasted_iota(jnp.int32, (n_cmp, 1), 0) * CMP_STRIDE + (CMP_LEN - 1)
        mask_c = cmp_end <= qpos_all
        z_c = jnp.where(mask_c, _dot(kc_ref[gg], q4t), NEG_INF)
        m_c = jnp.max(z_c, axis=0, keepdims=True)
        e_c = jnp.where(mask_c, jnp.exp2(z_c - m_c), 0.0)
        p_c = e_c * (1.0 / jnp.maximum(jnp.sum(e_c, axis=0, keepdims=True), 1e-30))
        o_c = _dot(vct_ref[gg], p_c.astype(BF16))

        dist = qpos - (w0 + lax.broadcasted_iota(jnp.int32, (span, 1), 0))
        in_win = (dist >= 0) & (dist < WIN_LEN)
        pen_w = jnp.concatenate(
            [jnp.where(in_win, base_ref[gg * hp + h], NEG_INF) for h in range(hp)], axis=1)
        z_w = _dot(kw_ref[pl.ds(w0, span), cols(gg)], q4t) + pen_w
        p_w = jnp.exp2(z_w - jnp.max(z_w, axis=0, keepdims=True)).astype(BF16)
        vw_cat = jnp.concatenate([vwt_ref[gg, wt + j] for j in range(span // tq)], axis=1)
        o_w = _split_acc(_dot(vw_cat, p_w), dv)

        p_sum = p_c[:, 0:tq]
        for h in range(1, hp):
            p_sum = p_sum + p_c[:, h * tq:(h + 1) * tq]
        ovt = ovt_ref[...]
        p_hi = p_sum.astype(BF16)
        r1 = p_sum - p_hi.astype(F32)
        p_mid = r1.astype(BF16)
        p_lo = (r1 - p_mid.astype(F32)).astype(BF16)
        imp = _dot(ovt, p_hi) + _dot(ovt, p_mid) + _dot(ovt, p_lo)
        jblk = lax.broadcasted_iota(jnp.int32, (n_slc, tq), 0)
        cur = qpos // SLC_LEN
        forced = (jblk == 0) | (jblk == cur) | (jblk == cur - 1)
        imp = jnp.where(forced, SEL_BIG, jnp.where(jblk > cur, -SEL_BIG, imp))
        jblk_f = jblk.astype(F32)
        sel = jnp.zeros((n_slc, tq), F32)
        for _ in range(min(SLC_TOPK, n_slc)):
            mx = jnp.max(imp, axis=0, keepdims=True)
            first = jnp.min(jnp.where(imp == mx, jblk_f, float(n_slc)), axis=0, keepdims=True)
            hit = jblk_f == first
            sel = jnp.where(hit, 1.0, sel)
            imp = jnp.where(hit, -jnp.inf, imp)
        return q4t, sel.astype(BF16), o_c, o_w

    fronts = [front(gg) for gg in range(gps)]

    m_ref[...] = jnp.full_like(m_ref, NEG_INF)
    acc_ref[...] = jnp.zeros_like(acc_ref)
    exp_blk = lax.broadcasted_iota(jnp.int32, (tk, n_slc), 0) // SLC_LEN
    exp_col = lax.broadcasted_iota(jnp.int32, (tk, n_slc), 1)
    krow = lax.broadcasted_iota(jnp.int32, (tk, 1), 0)
    assert hp == 4
    half = 2 * tq

    chains = range(2 * gps)

    def tile_off(t, c):
        gg, i = divmod(c, 2)
        return slope_row[gg][:, i * half:(i + 1) * half] * (t * tk - q0).astype(F32)

    def penalty(t):
        expand = jnp.where(exp_col == exp_blk + t * (tk // SLC_LEN), 1.0, 0.0).astype(BF16)
        causal = (krow + t * tk) <= qpos
        pens = []
        for gg in range(gps):
            keep = (_dot(expand, fronts[gg][1]) > 0.5) & causal
            pens += [jnp.concatenate([jnp.where(keep, base_ref[gg * hp + 2 * i + j, 0:tk, :], NEG_INF)
                                      for j in range(2)], axis=1) for i in range(2)]
        return pens

    def scores(t, c, pen):
        gg, i = divmod(c, 2)
        ks = pl.multiple_of(t * tk, tk)
        return _dot(ks_ref[pl.ds(ks, tk), cols(gg)], fronts[gg][0][:, i * half:(i + 1) * half]) + pen[c]

    def bookkeep(t, slot, c, z):
        z_ref[slot, c] = z
        m_old = m_ref[c]
        m_new = jnp.maximum(m_old, jnp.max(z, axis=0, keepdims=True) + tile_off(t, c))
        m_ref[c] = m_new
        mrow_ref[slot, c] = m_new
        alpha_ref[slot, c] = jnp.exp2(m_old - m_new)

    def weights(t, slot, c):
        return jnp.exp2(z_ref[slot, c] - (mrow_ref[slot, c] - tile_off(t, c))).astype(BF16)

    def accumulate(t, slot, c, p):
        acc_ref[c] = alpha_ref[slot, c] * acc_ref[c] + _dot(vst_ref[c // 2, t], p)

    def overlapped(t, slot):
        pen = penalty(t + 1)
        for c in chains:
            z_next = scores(t + 1, c, pen)
            p = weights(t, slot, c)
            bookkeep(t + 1, 1 - slot, c, z_next)
            accumulate(t, slot, c, p)

    def drain(t, slot):
        for c in chains:
            accumulate(t, slot, c, weights(t, slot, c))

    pen = penalty(0)
    for c in chains:
        bookkeep(0, 0, c, scores(0, c, pen))
    n_last = q0 // tk

    def body(u, carry):
        overlapped(2 * u, 0)
        overlapped(2 * u + 1, 1)
        return carry

    lax.fori_loop(0, n_last // 2, body, 0)

    @pl.when(n_last % 2 == 1)
    def _():
        overlapped(n_last - 1, 0)
        drain(n_last, 1)

    @pl.when(n_last % 2 == 0)
    def _():
        drain(n_last, 0)

    outs = []
    for gg in range(gps):
        _, _, o_c, o_w = fronts[gg]
        o_s = jnp.concatenate([_split_acc(acc_ref[2 * gg + i], dv) for i in range(2)], axis=1)

        def gate_row(branch):
            return jnp.concatenate(
                [gate_ref[pl.ds(head(gg, h) * 3 + branch, 1), :] for h in range(hp)], axis=1)

        ot = gate_row(0) * o_c + gate_row(1) * o_s + gate_row(2) * o_w
        outs += [ot[:, h * tq:(h + 1) * tq].T for h in range(hp)]
    o_ref[...] = jnp.concatenate(outs, axis=1).astype(BF16)


def _nsa_attention(proj, gates_t, k_cmp, vt_cmp, overlap_t, batch, seq):
    t = proj.shape[0]
    tq = _pick(seq, 256)
    tk = _pick(seq, 512)
    assert seq >= WIN_LEN + tq and WIN_LEN % tq == 0 and tk % tq == 0
    nq = seq // tq
    hp = NSA_HPG
    qb = C_NQ // (hp * NSA_DIM)
    kvb = C_NKV // LANES
    n_cmp = k_cmp.shape[3]
    slopes2 = _alibi_slopes(NSA_HEADS) * np.float32(LOG2E)
    _assert_slopes_fit(slopes2)
    slopes = jnp.asarray(slopes2)

    gps = NSA_GROUPS_PER_STEP
    assert NSA_GROUPS % gps == 0 and qb % gps == 0 and kvb % gps == 0

    def kv_spec(branch, which):
        base = (kvb + (branch * 2 + which) * NSA_GROUPS) // gps
        return pl.BlockSpec((seq, gps * LANES), lambda b, g, i: (b, base + g))

    return pl.pallas_call(
        functools.partial(_nsa_kernel, tq=tq, tk=tk, seq=seq),
        grid=(batch, NSA_GROUPS // gps, nq),
        in_specs=[
            pl.BlockSpec(memory_space=pltpu.SMEM),
            pl.BlockSpec((tq, gps * hp * NSA_DIM), lambda b, g, i: (b * nq + i, qb // gps + g)),
            pl.BlockSpec((LANES, tq), lambda b, g, i: (0, b * nq + i)),
            pl.BlockSpec((None, None, gps, n_cmp, NSA_DIM), lambda b, g, i: (0, b, g, 0, 0)),
            pl.BlockSpec((None, None, gps, NSA_DIM, n_cmp), lambda b, g, i: (1, b, g, 0, 0)),
            kv_spec(1, 0), kv_spec(1, 1),
            kv_spec(2, 0), kv_spec(2, 1),
            pl.BlockSpec(overlap_t.shape, lambda b, g, i: (0, 0)),
        ],
        out_specs=pl.BlockSpec((tq, gps * hp * NSA_DIM), lambda b, g, i: (b * nq + i, g)),
        out_shape=jax.ShapeDtypeStruct((t, NSA_HEADS * NSA_DIM), BF16),
        scratch_shapes=[
            pltpu.VMEM((gps, seq // tk, NSA_DIM + ROWS_L, tk), BF16),
            pltpu.VMEM((gps, seq // tq, NSA_DIM + ROWS_L, tq), BF16),
            pltpu.VMEM((gps * hp, WIN_LEN + tq, tq), F32),
            pltpu.VMEM((2, 2 * gps, tk, 2 * tq), F32),
            pltpu.VMEM((2, 2 * gps, 1, 2 * tq), F32),
            pltpu.VMEM((2, 2 * gps, 1, 2 * tq), F32),
            pltpu.VMEM((2 * gps, 1, 2 * tq), F32),
            pltpu.VMEM((2 * gps, NSA_DIM + ROWS_L, 2 * tq), F32),
        ],
        compiler_params=_params("parallel", "parallel", "arbitrary"),
        name="nsa_attn",
    )(slopes, proj, gates_t, k_cmp, vt_cmp, proj, proj, proj, proj, overlap_t)


def _memkv_kernel(mem_ref, g_ref, w_ref, o_ref):
    mn = _rmsnorm(mem_ref[...], g_ref[...]).astype(BF16)
    o_ref[...] = _dot(mn, w_ref[...]).astype(BF16)


def _memkv(mem, g, w):
    r, d = mem.shape
    n = w.shape[1]
    tm = _pick(r, 256)
    return pl.pallas_call(
        _memkv_kernel,
        grid=(r // tm,),
        in_specs=[
            pl.BlockSpec((tm, d), lambda i: (i, 0)),
            pl.BlockSpec((1, d), lambda i: (0, 0)),
            pl.BlockSpec((d, n), lambda i: (0, 0)),
        ],
        out_specs=pl.BlockSpec((tm, n), lambda i: (i, 0)),
        out_shape=jax.ShapeDtypeStruct((r, n), BF16),
        compiler_params=_params("parallel"),
        name="mem_kv",
    )(mem, g, w)


def _memattn_kernel(q_ref, k_ref, v_ref, o_ref):
    scale = MEM_DIM ** -0.5
    q = q_ref[...]
    k = k_ref[...]
    v = v_ref[...]
    outs = []
    for h in range(MEM_HEADS):
        sl = slice(h * MEM_DIM, (h + 1) * MEM_DIM)
        s = _dot_nt(q[:, sl], k[:, sl]) * scale
        m = jnp.max(s, axis=-1, keepdims=True)
        e = jnp.exp(s - m)
        p = e / jnp.sum(e, axis=-1, keepdims=True)
        outs.append(_dot(p.astype(BF16), v[:, sl]))
    o_ref[...] = jnp.concatenate(outs, axis=-1).astype(BF16)


def _mem_attention(proj, memkv, batch, seq, mem_len):
    t = proj.shape[0]
    tq = _pick(seq, 2048)
    nq = seq // tq
    width = MEM_HEADS * MEM_DIM
    qb = P_MQ // width
    return pl.pallas_call(
        _memattn_kernel,
        grid=(batch, nq),
        in_specs=[
            pl.BlockSpec((tq, width), lambda b, i: (b * nq + i, qb)),
            pl.BlockSpec((mem_len, width), lambda b, i: (b, 0)),
            pl.BlockSpec((mem_len, width), lambda b, i: (b, 1)),
        ],
        out_specs=pl.BlockSpec((tq, width), lambda b, i: (b * nq + i, 0)),
        out_shape=jax.ShapeDtypeStruct((t, width), BF16),
        compiler_params=_params("parallel", "parallel"),
        name="mem_attn",
    )(proj, memkv, memkv)


def _merge_kernel(x_ref, hn_ref, od_ref, on_ref, om_ref, wgd_ref, wgn_ref, wgm_ref,
                  wud_ref, wun_ref, wum_ref, wo_ref, o_ref, acc_ref):
    j = pl.program_id(1)

    @pl.when(j == 0)
    def _():
        acc_ref[...] = jnp.zeros_like(acc_ref)

    hn = hn_ref[...]
    merged = jax.nn.sigmoid(_dot_nt(hn, wgd_ref[...])) * _dot(od_ref[...], wud_ref[...])
    merged += jax.nn.sigmoid(_dot_nt(hn, wgn_ref[...])) * _dot(on_ref[...], wun_ref[...])
    merged += jax.nn.sigmoid(_dot_nt(hn, wgm_ref[...])) * _dot(om_ref[...], wum_ref[...])
    acc_ref[...] += _dot(merged.astype(BF16), wo_ref[...])

    @pl.when(j == pl.num_programs(1) - 1)
    def _():
        o_ref[...] = x_ref[...] + acc_ref[...]


def _merge(x, hn, od, on, om, tail, layer, wud, wun, wum, wo):
    t, d = x.shape
    tm = _pick(t, 512)
    tj = _pick(d, 512)
    assert MEM_Q_COLS % tj == 0

    def gate_spec(branch):
        first = (MEM_Q_COLS + branch * d) // tj
        return pl.BlockSpec((None, tj, d), lambda i, j: (layer, first + j, 0))

    return pl.pallas_call(
        _merge_kernel,
        grid=(t // tm, d // tj),
        in_specs=[
            pl.BlockSpec((tm, d), lambda i, j: (i, 0)),
            pl.BlockSpec((tm, d), lambda i, j: (i, 0)),
            pl.BlockSpec((tm, od.shape[1]), lambda i, j: (i, 0)),
            pl.BlockSpec((tm, on.shape[1]), lambda i, j: (i, 0)),
            pl.BlockSpec((tm, om.shape[1]), lambda i, j: (i, 0)),
            gate_spec(0), gate_spec(1), gate_spec(2),
            pl.BlockSpec((wud.shape[0], tj), lambda i, j: (0, j)),
            pl.BlockSpec((wun.shape[0], tj), lambda i, j: (0, j)),
            pl.BlockSpec((wum.shape[0], tj), lambda i, j: (0, j)),
            pl.BlockSpec((tj, d), lambda i, j: (j, 0)),
        ],
        out_specs=pl.BlockSpec((tm, d), lambda i, j: (i, 0)),
        out_shape=jax.ShapeDtypeStruct((t, d), F32),
        scratch_shapes=[pltpu.VMEM((tm, d), F32)],
        compiler_params=_params("parallel", "arbitrary"),
        name="merge",
    )(x, hn, od, on, om, tail, tail, tail, wud, wun, wum, wo)


def _overlap_matrix_t(n_slc, n_cmp):
    cmp_start = np.arange(n_cmp)[None, :] * CMP_STRIDE
    slc_start = np.arange(n_slc)[:, None] * SLC_LEN
    return ((cmp_start < slc_start + SLC_LEN) & (cmp_start + CMP_LEN > slc_start)).astype(np.float32)


def kernel(x, mem, ffn1_norm, ffn1_w_gate, ffn1_w_up, ffn1_w_down, mix_norm, w_in, diff_lambda, diff_subln, nsa_cmp_pos, nsa_cmp_w1, nsa_cmp_w2, mem_norm, w_mem_kv, w_up_diff, w_up_nsa, w_up_mem, w_out, ffn2_norm, ffn2_w_gate, ffn2_w_up, ffn2_w_down, final_norm):
    batch, seq, d = x.shape
    mem_len = mem.shape[1]
    depth = w_in.shape[0]
    t = batch * seq
    assert seq % CMP_STRIDE == 0 and CMP_LEN == 2 * CMP_STRIDE
    n_rows = seq // CMP_STRIDE
    n_slc = seq // SLC_LEN
    overlap_t = jnp.asarray(_overlap_matrix_t(n_slc, n_rows), BF16)

    col_scale = np.ones((1, P_COLS), np.float32)
    col_scale[0, C_DQ:C_DK] = (DIFF_QK_DIM ** -0.5) * LOG2E
    col_scale[0, C_NQ:C_NKV] = (NSA_DIM ** -0.5) * LOG2E
    col_scale = jnp.asarray(col_scale)

    w_in_t = jnp.swapaxes(w_in, 1, 2)
    w_tail, w_ng = _tail_weights(w_in_t)

    xt = x.reshape(t, d)
    memt = mem.reshape(batch * mem_len, d)
    fg = final_norm.reshape(1, d)
    bf = lambda a: a.astype(BF16)

    for l in range(depth):
        lam_init = 0.8 - 0.6 * math.exp(-0.3 * l)
        xt, hn = _ffn(xt, ffn1_norm[l].reshape(1, d), ffn1_w_gate, ffn1_w_up, ffn1_w_down,
                      mix_norm[l].reshape(1, d), l, "normed")

        proj, ngt = _inproj(hn, w_in_t, l, w_tail, w_ng, col_scale)

        o_diff = _diff_attention(proj, diff_lambda[l], diff_subln[l].reshape(1, DIFF_V_DIM),
                                 batch, seq, lam_init)

        pos = bf(jnp.broadcast_to(nsa_cmp_pos[l].reshape(2, 1, CMP_LEN * NSA_DIM), (2, 8, CMP_LEN * NSA_DIM)))
        kv_cmp, kvt_cmp = _compress(proj, pos, bf(nsa_cmp_w1[l]), bf(nsa_cmp_w2[l]), batch, seq)
        o_nsa = _nsa_attention(proj, ngt, kv_cmp, kvt_cmp, overlap_t, batch, seq)

        memkv = _memkv(memt, mem_norm[l].reshape(1, d), bf(w_mem_kv[l]))
        o_mem = _mem_attention(proj, memkv, batch, seq, mem_len)

        xt = _merge(xt, hn, o_diff, o_nsa, o_mem, w_tail, l, bf(w_up_diff[l]), bf(w_up_nsa[l]),
                    bf(w_up_mem[l]), bf(w_out[l]))

        (xt,) = _ffn(xt, ffn2_norm[l].reshape(1, d), ffn2_w_gate, ffn2_w_up, ffn2_w_down, fg, l,
                     "final" if l == depth - 1 else "plain")
    return xt.reshape(batch, seq, d)
```
